```python
import jax, jax.numpy as jnp
from jax import lax
import numpy as np

D_MODEL = 1024
BATCH = 8
SEQ = 2048
DEPTH = 1

CHUNK = 64
N_MEM = 256
HEAD_DIM = 64
D_MIX = D_MODEL
FOX_HEADS = D_MIX // 2 // HEAD_DIM
CHK_HEADS = D_MIX // 2 // HEAD_DIM
D_FOX = FOX_HEADS * HEAD_DIM
D_CHK = CHK_HEADS * HEAD_DIM
LEFT_CHUNKS = 8
BAND = (LEFT_CHUNKS + 1) * CHUNK
MAX_REL = 128
N_REL = 2 * MAX_REL + 1
Q_BLOCK = 128
MEM_HEADS = 4
MEM_HEAD_DIM = D_MODEL // MEM_HEADS
D_FF = 4 * D_MODEL
EPS = 1e-6
D_IN = 3 * D_FOX + FOX_HEADS + 3 * D_CHK

kernel_name = 'hybrid_fox_chunkrel_memxattn_block'


def rmsnorm(x, g):
    xf = x.astype(jnp.float32)
    y = xf * lax.rsqrt(jnp.mean(xf * xf, axis=-1, keepdims=True) + EPS) * g.astype(jnp.float32)
    return y.astype(x.dtype)


def forgetting_attention(q, k, v, f_logit):
    S = q.shape[1]
    Dh = q.shape[-1]
    scale = Dh ** -0.5
    logf = jax.nn.log_sigmoid(f_logit.astype(jnp.float32))
    c = jnp.cumsum(logf, axis=1).transpose(0, 2, 1)
    pos = jnp.arange(S)
    outs = []
    for i in range(S // Q_BLOCK):
        q0, q1 = i * Q_BLOCK, (i + 1) * Q_BLOCK
        qb = q[:, q0:q1]
        kb = k[:, :q1]
        vb = v[:, :q1]
        logits = jnp.einsum('bqhd,bkhd->bhqk', qb, kb).astype(jnp.float32) * scale
        logits = logits + c[:, :, q0:q1, None] - c[:, :, None, :q1]
        causal = pos[q0:q1, None] >= pos[None, :q1]
        logits = jnp.where(causal[None, None], logits, -jnp.inf)
        p = jax.nn.softmax(logits, axis=-1).astype(v.dtype)
        outs.append(jnp.einsum('bhqk,bkhd->bqhd', p, vb))
    return jnp.concatenate(outs, axis=1)


def _rel_index():
    i = np.arange(CHUNK)[:, None]
    m = np.arange(BAND)[None, :]
    rel = i + LEFT_CHUNKS * CHUNK - m
    return np.clip(rel, -MAX_REL, MAX_REL) + MAX_REL


def chunked_relpos_attention(q, k, v, rel_table):
    B, S, H, Dh = q.shape
    NC = S // CHUNK
    scale = Dh ** -0.5
    qc = q.reshape(B, NC, CHUNK, H, Dh)
    pad = ((0, 0), (LEFT_CHUNKS * CHUNK, 0), (0, 0), (0, 0))
    kp = jnp.pad(k, pad).reshape(B, NC + LEFT_CHUNKS, CHUNK, H, Dh)
    vp = jnp.pad(v, pad).reshape(B, NC + LEFT_CHUNKS, CHUNK, H, Dh)
    kband = jnp.concatenate([kp[:, j:j + NC] for j in range(LEFT_CHUNKS + 1)], axis=2)
    vband = jnp.concatenate([vp[:, j:j + NC] for j in range(LEFT_CHUNKS + 1)], axis=2)
    bias = rel_table.astype(jnp.float32)[:, _rel_index()]
    key_pos = jnp.arange(NC)[:, None] * CHUNK + jnp.arange(BAND)[None, :] - LEFT_CHUNKS * CHUNK
    valid = key_pos >= 0
    logits = jnp.einsum('bcqhd,bckhd->bhcqk', qc, kband).astype(jnp.float32) * scale
    logits = logits + bias[None, :, None]
    logits = jnp.where(valid[None, None, :, None, :], logits, -jnp.inf)
    p = jax.nn.softmax(logits, axis=-1).astype(v.dtype)
    out = jnp.einsum('bhcqk,bckhd->bcqhd', p, vband)
    return out.reshape(B, S, H, Dh)


def memory_cross_attention(h, mem_n, w_mq, w_mk, w_mv, w_mo):
    B, S, _ = h.shape
    M = mem_n.shape[1]
    q = (h @ w_mq).reshape(B, S, MEM_HEADS, MEM_HEAD_DIM)
    k = (mem_n @ w_mk).reshape(B, M, MEM_HEADS, MEM_HEAD_DIM)
    v = (mem_n @ w_mv).reshape(B, M, MEM_HEADS, MEM_HEAD_DIM)
    logits = jnp.einsum('bshd,bmhd->bhsm', q, k).astype(jnp.float32) * (MEM_HEAD_DIM ** -0.5)
    p = jax.nn.softmax(logits, axis=-1).astype(v.dtype)
    o = jnp.einsum('bhsm,bmhd->bshd', p, v).reshape(B, S, D_MODEL)
    return o @ w_mo


def setup_inputs(seed: int = 0) -> dict:
    key = jax.random.key(seed)
    ks = jax.random.split(key, 24)
    f32 = jnp.float32

    def w(k, shape, fan_in):
        return jax.random.normal(k, shape, f32) * fan_in ** -0.5

    def gain(k, n):
        return 1.0 + 0.05 * jax.random.normal(k, (DEPTH, n), f32)

    return {
        'x': jax.random.normal(ks[0], (BATCH, SEQ, D_MODEL), f32),
        'mem': jax.random.normal(ks[1], (BATCH, N_MEM, D_MODEL), f32),
        'w_in': w(ks[2], (DEPTH, D_MODEL, D_IN), D_MODEL),
        'b_fgt': 3.0 + 0.1 * jax.random.normal(ks[3], (DEPTH, FOX_HEADS), f32),
        'rel_bias': 0.2 * jax.random.normal(ks[4], (DEPTH, CHK_HEADS, N_REL), f32),
        'g_fox_out': gain(ks[5], D_FOX),
        'g_chk_out': gain(ks[6], D_CHK),
        'w_out': w(ks[7], (DEPTH, D_MIX, D_MODEL), D_MIX),
        'g_mix_pre': gain(ks[8], D_MODEL),
        'g_mix_post': gain(ks[9], D_MODEL),
        'g_mem_kv': gain(ks[10], D_MODEL),
        'w_mq': w(ks[11], (DEPTH, D_MODEL, D_MODEL), D_MODEL),
        'w_mk': w(ks[12], (DEPTH, D_MODEL, D_MODEL), D_MODEL),
        'w_mv': w(ks[13], (DEPTH, D_MODEL, D_MODEL), D_MODEL),
        'w_mo': w(ks[14], (DEPTH, D_MODEL, D_MODEL), D_MODEL),
        'g_mem_pre': gain(ks[15], D_MODEL),
        'g_mem_post': gain(ks[16], D_MODEL),
        'w_ff1': w(ks[17], (DEPTH, D_MODEL, D_FF), D_MODEL),
        'w_ff2': w(ks[18], (DEPTH, D_FF, D_MODEL), D_FF),
        'g_ff_pre': gain(ks[19], D_MODEL),
        'g_ff_post': gain(ks[20], D_MODEL),
    }


def reference(x, mem, w_in, b_fgt, rel_bias, g_fox_out, g_chk_out, w_out, g_mix_pre, g_mix_post,
              g_mem_kv, w_mq, w_mk, w_mv, w_mo, g_mem_pre, g_mem_post,
              w_ff1, w_ff2, g_ff_pre, g_ff_post):
    B, S, _ = x.shape
    for l in range(DEPTH):
        h = rmsnorm(x, g_mix_pre[l])
        proj = h @ w_in[l]
        o0 = 0
        fq = proj[..., o0:o0 + D_FOX]; o0 += D_FOX
        fk = proj[..., o0:o0 + D_FOX]; o0 += D_FOX
        fv = proj[..., o0:o0 + D_FOX]; o0 += D_FOX
        f_logit = proj[..., o0:o0 + FOX_HEADS] + b_fgt[l]; o0 += FOX_HEADS
        cq = proj[..., o0:o0 + D_CHK]; o0 += D_CHK
        ck = proj[..., o0:o0 + D_CHK]; o0 += D_CHK
        cv = proj[..., o0:o0 + D_CHK]
        shp_f = (B, S, FOX_HEADS, HEAD_DIM)
        shp_c = (B, S, CHK_HEADS, HEAD_DIM)
        y_fox = forgetting_attention(fq.reshape(shp_f), fk.reshape(shp_f), fv.reshape(shp_f), f_logit)
        y_chk = chunked_relpos_attention(cq.reshape(shp_c), ck.reshape(shp_c), cv.reshape(shp_c), rel_bias[l])
        y = jnp.concatenate([rmsnorm(y_fox.reshape(B, S, D_FOX), g_fox_out[l]),
                             rmsnorm(y_chk.reshape(B, S, D_CHK), g_chk_out[l])], axis=-1)
        x = x + rmsnorm(y @ w_out[l], g_mix_post[l])
        h = rmsnorm(x, g_mem_pre[l])
        mem_n = rmsnorm(mem, g_mem_kv[l])
        y = memory_cross_attention(h, mem_n, w_mq[l], w_mk[l], w_mv[l], w_mo[l])
        x = x + rmsnorm(y, g_mem_post[l])
        h = rmsnorm(x, g_ff_pre[l])
        y = jnp.square(jax.nn.relu(h @ w_ff1[l])) @ w_ff2[l]
        x = x + rmsnorm(y, g_ff_post[l])
    return x
```

```python
import functools

import numpy as np
import jax
import jax.numpy as jnp
from jax import lax
from jax.experimental import pallas as pl
from jax.experimental.pallas import tpu as pltpu

F32 = jnp.float32
BF16 = jnp.bfloat16

EPS = 1e-6
HEAD_DIM = 64
PAIR = 2 * HEAD_DIM
CHUNK = 64
LEFT_CHUNKS = 8
MAX_REL = 128
MEM_HEADS = 4
NEG = -1e30
LANES = 128
VMEM_LIMIT = 56 * 1024 * 1024


def _rms(x, g):
    return x * lax.rsqrt(jnp.mean(x * x, axis=-1, keepdims=True) + EPS) * g


def _dot(a, b):
    return jnp.dot(a, b, preferred_element_type=F32)


def _dot_nt(a, b):
    return lax.dot_general(a, b, (((1,), (1,)), ((), ())), preferred_element_type=F32)


def _inproj_kernel(x_ref, g_ref, w_ref, wf_ref, bf_ref, tri_ref, proj_ref, c_ref, ct_ref, carry_ref):
    @pl.when(pl.program_id(1) == 0)
    def _():
        carry_ref[...] = jnp.zeros_like(carry_ref)

    hb = _rms(x_ref[0], g_ref[...]).astype(BF16)
    proj_ref[0] = _dot(hb, w_ref[...]).astype(BF16)

    f = _dot(hb, wf_ref[...]) + bf_ref[...]
    logf = jnp.minimum(f, 0.0) - jnp.log1p(jnp.exp(-jnp.abs(f)))
    hi = logf.astype(BF16)
    r1 = logf - hi.astype(F32)
    mid = r1.astype(BF16)
    lo = (r1 - mid.astype(F32)).astype(BF16)
    tri = tri_ref[...]
    c = _dot(tri, hi) + _dot(tri, mid) + _dot(tri, lo) + carry_ref[...]
    tm = c.shape[0]
    carry_ref[...] = c[tm - 1:tm, :]
    c_ref[0] = c
    ct = c.T
    ct_ref[0] = ct[0:ct_ref.shape[1], :]


def _inproj(x, g, w_main, w_f, b_f, tm):
    B, S, D = x.shape
    n_main = w_main.shape[1]
    heads = 8
    tri = jnp.asarray(np.tril(np.ones((tm, tm), np.float32)), BF16)
    return pl.pallas_call(
        _inproj_kernel,
        grid=(B, S // tm),
        in_specs=[
            pl.BlockSpec((1, tm, D), lambda b, i: (b, i, 0)),
            pl.BlockSpec((1, D), lambda b, i: (0, 0)),
            pl.BlockSpec((D, n_main), lambda b, i: (0, 0)),
            pl.BlockSpec((D, LANES), lambda b, i: (0, 0)),
            pl.BlockSpec((1, LANES), lambda b, i: (0, 0)),
            pl.BlockSpec((tm, tm), lambda b, i: (0, 0)),
        ],
        out_specs=[
            pl.BlockSpec((1, tm, n_main), lambda b, i: (b, i, 0)),
            pl.BlockSpec((1, tm, LANES), lambda b, i: (b, i, 0)),
            pl.BlockSpec((1, heads, tm), lambda b, i: (b, 0, i)),
        ],
        out_shape=[
            jax.ShapeDtypeStruct((B, S, n_main), BF16),
            jax.ShapeDtypeStruct((B, S, LANES), F32),
            jax.ShapeDtypeStruct((B, heads, S), F32),
        ],
        scratch_shapes=[pltpu.VMEM((1, LANES), F32)],
        compiler_params=pltpu.CompilerParams(
            dimension_semantics=("arbitrary", "arbitrary"), vmem_limit_bytes=VMEM_LIMIT),
        name="inproj",
    )(x, g, w_main, w_f, b_f, tri)


def _pair_update(s0, s1, vv, lo_mask, state):
    m0, l0, m1, l1, acc = state
    mn0 = jnp.maximum(m0, jnp.max(s0, axis=-1, keepdims=True))
    mn1 = jnp.maximum(m1, jnp.max(s1, axis=-1, keepdims=True))
    a0 = jnp.exp(m0 - mn0)
    a1 = jnp.exp(m1 - mn1)
    p0 = jnp.exp(s0 - mn0)
    p1 = jnp.exp(s1 - mn1)
    l0 = a0 * l0 + jnp.sum(p0, axis=-1, keepdims=True)
    l1 = a1 * l1 + jnp.sum(p1, axis=-1, keepdims=True)
    v0 = jnp.where(lo_mask, vv, jnp.zeros_like(vv))
    v1 = jnp.where(lo_mask, jnp.zeros_like(vv), vv)
    pv = _dot(p0.astype(BF16), v0) + _dot(p1.astype(BF16), v1)
    acc = jnp.where(lo_mask, a0, a1) * acc + pv
    return mn0, l0, mn1, l1, acc


def _pair_init(tq):
    m = jnp.full((tq, 1), NEG, F32)
    z = jnp.zeros((tq, 1), F32)
    return m, z, m, z, jnp.zeros((tq, PAIR), F32)


def _pair_finish(state, lo_mask):
    _, l0, _, l1, acc = state
    return acc * jnp.where(lo_mask, 1.0 / l0, 1.0 / l1)


def _fox_kernel(q_ref, k_ref, v_ref, c_ref, ct_ref, o_ref, *, tq):
    p = pl.program_id(1)
    i = pl.program_id(2)
    lane = lax.broadcasted_iota(jnp.int32, (1, PAIR), 1)
    lo_mask = lane < HEAD_DIM
    q = q_ref[0]
    q0 = jnp.where(lo_mask, q, jnp.zeros_like(q))
    q1 = jnp.where(lo_mask, jnp.zeros_like(q), q)
    c = c_ref[0]
    cq0 = jnp.sum(jnp.where(lane == 2 * p, c, 0.0), axis=-1, keepdims=True)
    cq1 = jnp.sum(jnp.where(lane == 2 * p + 1, c, 0.0), axis=-1, keepdims=True)

    def logits(j):
        ks = pl.multiple_of(j * tq, tq)
        kk = k_ref[0, pl.ds(ks, tq), :]
        vv = v_ref[0, pl.ds(ks, tq), :]
        ck = ct_ref[0, 0, :, pl.ds(ks, tq)]
        s0 = _dot_nt(q0, kk) + cq0 - ck[0:1, :]
        s1 = _dot_nt(q1, kk) + cq1 - ck[1:2, :]
        return s0, s1, vv

    s0, s1, vv = logits(i)
    row = lax.broadcasted_iota(jnp.int32, (tq, tq), 0)
    col = lax.broadcasted_iota(jnp.int32, (tq, tq), 1)
    causal = row >= col
    state = _pair_update(jnp.where(causal, s0, NEG), jnp.where(causal, s1, NEG), vv, lo_mask,
                         _pair_init(tq))

    def body(j, st):
        s0, s1, vv = logits(j)
        return _pair_update(s0, s1, vv, lo_mask, st)

    state = lax.fori_loop(0, i, body, state)
    o_ref[0] = _pair_finish(state, lo_mask).astype(o_ref.dtype)


def _fox(proj, c, ct, tq):
    B, S, _ = proj.shape
    pairs = ct.shape[1]
    return pl.pallas_call(
        functools.partial(_fox_kernel, tq=tq),
        grid=(B, pairs, S // tq),
        in_specs=[
            pl.BlockSpec((1, tq, PAIR), lambda b, p, i: (b, i, p)),
            pl.BlockSpec((1, S, PAIR), lambda b, p, i: (b, 0, pairs + p)),
            pl.BlockSpec((1, S, PAIR), lambda b, p, i: (b, 0, 2 * pairs + p)),
            pl.BlockSpec((1, tq, LANES), lambda b, p, i: (b, i, 0)),
            pl.BlockSpec((1, 1, 2, S), lambda b, p, i: (b, p, 0, 0)),
        ],
        out_specs=pl.BlockSpec((1, tq, PAIR), lambda b, p, i: (b, i, p)),
        out_shape=jax.ShapeDtypeStruct((B, S, pairs * PAIR), BF16),
        compiler_params=pltpu.CompilerParams(
            dimension_semantics=("arbitrary", "arbitrary", "arbitrary"), vmem_limit_bytes=VMEM_LIMIT),
        name="fox_attn",
    )(proj, proj, proj, c, ct)


def _chunk_kernel(q_ref, k_ref, v_ref, bias_ref, o_ref, m0_ref, l0_ref, m1_ref, l1_ref, acc_ref, *, tq):
    i = pl.program_id(2)
    lane = lax.broadcasted_iota(jnp.int32, (1, PAIR), 1)
    lo_mask = lane < HEAD_DIM
    q = q_ref[0]
    q0 = jnp.where(lo_mask, q, jnp.zeros_like(q))
    q1 = jnp.where(lo_mask, jnp.zeros_like(q), q)

    def step(j, which, state):
        ks = pl.multiple_of(j * tq, tq)
        kk = k_ref[0, pl.ds(ks, tq), :]
        vv = v_ref[0, pl.ds(ks, tq), :]
        s0 = _dot_nt(q0, kk) + bias_ref[0, which, 0]
        s1 = _dot_nt(q1, kk) + bias_ref[0, which, 1]
        return _pair_update(s0, s1, vv, lo_mask, state)

    def load():
        return m0_ref[...], l0_ref[...], m1_ref[...], l1_ref[...], acc_ref[...]

    def store(state):
        m0_ref[...], l0_ref[...], m1_ref[...], l1_ref[...], acc_ref[...] = state

    store(step(i, 0, _pair_init(tq)))

    @pl.when(i >= 1)
    def _():
        store(step(i - 1, 1, load()))

    @pl.when(i >= 2)
    def _():
        store(step(i - 2, 2, load()))

    o_ref[0] = _pair_finish(load(), lo_mask).astype(o_ref.dtype)


def _chunk_bias(rel_table, tq):
    heads = rel_table.shape[0]
    qi = np.arange(tq)[:, None]
    kj = np.arange(tq)[None, :]
    out = []
    for which in range(3):
        rel = which * tq + qi - kj
        idx = np.clip(rel, -MAX_REL, MAX_REL) + MAX_REL
        key_chunk_back = (which * tq + (qi // CHUNK) * CHUNK - (kj // CHUNK) * CHUNK) // CHUNK
        vis = (key_chunk_back >= 0) & (key_chunk_back <= LEFT_CHUNKS)
        out.append(jnp.where(vis[None], rel_table.astype(F32)[:, idx], NEG))
    bias = jnp.stack(out, axis=1)
    return bias.reshape(heads // 2, 2, 3, tq, tq).transpose(0, 2, 1, 3, 4)


def _chunk(proj, bias, tq, col0):
    B, S, _ = proj.shape
    pairs = bias.shape[0]
    assert tq % CHUNK == 0 and LEFT_CHUNKS * CHUNK <= 2 * tq, "band must fit in three key blocks"
    return pl.pallas_call(
        functools.partial(_chunk_kernel, tq=tq),
        grid=(pairs, B, S // tq),
        in_specs=[
            pl.BlockSpec((1, tq, PAIR), lambda p, b, i: (b, i, col0 + p)),
            pl.BlockSpec((1, S, PAIR), lambda p, b, i: (b, 0, col0 + pairs + p)),
            pl.BlockSpec((1, S, PAIR), lambda p, b, i: (b, 0, col0 + 2 * pairs + p)),
            pl.BlockSpec((1, 3, 2, tq, tq), lambda p, b, i: (p, 0, 0, 0, 0)),
        ],
        out_specs=pl.BlockSpec((1, tq, PAIR), lambda p, b, i: (b, i, p)),
        out_shape=jax.ShapeDtypeStruct((B, S, pairs * PAIR), BF16),
        scratch_shapes=[pltpu.VMEM((tq, 1), F32)] * 4 + [pltpu.VMEM((tq, PAIR), F32)],
        compiler_params=pltpu.CompilerParams(
            dimension_semantics=("arbitrary", "arbitrary", "arbitrary"), vmem_limit_bytes=VMEM_LIMIT),
        name="chunk_attn",
    )(proj, proj, proj, bias)


def _memkv_kernel(mem_ref, g_ref, wk_ref, wv_ref, k_ref, v_ref):
    mb = _rms(mem_ref[0], g_ref[...]).astype(BF16)
    k_ref[0] = _dot(mb, wk_ref[...]).astype(BF16)
    v_ref[0] = _dot(mb, wv_ref[...]).astype(BF16)


def _memkv(mem, g, wk, wv):
    B, M, D = mem.shape
    full = lambda b: (0, 0)
    return pl.pallas_call(
        _memkv_kernel,
        grid=(B,),
        in_specs=[
            pl.BlockSpec((1, M, D), lambda b: (b, 0, 0)),
            pl.BlockSpec((1, D), full),
            pl.BlockSpec((D, D), full),
            pl.BlockSpec((D, D), full),
        ],
        out_specs=[pl.BlockSpec((1, M, D), lambda b: (b, 0, 0))] * 2,
        out_shape=[jax.ShapeDtypeStruct((B, M, D), BF16)] * 2,
        compiler_params=pltpu.CompilerParams(
            dimension_semantics=("arbitrary",), vmem_limit_bytes=VMEM_LIMIT),
        name="mem_kv",
    )(mem, g, wk, wv)


def _mix_mem_kernel(yf_ref, yc_ref, x_ref, gf_ref, gc_ref, wo_ref, gpost_ref,
                    gpre_ref, wq_ref, mk_ref, mv_ref, wmo_ref, gmpost_ref, o_ref):
    d_fox = yf_ref.shape[2]
    yf = _rms(yf_ref[0].astype(F32), gf_ref[...]).astype(BF16)
    yc = _rms(yc_ref[0].astype(F32), gc_ref[...]).astype(BF16)
    y = _dot(yf, wo_ref[0:d_fox, :]) + _dot(yc, wo_ref[d_fox:, :])
    x1 = x_ref[0] + _rms(y, gpost_ref[...])

    h = _rms(x1, gpre_ref[...]).astype(BF16)
    q = _dot(h, wq_ref[...]).astype(BF16)
    dh = q.shape[1] // MEM_HEADS
    outs = []
    for hh in range(MEM_HEADS):
        sl = slice(hh * dh, (hh + 1) * dh)
        s = _dot_nt(q[:, sl], mk_ref[0, :, sl])
        pr = jnp.exp(s - jnp.max(s, axis=-1, keepdims=True))
        inv = 1.0 / jnp.sum(pr, axis=-1, keepdims=True)
        outs.append((_dot(pr.astype(BF16), mv_ref[0, :, sl]) * inv).astype(BF16))
    o = jnp.concatenate(outs, axis=-1)
    y2 = _dot(o, wmo_ref[...])
    o_ref[0] = x1 + _rms(y2, gmpost_ref[...])


def _mix_mem(yf, yc, x, gf, gc, wo, gpost, gpre, wq, mk, mv, wmo, gmpost, tm):
    B, S, D = x.shape
    M = mk.shape[1]
    dg = yf.shape[2]
    vec = lambda n: pl.BlockSpec((1, n), lambda b, i: (0, 0))
    mat = lambda r, c: pl.BlockSpec((r, c), lambda b, i: (0, 0))
    return pl.pallas_call(
        _mix_mem_kernel,
        grid=(B, S // tm),
        in_specs=[
            pl.BlockSpec((1, tm, dg), lambda b, i: (b, i, 0)),
            pl.BlockSpec((1, tm, dg), lambda b, i: (b, i, 0)),
            pl.BlockSpec((1, tm, D), lambda b, i: (b, i, 0)),
            vec(dg), vec(dg), mat(2 * dg, D), vec(D),
            vec(D), mat(D, D),
            pl.BlockSpec((1, M, D), lambda b, i: (b, 0, 0)),
            pl.BlockSpec((1, M, D), lambda b, i: (b, 0, 0)),
            mat(D, D), vec(D),
        ],
        out_specs=pl.BlockSpec((1, tm, D), lambda b, i: (b, i, 0)),
        out_shape=jax.ShapeDtypeStruct((B, S, D), F32),
        compiler_params=pltpu.CompilerParams(
            dimension_semantics=("arbitrary", "arbitrary"), vmem_limit_bytes=VMEM_LIMIT),
        name="mix_mem",
    )(yf, yc, x, gf, gc, wo, gpost, gpre, wq, mk, mv, wmo, gmpost)


def _mlp_kernel(x_ref, gpre_ref, w1_ref, w2_ref, gpost_ref, o_ref, *, ff_block):
    x = x_ref[...]
    h = _rms(x, gpre_ref[...]).astype(BF16)
    d_ff = w1_ref.shape[1]
    y = None
    for n in range(0, d_ff, ff_block):
        a = jnp.maximum(_dot(h, w1_ref[:, n:n + ff_block]), 0.0)
        part = _dot((a * a).astype(BF16), w2_ref[n:n + ff_block, :])
        y = part if y is None else y + part
    o_ref[...] = x + _rms(y, gpost_ref[...])


def _mlp(x, gpre, w1, w2, gpost, tm, ff_block):
    R, D = x.shape
    d_ff = w1.shape[1]
    return pl.pallas_call(
        functools.partial(_mlp_kernel, ff_block=ff_block),
        grid=(R // tm,),
        in_specs=[
            pl.BlockSpec((tm, D), lambda i: (i, 0)),
            pl.BlockSpec((1, D), lambda i: (0, 0)),
            pl.BlockSpec((D, d_ff), lambda i: (0, 0)),
            pl.BlockSpec((d_ff, D), lambda i: (0, 0)),
            pl.BlockSpec((1, D), lambda i: (0, 0)),
        ],
        out_specs=pl.BlockSpec((tm, D), lambda i: (i, 0)),
        out_shape=jax.ShapeDtypeStruct((R, D), F32),
        compiler_params=pltpu.CompilerParams(
            dimension_semantics=("arbitrary",), vmem_limit_bytes=VMEM_LIMIT),
        name="mlp",
    )(x, gpre, w1, w2, gpost)


def kernel(x, mem, w_in, b_fgt, rel_bias, g_fox_out, g_chk_out, w_out, g_mix_pre, g_mix_post,
           g_mem_kv, w_mq, w_mk, w_mv, w_mo, g_mem_pre, g_mem_post,
           w_ff1, w_ff2, g_ff_pre, g_ff_post):
    B, S, D = x.shape
    depth = w_in.shape[0]
    heads = b_fgt.shape[1]
    d_fox = heads * HEAD_DIM
    d_chk = rel_bias.shape[1] * HEAD_DIM
    assert d_fox == d_chk and heads % 2 == 0 and heads <= LANES
    pairs = heads // 2
    tq = 256
    row = lambda v: v.reshape(1, -1)

    for l in range(depth):
        scale = HEAD_DIM ** -0.5
        wl = w_in[l]
        o_f = 3 * d_fox
        w_main = jnp.concatenate([
            wl[:, :d_fox] * scale, wl[:, d_fox:o_f],
            wl[:, o_f + heads:o_f + heads + d_chk] * scale, wl[:, o_f + heads + d_chk:],
        ], axis=1).astype(BF16)
        w_f = jnp.pad(wl[:, o_f:o_f + heads], ((0, 0), (0, LANES - heads))).astype(BF16)
        b_f = jnp.pad(b_fgt[l], (0, LANES - heads)).reshape(1, LANES)

        proj, c, ct = _inproj(x, row(g_mix_pre[l]), w_main, w_f, b_f, tm=512)
        ct = ct.reshape(B, pairs, 2, S)
        y_fox = _fox(proj, c, ct, tq)
        y_chk = _chunk(proj, _chunk_bias(rel_bias[l], tq), tq, col0=3 * pairs)

        mk, mv = _memkv(mem, row(g_mem_kv[l]), w_mk[l].astype(BF16), w_mv[l].astype(BF16))
        mem_scale = (D // MEM_HEADS) ** -0.5
        x = _mix_mem(y_fox, y_chk, x, row(g_fox_out[l]), row(g_chk_out[l]), w_out[l].astype(BF16),
                     row(g_mix_post[l]), row(g_mem_pre[l]), (w_mq[l] * mem_scale).astype(BF16),
                     mk, mv, w_mo[l].astype(BF16), row(g_mem_post[l]), tm=256)
        x = _mlp(x.reshape(B * S, D), row(g_ff_pre[l]), w_ff1[l].astype(BF16), w_ff2[l].astype(BF16),
                 row(g_ff_post[l]), tm=256, ff_block=1024).reshape(B, S, D)
    return x
```

```python
import functools

import numpy as np
import jax
import jax.numpy as jnp
from jax import lax
from jax.experimental import pallas as pl
from jax.experimental.pallas import tpu as pltpu

F32 = jnp.float32
BF16 = jnp.bfloat16

EPS = 1e-6
HEAD_DIM = 64
PAIR = 2 * HEAD_DIM
CHUNK = 64
LEFT_CHUNKS = 8
MAX_REL = 128
MEM_HEADS = 4
NEG = -1e30
LANES = 128
VMEM_LIMIT = 56 * 1024 * 1024


def _rms(x, g):
    return x * lax.rsqrt(jnp.mean(x * x, axis=-1, keepdims=True) + EPS) * g


def _dot(a, b):
    return jnp.dot(a, b, preferred_element_type=F32)


def _dot_nt(a, b):
    return lax.dot_general(a, b, (((1,), (1,)), ((), ())), preferred_element_type=F32)


def _inproj_kernel(x_ref, g_ref, w_ref, wf_ref, bf_ref, tri_ref, proj_ref, c_ref, ct_ref, carry_ref):
    @pl.when(pl.program_id(1) == 0)
    def _():
        carry_ref[...] = jnp.zeros_like(carry_ref)

    hb = _rms(x_ref[0], g_ref[...]).astype(BF16)
    proj_ref[0] = _dot(hb, w_ref[...]).astype(BF16)

    f = _dot(hb, wf_ref[...]) + bf_ref[...]
    logf = jnp.minimum(f, 0.0) - jnp.log1p(jnp.exp(-jnp.abs(f)))
    hi = logf.astype(BF16)
    r1 = logf - hi.astype(F32)
    mid = r1.astype(BF16)
    lo = (r1 - mid.astype(F32)).astype(BF16)
    tri = tri_ref[...]
    c = _dot(tri, hi) + _dot(tri, mid) + _dot(tri, lo) + carry_ref[...]
    tm = c.shape[0]
    carry_ref[...] = c[tm - 1:tm, :]
    c_ref[0] = c
    ct = c.T
    ct_ref[0] = ct[0:ct_ref.shape[1], :]


def _inproj(x, g, w_main, w_f, b_f, tm):
    B, S, D = x.shape
    n_main = w_main.shape[1]
    heads = 8
    tri = jnp.asarray(np.tril(np.ones((tm, tm), np.float32)), BF16)
    return pl.pallas_call(
        _inproj_kernel,
        grid=(B, S // tm),
        in_specs=[
            pl.BlockSpec((1, tm, D), lambda b, i: (b, i, 0)),
            pl.BlockSpec((1, D), lambda b, i: (0, 0)),
            pl.BlockSpec((D, n_main), lambda b, i: (0, 0)),
            pl.BlockSpec((D, LANES), lambda b, i: (0, 0)),
            pl.BlockSpec((1, LANES), lambda b, i: (0, 0)),
            pl.BlockSpec((tm, tm), lambda b, i: (0, 0)),
        ],
        out_specs=[
            pl.BlockSpec((1, tm, n_main), lambda b, i: (b, i, 0)),
            pl.BlockSpec((1, tm, LANES), lambda b, i: (b, i, 0)),
            pl.BlockSpec((1, heads, tm), lambda b, i: (b, 0, i)),
        ],
        out_shape=[
            jax.ShapeDtypeStruct((B, S, n_main), BF16),
            jax.ShapeDtypeStruct((B, S, LANES), F32),
            jax.ShapeDtypeStruct((B, heads, S), F32),
        ],
        scratch_shapes=[pltpu.VMEM((1, LANES), F32)],
        compiler_params=pltpu.CompilerParams(
            dimension_semantics=("arbitrary", "arbitrary"), vmem_limit_bytes=VMEM_LIMIT),
        name="inproj",
    )(x, g, w_main, w_f, b_f, tri)


def _pair_update(s0, s1, vv, lo_mask, state):
    m0, l0, m1, l1, acc = state
    mn0 = jnp.maximum(m0, jnp.max(s0, axis=-1, keepdims=True))
    mn1 = jnp.maximum(m1, jnp.max(s1, axis=-1, keepdims=True))
    a0 = jnp.exp(m0 - mn0)
    a1 = jnp.exp(m1 - mn1)
    p0 = jnp.exp(s0 - mn0)
    p1 = jnp.exp(s1 - mn1)
    l0 = a0 * l0 + jnp.sum(p0, axis=-1, keepdims=True)
    l1 = a1 * l1 + jnp.sum(p1, axis=-1, keepdims=True)
    v0 = jnp.where(lo_mask, vv, jnp.zeros_like(vv))
    v1 = jnp.where(lo_mask, jnp.zeros_like(vv), vv)
    pv = _dot(p0.astype(BF16), v0) + _dot(p1.astype(BF16), v1)
    acc = jnp.where(lo_mask, a0, a1) * acc + pv
    return mn0, l0, mn1, l1, acc


def _pair_init(tq):
    m = jnp.full((tq, 1), NEG, F32)
    z = jnp.zeros((tq, 1), F32)
    return m, z, m, z, jnp.zeros((tq, PAIR), F32)


def _pair_finish(state, lo_mask):
    _, l0, _, l1, acc = state
    return acc * jnp.where(lo_mask, 1.0 / l0, 1.0 / l1)


def _fox_kernel(q_ref, k_ref, v_ref, c_ref, ct_ref, o_ref, *, tq):
    p = pl.program_id(1)
    i = pl.program_id(2)
    lane = lax.broadcasted_iota(jnp.int32, (1, PAIR), 1)
    lo_mask = lane < HEAD_DIM
    q = q_ref[0]
    q0 = jnp.where(lo_mask, q, jnp.zeros_like(q))
    q1 = jnp.where(lo_mask, jnp.zeros_like(q), q)
    c = c_ref[0]
    cq0 = jnp.sum(jnp.where(lane == 2 * p, c, 0.0), axis=-1, keepdims=True)
    cq1 = jnp.sum(jnp.where(lane == 2 * p + 1, c, 0.0), axis=-1, keepdims=True)

    def logits(j):
        ks = pl.multiple_of(j * tq, tq)
        kk = k_ref[0, pl.ds(ks, tq), :]
        vv = v_ref[0, pl.ds(ks, tq), :]
        ck = ct_ref[0, 0, :, pl.ds(ks, tq)]
        s0 = _dot_nt(q0, kk) + cq0 - ck[0:1, :]
        s1 = _dot_nt(q1, kk) + cq1 - ck[1:2, :]
        return s0, s1, vv

    s0, s1, vv = logits(i)
    row = lax.broadcasted_iota(jnp.int32, (tq, tq), 0)
    col = lax.broadcasted_iota(jnp.int32, (tq, tq), 1)
    causal = row >= col
    state = _pair_update(jnp.where(causal, s0, NEG), jnp.where(causal, s1, NEG), vv, lo_mask,
                         _pair_init(tq))

    def body(j, st):
        s0, s1, vv = logits(j)
        return _pair_update(s0, s1, vv, lo_mask, st)

    state = lax.fori_loop(0, i, body, state)
    o_ref[0] = _pair_finish(state, lo_mask).astype(o_ref.dtype)


def _fox(proj, c, ct, tq):
    B, S, _ = proj.shape
    pairs = ct.shape[1]
    return pl.pallas_call(
        functools.partial(_fox_kernel, tq=tq),
        grid=(B, pairs, S // tq),
        in_specs=[
            pl.BlockSpec((1, tq, PAIR), lambda b, p, i: (b, i, p)),
            pl.BlockSpec((1, S, PAIR), lambda b, p, i: (b, 0, pairs + p)),
            pl.BlockSpec((1, S, PAIR), lambda b, p, i: (b, 0, 2 * pairs + p)),
            pl.BlockSpec((1, tq, LANES), lambda b, p, i: (b, i, 0)),
            pl.BlockSpec((1, 1, 2, S), lambda b, p, i: (b, p, 0, 0)),
        ],
        out_specs=pl.BlockSpec((1, tq, PAIR), lambda b, p, i: (b, i, p)),
        out_shape=jax.ShapeDtypeStruct((B, S, pairs * PAIR), BF16),
        compiler_params=pltpu.CompilerParams(
            dimension_semantics=("arbitrary", "arbitrary", "arbitrary"), vmem_limit_bytes=VMEM_LIMIT),
        name="fox_attn",
    )(proj, proj, proj, c, ct)


def _chunk_kernel(q_ref, k_ref, v_ref, g_ref, o_ref, bias_ref, m0_ref, l0_ref, m1_ref, l1_ref, acc_ref,
                  *, tq):
    i = pl.program_id(2)
    lane = lax.broadcasted_iota(jnp.int32, (1, PAIR), 1)
    lo_mask = lane < HEAD_DIM
    q = q_ref[0]
    q0 = jnp.where(lo_mask, q, jnp.zeros_like(q))
    q1 = jnp.where(lo_mask, jnp.zeros_like(q), q)

    @pl.when((pl.program_id(1) == 0) & (i == 0))
    def _():
        q_chunk = lax.broadcasted_iota(jnp.int32, (tq, tq), 0) // CHUNK
        k_chunk = lax.broadcasted_iota(jnp.int32, (tq, tq), 1) // CHUNK
        for which in range(3):
            back = which * (tq // CHUNK) + q_chunk - k_chunk
            vis = (back >= 0) & (back <= LEFT_CHUNKS)
            for hh in range(2):
                g = jnp.broadcast_to(g_ref[0, which, hh], (tq, 2 * tq))
                toep = pltpu.roll(g, 0, 1, stride=1, stride_axis=0)[:, :tq]
                bias_ref[which, hh] = jnp.where(vis, toep, NEG)

    def step(j, which, state):
        ks = pl.multiple_of(j * tq, tq)
        kk = k_ref[0, pl.ds(ks, tq), :]
        vv = v_ref[0, pl.ds(ks, tq), :]
        s0 = _dot_nt(q0, kk) + bias_ref[which, 0]
        s1 = _dot_nt(q1, kk) + bias_ref[which, 1]
        return _pair_update(s0, s1, vv, lo_mask, state)

    def load():
        return m0_ref[...], l0_ref[...], m1_ref[...], l1_ref[...], acc_ref[...]

    def store(state):
        m0_ref[...], l0_ref[...], m1_ref[...], l1_ref[...], acc_ref[...] = state

    store(step(i, 0, _pair_init(tq)))

    @pl.when(i >= 1)
    def _():
        store(step(i - 1, 1, load()))

    @pl.when(i >= 2)
    def _():
        store(step(i - 2, 2, load()))

    o_ref[0] = _pair_finish(load(), lo_mask).astype(o_ref.dtype)


def _chunk_bias_rows(rel_table, tq):
    heads = rel_table.shape[0]
    m = np.arange(2 * tq)
    offset = np.where(m < tq, -m, 2 * tq - m)
    idx = np.stack([np.clip(which * tq + offset, -MAX_REL, MAX_REL) + MAX_REL for which in range(3)])
    rows = rel_table.astype(F32)[:, idx]
    return rows.reshape(heads // 2, 2, 3, 1, 2 * tq).transpose(0, 2, 1, 3, 4)


def _chunk(proj, bias_rows, tq, col0):
    B, S, _ = proj.shape
    pairs = bias_rows.shape[0]
    assert tq % CHUNK == 0 and LEFT_CHUNKS * CHUNK <= 2 * tq, "band must fit in three key blocks"
    return pl.pallas_call(
        functools.partial(_chunk_kernel, tq=tq),
        grid=(pairs, B, S // tq),
        in_specs=[
            pl.BlockSpec((1, tq, PAIR), lambda p, b, i: (b, i, col0 + p)),
            pl.BlockSpec((1, S, PAIR), lambda p, b, i: (b, 0, col0 + pairs + p)),
            pl.BlockSpec((1, S, PAIR), lambda p, b, i: (b, 0, col0 + 2 * pairs + p)),
            pl.BlockSpec((1, 3, 2, 1, 2 * tq), lambda p, b, i: (p, 0, 0, 0, 0)),
        ],
        out_specs=pl.BlockSpec((1, tq, PAIR), lambda p, b, i: (b, i, p)),
        out_shape=jax.ShapeDtypeStruct((B, S, pairs * PAIR), BF16),
        scratch_shapes=[pltpu.VMEM((3, 2, tq, tq), F32)] + [pltpu.VMEM((tq, 1), F32)] * 4
        + [pltpu.VMEM((tq, PAIR), F32)],
        compiler_params=pltpu.CompilerParams(
            dimension_semantics=("arbitrary", "arbitrary", "arbitrary"), vmem_limit_bytes=VMEM_LIMIT),
        name="chunk_attn",
    )(proj, proj, proj, bias_rows)


def _memkv_kernel(mem_ref, g_ref, wk_ref, wv_ref, k_ref, v_ref):
    mb = _rms(mem_ref[0], g_ref[...]).astype(BF16)
    k_ref[0] = _dot(mb, wk_ref[...]).astype(BF16)
    v_ref[0] = _dot(mb, wv_ref[...]).astype(BF16)


def _memkv(mem, g, wk, wv):
    B, M, D = mem.shape
    full = lambda b: (0, 0)
    return pl.pallas_call(
        _memkv_kernel,
        grid=(B,),
        in_specs=[
            pl.BlockSpec((1, M, D), lambda b: (b, 0, 0)),
            pl.BlockSpec((1, D), full),
            pl.BlockSpec((D, D), full),
            pl.BlockSpec((D, D), full),
        ],
        out_specs=[pl.BlockSpec((1, M, D), lambda b: (b, 0, 0))] * 2,
        out_shape=[jax.ShapeDtypeStruct((B, M, D), BF16)] * 2,
        compiler_params=pltpu.CompilerParams(
            dimension_semantics=("arbitrary",), vmem_limit_bytes=VMEM_LIMIT),
        name="mem_kv",
    )(mem, g, wk, wv)


def _mix_mem_kernel(yf_ref, yc_ref, x_ref, gf_ref, gc_ref, wo_ref, gpost_ref,
                    gpre_ref, wq_ref, mk_ref, mv_ref, wmo_ref, gmpost_ref, o_ref):
    d_fox = yf_ref.shape[2]
    yf = _rms(yf_ref[0].astype(F32), gf_ref[...]).astype(BF16)
    yc = _rms(yc_ref[0].astype(F32), gc_ref[...]).astype(BF16)
    y = _dot(yf, wo_ref[0:d_fox, :]) + _dot(yc, wo_ref[d_fox:, :])
    x1 = x_ref[0] + _rms(y, gpost_ref[...])

    h = _rms(x1, gpre_ref[...]).astype(BF16)
    q = _dot(h, wq_ref[...]).astype(BF16)
    dh = q.shape[1] // MEM_HEADS
    outs = []
    for hh in range(MEM_HEADS):
        sl = slice(hh * dh, (hh + 1) * dh)
        s = _dot_nt(q[:, sl], mk_ref[0, :, sl])
        pr = jnp.exp(s - jnp.max(s, axis=-1, keepdims=True))
        inv = 1.0 / jnp.sum(pr, axis=-1, keepdims=True)
        outs.append((_dot(pr.astype(BF16), mv_ref[0, :, sl]) * inv).astype(BF16))
    o = jnp.concatenate(outs, axis=-1)
    y2 = _dot(o, wmo_ref[...])
    o_ref[0] = x1 + _rms(y2, gmpost_ref[...])


def _mix_mem(yf, yc, x, gf, gc, wo, gpost, gpre, wq, mk, mv, wmo, gmpost, tm):
    B, S, D = x.shape
    M = mk.shape[1]
    dg = yf.shape[2]
    vec = lambda n: pl.BlockSpec((1, n), lambda b, i: (0, 0))
    mat = lambda r, c: pl.BlockSpec((r, c), lambda b, i: (0, 0))
    return pl.pallas_call(
        _mix_mem_kernel,
        grid=(B, S // tm),
        in_specs=[
            pl.BlockSpec((1, tm, dg), lambda b, i: (b, i, 0)),
            pl.BlockSpec((1, tm, dg), lambda b, i: (b, i, 0)),
            pl.BlockSpec((1, tm, D), lambda b, i: (b, i, 0)),
            vec(dg), vec(dg), mat(2 * dg, D), vec(D),
            vec(D), mat(D, D),
            pl.BlockSpec((1, M, D), lambda b, i: (b, 0, 0)),
            pl.BlockSpec((1, M, D), lambda b, i: (b, 0, 0)),
            mat(D, D), vec(D),
        ],
        out_specs=pl.BlockSpec((1, tm, D), lambda b, i: (b, i, 0)),
        out_shape=jax.ShapeDtypeStruct((B, S, D), F32),
        compiler_params=pltpu.CompilerParams(
            dimension_semantics=("arbitrary", "arbitrary"), vmem_limit_bytes=VMEM_LIMIT),
        name="mix_mem",
    )(yf, yc, x, gf, gc, wo, gpost, gpre, wq, mk, mv, wmo, gmpost)


def _mlp_kernel(x_ref, gpre_ref, w1_ref, w2_ref, gpost_ref, o_ref, *, ff_block):
    x = x_ref[...]
    h = _rms(x, gpre_ref[...]).astype(BF16)
    d_ff = w1_ref.shape[1]
    y = None
    for n in range(0, d_ff, ff_block):
        a = jnp.maximum(_dot(h, w1_ref[:, n:n + ff_block]), 0.0)
        part = _dot((a * a).astype(BF16), w2_ref[n:n + ff_block, :])
        y = part if y is None else y + part
    o_ref[...] = x + _rms(y, gpost_ref[...])


def _mlp(x, gpre, w1, w2, gpost, tm, ff_block):
    R, D = x.shape
    d_ff = w1.shape[1]
    return pl.pallas_call(
        functools.partial(_mlp_kernel, ff_block=ff_block),
        grid=(R // tm,),
        in_specs=[
            pl.BlockSpec((tm, D), lambda i: (i, 0)),
            pl.BlockSpec((1, D), lambda i: (0, 0)),
            pl.BlockSpec((D, d_ff), lambda i: (0, 0)),
            pl.BlockSpec((d_ff, D), lambda i: (0, 0)),
            pl.BlockSpec((1, D), lambda i: (0, 0)),
        ],
        out_specs=pl.BlockSpec((tm, D), lambda i: (i, 0)),
        out_shape=jax.ShapeDtypeStruct((R, D), F32),
        compiler_params=pltpu.CompilerParams(
            dimension_semantics=("arbitrary",), vmem_limit_bytes=VMEM_LIMIT),
        name="mlp",
    )(x, gpre, w1, w2, gpost)


def kernel(x, mem, w_in, b_fgt, rel_bias, g_fox_out, g_chk_out, w_out, g_mix_pre, g_mix_post,
           g_mem_kv, w_mq, w_mk, w_mv, w_mo, g_mem_pre, g_mem_post,
           w_ff1, w_ff2, g_ff_pre, g_ff_post):
    B, S, D = x.shape
    depth = w_in.shape[0]
    heads = b_fgt.shape[1]
    d_fox = heads * HEAD_DIM
    d_chk = rel_bias.shape[1] * HEAD_DIM
    assert d_fox == d_chk and heads % 2 == 0 and heads <= LANES
    pairs = heads // 2
    tq = 256
    row = lambda v: v.reshape(1, -1)

    for l in range(depth):
        scale = HEAD_DIM ** -0.5
        wl = w_in[l]
        o_f = 3 * d_fox
        w_main = jnp.concatenate([
            wl[:, :d_fox] * scale, wl[:, d_fox:o_f],
            wl[:, o_f + heads:o_f + heads + d_chk] * scale, wl[:, o_f + heads + d_chk:],
        ], axis=1).astype(BF16)
        w_f = jnp.pad(wl[:, o_f:o_f + heads], ((0, 0), (0, LANES - heads))).astype(BF16)
        b_f = jnp.pad(b_fgt[l], (0, LANES - heads)).reshape(1, LANES)

        proj, c, ct = _inproj(x, row(g_mix_pre[l]), w_main, w_f, b_f, tm=512)
        ct = ct.reshape(B, pairs, 2, S)
        y_fox = _fox(proj, c, ct, tq)
        y_chk = _chunk(proj, _chunk_bias_rows(rel_bias[l], tq), tq, col0=3 * pairs)

        mk, mv = _memkv(mem, row(g_mem_kv[l]), w_mk[l].astype(BF16), w_mv[l].astype(BF16))
        mem_scale = (D // MEM_HEADS) ** -0.5
        x = _mix_mem(y_fox, y_chk, x, row(g_fox_out[l]), row(g_chk_out[l]), w_out[l].astype(BF16),
                     row(g_mix_post[l]), row(g_mem_pre[l]), (w_mq[l] * mem_scale).astype(BF16),
                     mk, mv, w_mo[l].astype(BF16), row(g_mem_post[l]), tm=256)
        x = _mlp(x.reshape(B * S, D), row(g_ff_pre[l]), w_ff1[l].astype(BF16), w_ff2[l].astype(BF16),
                 row(g_ff_post[l]), tm=256, ff_block=1024).reshape(B, S, D)
    return x
```

```python
import functools
import math

import numpy as np
import jax
import jax.numpy as jnp
from jax import lax
from jax.experimental import pallas as pl
from jax.experimental.pallas import tpu as pltpu

F32 = jnp.float32
BF16 = jnp.bfloat16

EPS = 1e-6
HEAD_DIM = 64
PAIR = 2 * HEAD_DIM
CHUNK = 64
LEFT_CHUNKS = 8
MAX_REL = 128
MEM_HEADS = 4
NEG = -1e30
LOG2E = math.log2(math.e)
LANES = 128
VMEM_LIMIT = 56 * 1024 * 1024
PIECES = 3
SUM_LANE = (HEAD_DIM, 0)


def _rms(x, g):
    return x * lax.rsqrt(jnp.mean(x * x, axis=-1, keepdims=True) + EPS) * g


def _dot(a, b):
    return jnp.dot(a, b, preferred_element_type=F32)


def _dot_nt(a, b):
    return lax.dot_general(a, b, (((1,), (1,)), ((), ())), preferred_element_type=F32)


def _split3(x):
    hi = x.astype(BF16)
    r1 = x - hi.astype(F32)
    mid = r1.astype(BF16)
    lo = (r1 - mid.astype(F32)).astype(BF16)
    return hi, mid, lo


def _inproj_kernel(x_ref, g_ref, w_ref, wf_ref, bf_ref, tri_ref, eq_ref, ek_ref, oq_ref, ok_ref,
                   proj_ref, qa_ref, ka_ref, carry_ref):
    @pl.when(pl.program_id(1) == 0)
    def _():
        carry_ref[...] = jnp.zeros_like(carry_ref)

    hb = _rms(x_ref[0], g_ref[...]).astype(BF16)
    proj_ref[0] = _dot(hb, w_ref[...]).astype(BF16)

    f = _dot(hb, wf_ref[...]) + bf_ref[...]
    logf = jnp.minimum(f, 0.0) - jnp.log1p(jnp.exp(-jnp.abs(f)))
    tri = tri_ref[...]
    c = sum(_dot(tri, piece) for piece in _split3(logf)) + carry_ref[...]
    tm = c.shape[0]
    carry_ref[...] = c[tm - 1:tm, :]
    pieces = jnp.concatenate(_split3(c * LOG2E), axis=-1)
    qa_ref[0] = (_dot(pieces, eq_ref[...]) + oq_ref[...]).astype(BF16)
    ka_ref[0] = (ok_ref[...] - _dot(pieces, ek_ref[...])).astype(BF16)


def _aug_constants(heads):
    k0 = PIECES * heads
    assert 2 * k0 <= LANES
    eq = np.zeros((PIECES * LANES, LANES), np.float32)
    ek = np.zeros((PIECES * LANES, LANES), np.float32)
    for h in range(heads):
        for r in range(PIECES):
            eq[r * LANES + h, PIECES * h + r] = 1.0
            ek[r * LANES + h, k0 + PIECES * h + r] = 1.0
    oq = np.zeros((1, LANES), np.float32)
    ok = np.zeros((1, LANES), np.float32)
    oq[0, k0:2 * k0] = 1.0
    ok[0, 0:k0] = 1.0
    return jnp.asarray(eq, BF16), jnp.asarray(ek, BF16), jnp.asarray(oq), jnp.asarray(ok)


def _inproj(x, g, w_main, w_f, b_f, heads, tm):
    B, S, D = x.shape
    n_main = w_main.shape[1]
    tri = jnp.asarray(np.tril(np.ones((tm, tm), np.float32)), BF16)
    eq, ek, oq, ok = _aug_constants(heads)
    const = lambda r, c: pl.BlockSpec((r, c), lambda b, i: (0, 0))
    return pl.pallas_call(
        _inproj_kernel,
        grid=(B, S // tm),
        in_specs=[
            pl.BlockSpec((1, tm, D), lambda b, i: (b, i, 0)),
            const(1, D), const(D, n_main), const(D, LANES), const(1, LANES), const(tm, tm),
            const(PIECES * LANES, LANES), const(PIECES * LANES, LANES), const(1, LANES), const(1, LANES),
        ],
        out_specs=[
            pl.BlockSpec((1, tm, n_main), lambda b, i: (b, i, 0)),
            pl.BlockSpec((1, tm, LANES), lambda b, i: (b, i, 0)),
            pl.BlockSpec((1, tm, LANES), lambda b, i: (b, i, 0)),
        ],
        out_shape=[
            jax.ShapeDtypeStruct((B, S, n_main), BF16),
            jax.ShapeDtypeStruct((B, S, LANES), BF16),
            jax.ShapeDtypeStruct((B, S, LANES), BF16),
        ],
        scratch_shapes=[pltpu.VMEM((1, LANES), F32)],
        compiler_params=pltpu.CompilerParams(
            dimension_semantics=("arbitrary", "arbitrary"), vmem_limit_bytes=VMEM_LIMIT),
        name="inproj",
    )(x, g, w_main, w_f, b_f, tri, eq, ek, oq, ok)


def _head_mask(lane, hh):
    return lane < HEAD_DIM if hh == 0 else lane >= HEAD_DIM


def _value_aug(vv, lane, hh):
    ones = jnp.where(lane == SUM_LANE[hh], 1.0, 0.0).astype(vv.dtype)
    return jnp.where(_head_mask(lane, hh), vv, ones)


def _pair_output(acc0, acc1, lane):
    l0 = acc0[:, SUM_LANE[0]:SUM_LANE[0] + 1]
    l1 = acc1[:, SUM_LANE[1]:SUM_LANE[1] + 1]
    return jnp.where(_head_mask(lane, 0), acc0 * (1.0 / l0), acc1 * (1.0 / l1))


def _fox_kernel(q_ref, qa_ref, k_ref, ka_ref, v_ref, o_ref, *, tq, pp, heads):
    grp = pl.program_id(1)
    i = pl.program_id(2)
    lane = lax.broadcasted_iota(jnp.int32, (1, LANES), 1)
    k0 = PIECES * heads
    qa = qa_ref[0]
    qs = []
    for pi in range(pp):
        q = q_ref[0, :, pi * PAIR:(pi + 1) * PAIR]
        for hh in range(2):
            a0 = PIECES * (2 * (grp * pp + pi) + hh)
            amask = ((lane >= a0) & (lane < a0 + PIECES)) | ((lane >= k0 + a0) & (lane < k0 + a0 + PIECES))
            qs.append(jnp.concatenate([jnp.where(_head_mask(lane, hh), q, jnp.zeros_like(q)),
                                       jnp.where(amask, qa, jnp.zeros_like(qa))], axis=-1))

    def block(ks, mask, state):
        ka = ka_ref[0, pl.ds(ks, tq), :]
        new = []
        for pi in range(pp):
            kaug = jnp.concatenate([k_ref[0, pl.ds(ks, tq), pi * PAIR:(pi + 1) * PAIR], ka], axis=-1)
            vv = v_ref[0, pl.ds(ks, tq), pi * PAIR:(pi + 1) * PAIR]
            for hh in range(2):
                m, acc = state[2 * pi + hh]
                s = _dot_nt(qs[2 * pi + hh], kaug)
                if mask is not None:
                    s = jnp.where(mask, s, NEG)
                mn = jnp.maximum(m, jnp.max(s, axis=-1, keepdims=True))
                p = jnp.exp2(s - mn).astype(BF16)
                acc = jnp.exp2(m - mn) * acc + _dot(p, _value_aug(vv, lane, hh))
                new.append((mn, acc))
        return tuple(new)

    init = tuple((jnp.full((tq, 1), NEG, F32), jnp.zeros((tq, PAIR), F32)) for _ in range(2 * pp))
    causal = lax.broadcasted_iota(jnp.int32, (tq, tq), 0) >= lax.broadcasted_iota(jnp.int32, (tq, tq), 1)
    state = block(pl.multiple_of(i * tq, tq), causal, init)
    state = lax.fori_loop(0, i, lambda j, st: block(pl.multiple_of(j * tq, tq), None, st), state)
    for pi in range(pp):
        o_ref[0, :, pi * PAIR:(pi + 1) * PAIR] = _pair_output(
            state[2 * pi][1], state[2 * pi + 1][1], lane).astype(o_ref.dtype)


def _fox(proj, qa, ka, heads, tq, pp):
    B, S, _ = proj.shape
    groups = heads // 2 // pp
    w = pp * PAIR
    return pl.pallas_call(
        functools.partial(_fox_kernel, tq=tq, pp=pp, heads=heads),
        grid=(B, groups, S // tq),
        in_specs=[
            pl.BlockSpec((1, tq, w), lambda b, g, i: (b, i, g)),
            pl.BlockSpec((1, tq, LANES), lambda b, g, i: (b, i, 0)),
            pl.BlockSpec((1, S, w), lambda b, g, i: (b, 0, groups + g)),
            pl.BlockSpec((1, S, LANES), lambda b, g, i: (b, 0, 0)),
            pl.BlockSpec((1, S, w), lambda b, g, i: (b, 0, 2 * groups + g)),
        ],
        out_specs=pl.BlockSpec((1, tq, w), lambda b, g, i: (b, i, g)),
        out_shape=jax.ShapeDtypeStruct((B, S, heads * HEAD_DIM), BF16),
        compiler_params=pltpu.CompilerParams(
            dimension_semantics=("arbitrary", "arbitrary", "arbitrary"), vmem_limit_bytes=VMEM_LIMIT),
        name="fox_attn",
    )(proj, qa, proj, ka, proj)


def _chunk_kernel(q_ref, k_ref, v_ref, g_ref, o_ref, bias_ref, *, tq, pp):
    i = pl.program_id(2)
    lane = lax.broadcasted_iota(jnp.int32, (1, LANES), 1)

    @pl.when((pl.program_id(1) == 0) & (i == 0))
    def _():
        q_chunk = lax.broadcasted_iota(jnp.int32, (tq, tq), 0) // CHUNK
        k_chunk = lax.broadcasted_iota(jnp.int32, (tq, tq), 1) // CHUNK
        for which in range(3):
            back = which * (tq // CHUNK) + q_chunk - k_chunk
            vis = (back >= 0) & (back <= LEFT_CHUNKS)
            for hd in range(2 * pp):
                g = jnp.broadcast_to(g_ref[0, which, hd], (tq, 2 * tq))
                toep = pltpu.roll(g, 0, 1, stride=1, stride_axis=0)[:, :tq]
                bias_ref[which, hd] = jnp.where(vis, toep, NEG)

    def attend(n_blocks):
        for pi in range(pp):
            q = q_ref[0, :, pi * PAIR:(pi + 1) * PAIR]
            accs = []
            for hh in range(2):
                qh = jnp.where(_head_mask(lane, hh), q, jnp.zeros_like(q))
                ss = []
                for which in range(n_blocks):
                    ks = pl.multiple_of((i - which) * tq, tq)
                    kk = k_ref[0, pl.ds(ks, tq), pi * PAIR:(pi + 1) * PAIR]
                    ss.append(_dot_nt(qh, kk) + bias_ref[which, 2 * pi + hh])
                m = functools.reduce(jnp.maximum, [jnp.max(s, axis=-1, keepdims=True) for s in ss])
                acc = None
                for which in range(n_blocks):
                    ks = pl.multiple_of((i - which) * tq, tq)
                    vv = v_ref[0, pl.ds(ks, tq), pi * PAIR:(pi + 1) * PAIR]
                    part = _dot(jnp.exp2(ss[which] - m).astype(BF16), _value_aug(vv, lane, hh))
                    acc = part if acc is None else acc + part
                accs.append(acc)
            o_ref[0, :, pi * PAIR:(pi + 1) * PAIR] = _pair_output(accs[0], accs[1], lane).astype(o_ref.dtype)

    for n_blocks in (1, 2):
        pl.when(i == n_blocks - 1)(functools.partial(attend, n_blocks))
    pl.when(i >= 2)(functools.partial(attend, 3))


def _chunk_bias_rows(rel_table, tq, pp):
    heads = rel_table.shape[0]
    m = np.arange(2 * tq)
    offset = np.where(m < tq, -m, 2 * tq - m)
    idx = np.stack([np.clip(which * tq + offset, -MAX_REL, MAX_REL) + MAX_REL for which in range(3)])
    rows = rel_table.astype(F32)[:, idx] * LOG2E
    return rows.reshape(heads // (2 * pp), 2 * pp, 3, 1, 2 * tq).transpose(0, 2, 1, 3, 4)


def _chunk(proj, bias_rows, tq, pp, col0):
    B, S, _ = proj.shape
    groups = bias_rows.shape[0]
    w = pp * PAIR
    c0 = col0 // pp
    assert tq % CHUNK == 0 and LEFT_CHUNKS * CHUNK <= 2 * tq, "band must fit in three key blocks"
    return pl.pallas_call(
        functools.partial(_chunk_kernel, tq=tq, pp=pp),
        grid=(groups, B, S // tq),
        in_specs=[
            pl.BlockSpec((1, tq, w), lambda g, b, i: (b, i, c0 + g)),
            pl.BlockSpec((1, S, w), lambda g, b, i: (b, 0, c0 + groups + g)),
            pl.BlockSpec((1, S, w), lambda g, b, i: (b, 0, c0 + 2 * groups + g)),
            pl.BlockSpec((1, 3, 2 * pp, 1, 2 * tq), lambda g, b, i: (g, 0, 0, 0, 0)),
        ],
        out_specs=pl.BlockSpec((1, tq, w), lambda g, b, i: (b, i, g)),
        out_shape=jax.ShapeDtypeStruct((B, S, groups * w), BF16),
        scratch_shapes=[pltpu.VMEM((3, 2 * pp, tq, tq), F32)],
        compiler_params=pltpu.CompilerParams(
            dimension_semantics=("arbitrary", "arbitrary", "arbitrary"), vmem_limit_bytes=VMEM_LIMIT),
        name="chunk_attn",
    )(proj, proj, proj, bias_rows)


def _memkv_kernel(mem_ref, g_ref, wk_ref, wv_ref, k_ref, v_ref):
    mb = _rms(mem_ref[0], g_ref[...]).astype(BF16)
    k_ref[0] = _dot(mb, wk_ref[...]).astype(BF16)
    v_ref[0] = _dot(mb, wv_ref[...]).astype(BF16)


def _memkv(mem, g, wk, wv):
    B, M, D = mem.shape
    full = lambda b: (0, 0)
    return pl.pallas_call(
        _memkv_kernel,
        grid=(B,),
        in_specs=[
            pl.BlockSpec((1, M, D), lambda b: (b, 0, 0)),
            pl.BlockSpec((1, D), full),
            pl.BlockSpec((D, D), full),
            pl.BlockSpec((D, D), full),
        ],
        out_specs=[pl.BlockSpec((1, M, D), lambda b: (b, 0, 0))] * 2,
        out_shape=[jax.ShapeDtypeStruct((B, M, D), BF16)] * 2,
        compiler_params=pltpu.CompilerParams(
            dimension_semantics=("arbitrary",), vmem_limit_bytes=VMEM_LIMIT),
        name="mem_kv",
    )(mem, g, wk, wv)


def _mix_mem_kernel(yf_ref, yc_ref, x_ref, gf_ref, gc_ref, wo_ref, gpost_ref,
                    gpre_ref, wq_ref, mk_ref, mv_ref, wmo_ref, gmpost_ref, o_ref):
    d_fox = yf_ref.shape[2]
    yf = _rms(yf_ref[0].astype(F32), gf_ref[...]).astype(BF16)
    yc = _rms(yc_ref[0].astype(F32), gc_ref[...]).astype(BF16)
    y = _dot(yf, wo_ref[0:d_fox, :]) + _dot(yc, wo_ref[d_fox:, :])
    x1 = x_ref[0] + _rms(y, gpost_ref[...])

    h = _rms(x1, gpre_ref[...]).astype(BF16)
    q = _dot(h, wq_ref[...]).astype(BF16)
    dh = q.shape[1] // MEM_HEADS
    outs = []
    for hh in range(MEM_HEADS):
        sl = slice(hh * dh, (hh + 1) * dh)
        s = _dot_nt(q[:, sl], mk_ref[0, :, sl])
        pr = jnp.exp2(s - jnp.max(s, axis=-1, keepdims=True))
        inv = 1.0 / jnp.sum(pr, axis=-1, keepdims=True)
        outs.append((_dot(pr.astype(BF16), mv_ref[0, :, sl]) * inv).astype(BF16))
    o = jnp.concatenate(outs, axis=-1)
    y2 = _dot(o, wmo_ref[...])
    o_ref[0] = x1 + _rms(y2, gmpost_ref[...])


def _mix_mem(yf, yc, x, gf, gc, wo, gpost, gpre, wq, mk, mv, wmo, gmpost, tm):
    B, S, D = x.shape
    M = mk.shape[1]
    dg = yf.shape[2]
    vec = lambda n: pl.BlockSpec((1, n), lambda b, i: (0, 0))
    mat = lambda r, c: pl.BlockSpec((r, c), lambda b, i: (0, 0))
    return pl.pallas_call(
        _mix_mem_kernel,
        grid=(B, S // tm),
        in_specs=[
            pl.BlockSpec((1, tm, dg), lambda b, i: (b, i, 0)),
            pl.BlockSpec((1, tm, dg), lambda b, i: (b, i, 0)),
            pl.BlockSpec((1, tm, D), lambda b, i: (b, i, 0)),
            vec(dg), vec(dg), mat(2 * dg, D), vec(D),
            vec(D), mat(D, D),
            pl.BlockSpec((1, M, D), lambda b, i: (b, 0, 0)),
            pl.BlockSpec((1, M, D), lambda b, i: (b, 0, 0)),
            mat(D, D), vec(D),
        ],
        out_specs=pl.BlockSpec((1, tm, D), lambda b, i: (b, i, 0)),
        out_shape=jax.ShapeDtypeStruct((B, S, D), F32),
        compiler_params=pltpu.CompilerParams(
            dimension_semantics=("arbitrary", "arbitrary"), vmem_limit_bytes=VMEM_LIMIT),
        name="mix_mem",
    )(yf, yc, x, gf, gc, wo, gpost, gpre, wq, mk, mv, wmo, gmpost)


def _mlp_kernel(x_ref, gpre_ref, w1_ref, w2_ref, gpost_ref, o_ref, *, ff_block):
    x = x_ref[...]
    h = _rms(x, gpre_ref[...]).astype(BF16)
    d_ff = w1_ref.shape[1]
    y = None
    for n in range(0, d_ff, ff_block):
        a = jnp.maximum(_dot(h, w1_ref[:, n:n + ff_block]), 0.0)
        part = _dot((a * a).astype(BF16), w2_ref[n:n + ff_block, :])
        y = part if y is None else y + part
    o_ref[...] = x + _rms(y, gpost_ref[...])


def _mlp(x, gpre, w1, w2, gpost, tm, ff_block):
    R, D = x.shape
    d_ff = w1.shape[1]
    return pl.pallas_call(
        functools.partial(_mlp_kernel, ff_block=ff_block),
        grid=(R // tm,),
        in_specs=[
            pl.BlockSpec((tm, D), lambda i: (i, 0)),
            pl.BlockSpec((1, D), lambda i: (0, 0)),
            pl.BlockSpec((D, d_ff), lambda i: (0, 0)),
            pl.BlockSpec((d_ff, D), lambda i: (0, 0)),
            pl.BlockSpec((1, D), lambda i: (0, 0)),
        ],
        out_specs=pl.BlockSpec((tm, D), lambda i: (i, 0)),
        out_shape=jax.ShapeDtypeStruct((R, D), F32),
        compiler_params=pltpu.CompilerParams(
            dimension_semantics=("arbitrary",), vmem_limit_bytes=VMEM_LIMIT),
        name="mlp",
    )(x, gpre, w1, w2, gpost)


def kernel(x, mem, w_in, b_fgt, rel_bias, g_fox_out, g_chk_out, w_out, g_mix_pre, g_mix_post,
           g_mem_kv, w_mq, w_mk, w_mv, w_mo, g_mem_pre, g_mem_post,
           w_ff1, w_ff2, g_ff_pre, g_ff_post):
    B, S, D = x.shape
    depth = w_in.shape[0]
    heads = b_fgt.shape[1]
    d_fox = heads * HEAD_DIM
    d_chk = rel_bias.shape[1] * HEAD_DIM
    assert d_fox == d_chk and heads % 2 == 0 and heads <= LANES
    pairs = heads // 2
    row = lambda v: v.reshape(1, -1)

    for l in range(depth):
        scale = HEAD_DIM ** -0.5 * LOG2E
        wl = w_in[l]
        o_f = 3 * d_fox
        w_main = jnp.concatenate([
            wl[:, :d_fox] * scale, wl[:, d_fox:o_f],
            wl[:, o_f + heads:o_f + heads + d_chk] * scale, wl[:, o_f + heads + d_chk:],
        ], axis=1).astype(BF16)
        w_f = jnp.pad(wl[:, o_f:o_f + heads], ((0, 0), (0, LANES - heads))).astype(BF16)
        b_f = jnp.pad(b_fgt[l], (0, LANES - heads)).reshape(1, LANES)

        proj, qa, ka = _inproj(x, row(g_mix_pre[l]), w_main, w_f, b_f, heads, tm=512)
        y_fox = _fox(proj, qa, ka, heads, tq=512, pp=2)
        y_chk = _chunk(proj, _chunk_bias_rows(rel_bias[l], 256, 2), tq=256, pp=2, col0=3 * pairs)

        mk, mv = _memkv(mem, row(g_mem_kv[l]), w_mk[l].astype(BF16), w_mv[l].astype(BF16))
        mem_scale = (D // MEM_HEADS) ** -0.5 * LOG2E
        x = _mix_mem(y_fox, y_chk, x, row(g_fox_out[l]), row(g_chk_out[l]), w_out[l].astype(BF16),
                     row(g_mix_post[l]), row(g_mem_pre[l]), (w_mq[l] * mem_scale).astype(BF16),
                     mk, mv, w_mo[l].astype(BF16), row(g_mem_post[l]), tm=256)
        x = _mlp(x.reshape(B * S, D), row(g_ff_pre[l]), w_ff1[l].astype(BF16), w_ff2[l].astype(BF16),
                 row(g_ff_post[l]), tm=256, ff_block=1024).reshape(B, S, D)
    return x
```

```python
import functools
import math

import numpy as np
import jax
import jax.numpy as jnp
from jax import lax
from jax.experimental import pallas as pl
from jax.experimental.pallas import tpu as pltpu

F32 = jnp.float32
BF16 = jnp.bfloat16

EPS = 1e-6
HEAD_DIM = 64
PAIR = 2 * HEAD_DIM
CHUNK = 64
LEFT_CHUNKS = 8
MAX_REL = 128
MEM_HEADS = 4
NEG = -1e30
LOG2E = math.log2(math.e)
LANES = 128
VMEM_LIMIT = 56 * 1024 * 1024
PIECES = 3
SUM_LANE = (HEAD_DIM, 0)


def _rms(x, g):
    return x * lax.rsqrt(jnp.mean(x * x, axis=-1, keepdims=True) + EPS) * g


def _dot(a, b):
    return jnp.dot(a, b, preferred_element_type=F32)


def _dot_nt(a, b):
    return lax.dot_general(a, b, (((1,), (1,)), ((), ())), preferred_element_type=F32)


def _split3(x):
    hi = x.astype(BF16)
    r1 = x - hi.astype(F32)
    mid = r1.astype(BF16)
    lo = (r1 - mid.astype(F32)).astype(BF16)
    return hi, mid, lo


def _inproj_kernel(x_ref, g_ref, w_ref, wf_ref, bf_ref, tri_ref, eq_ref, ek_ref, oq_ref, ok_ref,
                   proj_ref, qa_ref, ka_ref, carry_ref):
    @pl.when(pl.program_id(1) == 0)
    def _():
        carry_ref[...] = jnp.zeros_like(carry_ref)

    hb = _rms(x_ref[0], g_ref[...]).astype(BF16)
    proj_ref[0] = _dot(hb, w_ref[...]).astype(BF16)

    f = _dot(hb, wf_ref[...]) + bf_ref[...]
    logf = jnp.minimum(f, 0.0) - jnp.log1p(jnp.exp(-jnp.abs(f)))
    tri = tri_ref[...]
    c = sum(_dot(tri, piece) for piece in _split3(logf)) + carry_ref[...]
    tm = c.shape[0]
    carry_ref[...] = c[tm - 1:tm, :]
    pieces = jnp.concatenate(_split3(c * LOG2E), axis=-1)
    qa_ref[0] = (_dot(pieces, eq_ref[...]) + oq_ref[...]).astype(BF16)
    ka_ref[0] = (ok_ref[...] - _dot(pieces, ek_ref[...])).astype(BF16)


def _aug_constants(heads):
    k0 = PIECES * heads
    assert 2 * k0 <= LANES
    eq = np.zeros((PIECES * LANES, LANES), np.float32)
    ek = np.zeros((PIECES * LANES, LANES), np.float32)
    for h in range(heads):
        for r in range(PIECES):
            eq[r * LANES + h, PIECES * h + r] = 1.0
            ek[r * LANES + h, k0 + PIECES * h + r] = 1.0
    oq = np.zeros((1, LANES), np.float32)
    ok = np.zeros((1, LANES), np.float32)
    oq[0, k0:2 * k0] = 1.0
    ok[0, 0:k0] = 1.0
    return jnp.asarray(eq, BF16), jnp.asarray(ek, BF16), jnp.asarray(oq), jnp.asarray(ok)


def _inproj(x, g, w_main, w_f, b_f, heads, tm):
    B, S, D = x.shape
    n_main = w_main.shape[1]
    tri = jnp.asarray(np.tril(np.ones((tm, tm), np.float32)), BF16)
    eq, ek, oq, ok = _aug_constants(heads)
    const = lambda r, c: pl.BlockSpec((r, c), lambda b, i: (0, 0))
    return pl.pallas_call(
        _inproj_kernel,
        grid=(B, S // tm),
        in_specs=[
            pl.BlockSpec((1, tm, D), lambda b, i: (b, i, 0)),
            const(1, D), const(D, n_main), const(D, LANES), const(1, LANES), const(tm, tm),
            const(PIECES * LANES, LANES), const(PIECES * LANES, LANES), const(1, LANES), const(1, LANES),
        ],
        out_specs=[
            pl.BlockSpec((1, tm, n_main), lambda b, i: (b, i, 0)),
            pl.BlockSpec((1, tm, LANES), lambda b, i: (b, i, 0)),
            pl.BlockSpec((1, tm, LANES), lambda b, i: (b, i, 0)),
        ],
        out_shape=[
            jax.ShapeDtypeStruct((B, S, n_main), BF16),
            jax.ShapeDtypeStruct((B, S, LANES), BF16),
            jax.ShapeDtypeStruct((B, S, LANES), BF16),
        ],
        scratch_shapes=[pltpu.VMEM((1, LANES), F32)],
        compiler_params=pltpu.CompilerParams(
            dimension_semantics=("arbitrary", "arbitrary"), vmem_limit_bytes=VMEM_LIMIT),
        name="inproj",
    )(x, g, w_main, w_f, b_f, tri, eq, ek, oq, ok)


def _head_mask(lane, hh):
    return lane < HEAD_DIM if hh == 0 else lane >= HEAD_DIM


def _value_aug(vv, lane, hh):
    ones = jnp.where(lane == SUM_LANE[hh], 1.0, 0.0).astype(vv.dtype)
    return jnp.where(_head_mask(lane, hh), vv, ones)


def _pair_output(acc0, acc1, lane):
    l0 = acc0[:, SUM_LANE[0]:SUM_LANE[0] + 1]
    l1 = acc1[:, SUM_LANE[1]:SUM_LANE[1] + 1]
    return jnp.where(_head_mask(lane, 0), acc0 * (1.0 / l0), acc1 * (1.0 / l1))


def _fox_kernel(q_ref, qa_ref, k_ref, ka_ref, v_ref, o_ref, *, tq, rb, pp, heads):
    grp = pl.program_id(1)
    lane = lax.broadcasted_iota(jnp.int32, (1, LANES), 1)
    k0 = PIECES * heads
    causal = lax.broadcasted_iota(jnp.int32, (rb, rb), 0) >= lax.broadcasted_iota(jnp.int32, (rb, rb), 1)

    def attend(n):
        for pi in range(pp):
            cols = slice(pi * PAIR, (pi + 1) * PAIR)
            for r in range(tq // rb):
                rows = slice(r * rb, (r + 1) * rb)
                past = n * tq + r * rb
                q = q_ref[0, rows, cols]
                qa = qa_ref[0, rows, :]
                accs = []
                for hh in range(2):
                    a0 = PIECES * (2 * (grp * pp + pi) + hh)
                    amask = (((lane >= a0) & (lane < a0 + PIECES))
                             | ((lane >= k0 + a0) & (lane < k0 + a0 + PIECES)))
                    q_aug = jnp.concatenate([jnp.where(_head_mask(lane, hh), q, jnp.zeros_like(q)),
                                             jnp.where(amask, qa, jnp.zeros_like(qa))], axis=-1)
                    spans = ([(0, past)] if past else []) + [(past, past + rb)]
                    ss = []
                    for lo, hi in spans:
                        kaug = jnp.concatenate([k_ref[0, lo:hi, cols], ka_ref[0, lo:hi, :]], axis=-1)
                        ss.append(_dot_nt(q_aug, kaug))
                    ss[-1] = jnp.where(causal, ss[-1], NEG)
                    m = functools.reduce(jnp.maximum, [jnp.max(s, axis=-1, keepdims=True) for s in ss])
                    acc = None
                    for (lo, hi), s in zip(spans, ss):
                        part = _dot(jnp.exp2(s - m).astype(BF16), _value_aug(v_ref[0, lo:hi, cols], lane, hh))
                        acc = part if acc is None else acc + part
                    accs.append(acc)
                o_ref[0, rows, cols] = _pair_output(accs[0], accs[1], lane).astype(o_ref.dtype)

    for n in range(k_ref.shape[1] // tq):
        pl.when(pl.program_id(2) == n)(functools.partial(attend, n))


def _fox(proj, qa, ka, heads, tq, rb, pp):
    B, S, _ = proj.shape
    groups = heads // 2 // pp
    w = pp * PAIR
    return pl.pallas_call(
        functools.partial(_fox_kernel, tq=tq, rb=rb, pp=pp, heads=heads),
        grid=(B, groups, S // tq),
        in_specs=[
            pl.BlockSpec((1, tq, w), lambda b, g, i: (b, i, g)),
            pl.BlockSpec((1, tq, LANES), lambda b, g, i: (b, i, 0)),
            pl.BlockSpec((1, S, w), lambda b, g, i: (b, 0, groups + g)),
            pl.BlockSpec((1, S, LANES), lambda b, g, i: (b, 0, 0)),
            pl.BlockSpec((1, S, w), lambda b, g, i: (b, 0, 2 * groups + g)),
        ],
        out_specs=pl.BlockSpec((1, tq, w), lambda b, g, i: (b, i, g)),
        out_shape=jax.ShapeDtypeStruct((B, S, heads * HEAD_DIM), BF16),
        compiler_params=pltpu.CompilerParams(
            dimension_semantics=("arbitrary", "arbitrary", "arbitrary"), vmem_limit_bytes=VMEM_LIMIT),
        name="fox_attn",
    )(proj, qa, proj, ka, proj)


def _chunk_kernel(q_ref, k_ref, v_ref, g_ref, o_ref, bias_ref, *, tq, pp):
    i = pl.program_id(2)
    lane = lax.broadcasted_iota(jnp.int32, (1, LANES), 1)

    @pl.when((pl.program_id(1) == 0) & (i == 0))
    def _():
        q_chunk = lax.broadcasted_iota(jnp.int32, (tq, tq), 0) // CHUNK
        k_chunk = lax.broadcasted_iota(jnp.int32, (tq, tq), 1) // CHUNK
        for which in range(3):
            back = which * (tq // CHUNK) + q_chunk - k_chunk
            vis = (back >= 0) & (back <= LEFT_CHUNKS)
            for hd in range(2 * pp):
                g = jnp.broadcast_to(g_ref[0, which, hd], (tq, 2 * tq))
                toep = pltpu.roll(g, 0, 1, stride=1, stride_axis=0)[:, :tq]
                bias_ref[hd, :, (2 - which) * tq:(3 - which) * tq] = jnp.where(vis, toep, NEG)

    def attend(n_blocks):
        span = n_blocks * tq
        ks = pl.multiple_of((i - (n_blocks - 1)) * tq, tq)
        for pi in range(pp):
            cols = slice(pi * PAIR, (pi + 1) * PAIR)
            q = q_ref[0, :, cols]
            kk = k_ref[0, pl.ds(ks, span), cols]
            vv = v_ref[0, pl.ds(ks, span), cols]
            accs = []
            for hh in range(2):
                qh = jnp.where(_head_mask(lane, hh), q, jnp.zeros_like(q))
                s = _dot_nt(qh, kk) + bias_ref[2 * pi + hh, :, 3 * tq - span:]
                p = jnp.exp2(s - jnp.max(s, axis=-1, keepdims=True)).astype(BF16)
                accs.append(_dot(p, _value_aug(vv, lane, hh)))
            o_ref[0, :, cols] = _pair_output(accs[0], accs[1], lane).astype(o_ref.dtype)

    for n_blocks in (1, 2):
        pl.when(i == n_blocks - 1)(functools.partial(attend, n_blocks))
    pl.when(i >= 2)(functools.partial(attend, 3))


def _chunk_bias_rows(rel_table, tq, pp):
    heads = rel_table.shape[0]
    m = np.arange(2 * tq)
    offset = np.where(m < tq, -m, 2 * tq - m)
    idx = np.stack([np.clip(which * tq + offset, -MAX_REL, MAX_REL) + MAX_REL for which in range(3)])
    rows = rel_table.astype(F32)[:, idx] * LOG2E
    return rows.reshape(heads // (2 * pp), 2 * pp, 3, 1, 2 * tq).transpose(0, 2, 1, 3, 4)


def _chunk(proj, bias_rows, tq, pp, col0):
    B, S, _ = proj.shape
    groups = bias_rows.shape[0]
    w = pp * PAIR
    c0 = col0 // pp
    assert tq % CHUNK == 0 and LEFT_CHUNKS * CHUNK <= 2 * tq, "band must fit in three key blocks"
    return pl.pallas_call(
        functools.partial(_chunk_kernel, tq=tq, pp=pp),
        grid=(groups, B, S // tq),
        in_specs=[
            pl.BlockSpec((1, tq, w), lambda g, b, i: (b, i, c0 + g)),
            pl.BlockSpec((1, S, w), lambda g, b, i: (b, 0, c0 + groups + g)),
            pl.BlockSpec((1, S, w), lambda g, b, i: (b, 0, c0 + 2 * groups + g)),
            pl.BlockSpec((1, 3, 2 * pp, 1, 2 * tq), lambda g, b, i: (g, 0, 0, 0, 0)),
        ],
        out_specs=pl.BlockSpec((1, tq, w), lambda g, b, i: (b, i, g)),
        out_shape=jax.ShapeDtypeStruct((B, S, groups * w), BF16),
        scratch_shapes=[pltpu.VMEM((2 * pp, tq, 3 * tq), F32)],
        compiler_params=pltpu.CompilerParams(
            dimension_semantics=("arbitrary", "arbitrary", "arbitrary"), vmem_limit_bytes=VMEM_LIMIT),
        name="chunk_attn",
    )(proj, proj, proj, bias_rows)


def _memkv_kernel(mem_ref, g_ref, wk_ref, wv_ref, k_ref, v_ref):
    mb = _rms(mem_ref[0], g_ref[...]).astype(BF16)
    k_ref[0] = _dot(mb, wk_ref[...]).astype(BF16)
    v_ref[0] = _dot(mb, wv_ref[...]).astype(BF16)


def _memkv(mem, g, wk, wv):
    B, M, D = mem.shape
    full = lambda b: (0, 0)
    return pl.pallas_call(
        _memkv_kernel,
        grid=(B,),
        in_specs=[
            pl.BlockSpec((1, M, D), lambda b: (b, 0, 0)),
            pl.BlockSpec((1, D), full),
            pl.BlockSpec((D, D), full),
            pl.BlockSpec((D, D), full),
        ],
        out_specs=[pl.BlockSpec((1, M, D), lambda b: (b, 0, 0))] * 2,
        out_shape=[jax.ShapeDtypeStruct((B, M, D), BF16)] * 2,
        compiler_params=pltpu.CompilerParams(
            dimension_semantics=("arbitrary",), vmem_limit_bytes=VMEM_LIMIT),
        name="mem_kv",
    )(mem, g, wk, wv)


def _mix_mem_kernel(yf_ref, yc_ref, x_ref, gf_ref, gc_ref, wo_ref, gpost_ref,
                    gpre_ref, wq_ref, mk_ref, mv_ref, wmo_ref, gmpost_ref, o_ref):
    d_fox = yf_ref.shape[2]
    yf = _rms(yf_ref[0].astype(F32), gf_ref[...]).astype(BF16)
    yc = _rms(yc_ref[0].astype(F32), gc_ref[...]).astype(BF16)
    y = _dot(yf, wo_ref[0:d_fox, :]) + _dot(yc, wo_ref[d_fox:, :])
    x1 = x_ref[0] + _rms(y, gpost_ref[...])

    h = _rms(x1, gpre_ref[...]).astype(BF16)
    q = _dot(h, wq_ref[...]).astype(BF16)
    dh = q.shape[1] // MEM_HEADS
    outs = []
    for hh in range(MEM_HEADS):
        sl = slice(hh * dh, (hh + 1) * dh)
        s = _dot_nt(q[:, sl], mk_ref[0, :, sl])
        pr = jnp.exp2(s - jnp.max(s, axis=-1, keepdims=True))
        inv = 1.0 / jnp.sum(pr, axis=-1, keepdims=True)
        outs.append((_dot(pr.astype(BF16), mv_ref[0, :, sl]) * inv).astype(BF16))
    o = jnp.concatenate(outs, axis=-1)
    y2 = _dot(o, wmo_ref[...])
    o_ref[0] = x1 + _rms(y2, gmpost_ref[...])


def _mix_mem(yf, yc, x, gf, gc, wo, gpost, gpre, wq, mk, mv, wmo, gmpost, tm):
    B, S, D = x.shape
    M = mk.shape[1]
    dg = yf.shape[2]
    vec = lambda n: pl.BlockSpec((1, n), lambda b, i: (0, 0))
    mat = lambda r, c: pl.BlockSpec((r, c), lambda b, i: (0, 0))
    return pl.pallas_call(
        _mix_mem_kernel,
        grid=(B, S // tm),
        in_specs=[
            pl.BlockSpec((1, tm, dg), lambda b, i: (b, i, 0)),
            pl.BlockSpec((1, tm, dg), lambda b, i: (b, i, 0)),
            pl.BlockSpec((1, tm, D), lambda b, i: (b, i, 0)),
            vec(dg), vec(dg), mat(2 * dg, D), vec(D),
            vec(D), mat(D, D),
            pl.BlockSpec((1, M, D), lambda b, i: (b, 0, 0)),
            pl.BlockSpec((1, M, D), lambda b, i: (b, 0, 0)),
            mat(D, D), vec(D),
        ],
        out_specs=pl.BlockSpec((1, tm, D), lambda b, i: (b, i, 0)),
        out_shape=jax.ShapeDtypeStruct((B, S, D), F32),
        compiler_params=pltpu.CompilerParams(
            dimension_semantics=("arbitrary", "arbitrary"), vmem_limit_bytes=VMEM_LIMIT),
        name="mix_mem",
    )(yf, yc, x, gf, gc, wo, gpost, gpre, wq, mk, mv, wmo, gmpost)


def _mlp_kernel(x_ref, gpre_ref, w1_ref, w2_ref, gpost_ref, o_ref, *, ff_block):
    x = x_ref[...]
    h = _rms(x, gpre_ref[...]).astype(BF16)
    d_ff = w1_ref.shape[1]
    y = None
    for n in range(0, d_ff, ff_block):
        a = jnp.maximum(_dot(h, w1_ref[:, n:n + ff_block]), 0.0)
        part = _dot((a * a).astype(BF16), w2_ref[n:n + ff_block, :])
        y = part if y is None else y + part
    o_ref[...] = x + _rms(y, gpost_ref[...])


def _mlp(x, gpre, w1, w2, gpost, tm, ff_block):
    R, D = x.shape
    d_ff = w1.shape[1]
    return pl.pallas_call(
        functools.partial(_mlp_kernel, ff_block=ff_block),
        grid=(R // tm,),
        in_specs=[
            pl.BlockSpec((tm, D), lambda i: (i, 0)),
            pl.BlockSpec((1, D), lambda i: (0, 0)),
            pl.BlockSpec((D, d_ff), lambda i: (0, 0)),
            pl.BlockSpec((d_ff, D), lambda i: (0, 0)),
            pl.BlockSpec((1, D), lambda i: (0, 0)),
        ],
        out_specs=pl.BlockSpec((tm, D), lambda i: (i, 0)),
        out_shape=jax.ShapeDtypeStruct((R, D), F32),
        compiler_params=pltpu.CompilerParams(
            dimension_semantics=("arbitrary",), vmem_limit_bytes=VMEM_LIMIT),
        name="mlp",
    )(x, gpre, w1, w2, gpost)


def kernel(x, mem, w_in, b_fgt, rel_bias, g_fox_out, g_chk_out, w_out, g_mix_pre, g_mix_post,
           g_mem_kv, w_mq, w_mk, w_mv, w_mo, g_mem_pre, g_mem_post,
           w_ff1, w_ff2, g_ff_pre, g_ff_post):
    B, S, D = x.shape
    depth = w_in.shape[0]
    heads = b_fgt.shape[1]
    d_fox = heads * HEAD_DIM
    d_chk = rel_bias.shape[1] * HEAD_DIM
    assert d_fox == d_chk and heads % 2 == 0 and heads <= LANES
    pairs = heads // 2
    row = lambda v: v.reshape(1, -1)

    for l in range(depth):
        scale = HEAD_DIM ** -0.5 * LOG2E
        wl = w_in[l]
        o_f = 3 * d_fox
        w_main = jnp.concatenate([
            wl[:, :d_fox] * scale, wl[:, d_fox:o_f],
            wl[:, o_f + heads:o_f + heads + d_chk] * scale, wl[:, o_f + heads + d_chk:],
        ], axis=1).astype(BF16)
        w_f = jnp.pad(wl[:, o_f:o_f + heads], ((0, 0), (0, LANES - heads))).astype(BF16)
        b_f = jnp.pad(b_fgt[l], (0, LANES - heads)).reshape(1, LANES)

        proj, qa, ka = _inproj(x, row(g_mix_pre[l]), w_main, w_f, b_f, heads, tm=512)
        y_fox = _fox(proj, qa, ka, heads, tq=512, rb=256, pp=2)
        y_chk = _chunk(proj, _chunk_bias_rows(rel_bias[l], 256, 4), tq=256, pp=4, col0=3 * pairs)

        mk, mv = _memkv(mem, row(g_mem_kv[l]), w_mk[l].astype(BF16), w_mv[l].astype(BF16))
        mem_scale = (D // MEM_HEADS) ** -0.5 * LOG2E
        x = _mix_mem(y_fox, y_chk, x, row(g_fox_out[l]), row(g_chk_out[l]), w_out[l].astype(BF16),
                     row(g_mix_post[l]), row(g_mem_pre[l]), (w_mq[l] * mem_scale).astype(BF16),
                     mk, mv, w_mo[l].astype(BF16), row(g_mem_post[l]), tm=256)
        x = _mlp(x.reshape(B * S, D), row(g_ff_pre[l]), w_ff1[l].astype(BF16), w_ff2[l].astype(BF16),
                 row(g_ff_post[l]), tm=256, ff_block=1024).reshape(B, S, D)
    return x
```

```python
import functools
import math

import numpy as np
import jax
import jax.numpy as jnp
from jax import lax
from jax.experimental import pallas as pl
from jax.experimental.pallas import tpu as pltpu

F32 = jnp.float32
BF16 = jnp.bfloat16

EPS = 1e-6
HEAD_DIM = 64
PAIR = 2 * HEAD_DIM
CHUNK = 64
LEFT_CHUNKS = 8
MAX_REL = 128
MEM_HEADS = 4
NEG = -1e30
LOG2E = math.log2(math.e)
LANES = 128
VMEM_LIMIT = 56 * 1024 * 1024
PIECES = 3
SUM_LANE = (HEAD_DIM, 0)


def _rms(x, g):
    return x * lax.rsqrt(jnp.mean(x * x, axis=-1, keepdims=True) + EPS) * g


def _dot(a, b):
    return jnp.dot(a, b, preferred_element_type=F32)


def _dot_nt(a, b):
    return lax.dot_general(a, b, (((1,), (1,)), ((), ())), preferred_element_type=F32)


def _split3(x):
    hi = x.astype(BF16)
    r1 = x - hi.astype(F32)
    mid = r1.astype(BF16)
    lo = (r1 - mid.astype(F32)).astype(BF16)
    return hi, mid, lo


def _inproj_kernel(x_ref, g_ref, w_ref, wf_ref, bf_ref, tri_ref, eq_ref, ek_ref, oq_ref, ok_ref,
                   proj_ref, qa_ref, ka_ref, carry_ref):
    @pl.when(pl.program_id(1) == 0)
    def _():
        carry_ref[...] = jnp.zeros_like(carry_ref)

    hb = _rms(x_ref[0], g_ref[...]).astype(BF16)
    proj_ref[0] = _dot(hb, w_ref[...]).astype(BF16)

    f = _dot(hb, wf_ref[...]) + bf_ref[...]
    logf = jnp.minimum(f, 0.0) - jnp.log1p(jnp.exp(-jnp.abs(f)))
    tri = tri_ref[...]
    c = sum(_dot(tri, piece) for piece in _split3(logf)) + carry_ref[...]
    tm = c.shape[0]
    carry_ref[...] = c[tm - 1:tm, :]
    pieces = jnp.concatenate(_split3(c * LOG2E), axis=-1)
    qa_ref[0] = (_dot(pieces, eq_ref[...]) + oq_ref[...]).astype(BF16)
    ka_ref[0] = (ok_ref[...] - _dot(pieces, ek_ref[...])).astype(BF16)


def _aug_constants(heads):
    k0 = PIECES * heads
    assert 2 * k0 <= LANES
    eq = np.zeros((PIECES * LANES, LANES), np.float32)
    ek = np.zeros((PIECES * LANES, LANES), np.float32)
    for h in range(heads):
        for r in range(PIECES):
            eq[r * LANES + h, PIECES * h + r] = 1.0
            ek[r * LANES + h, k0 + PIECES * h + r] = 1.0
    oq = np.zeros((1, LANES), np.float32)
    ok = np.zeros((1, LANES), np.float32)
    oq[0, k0:2 * k0] = 1.0
    ok[0, 0:k0] = 1.0
    return jnp.asarray(eq, BF16), jnp.asarray(ek, BF16), jnp.asarray(oq), jnp.asarray(ok)


def _inproj(x, g, w_main, w_f, b_f, heads, tm):
    B, S, D = x.shape
    n_main = w_main.shape[1]
    tri = jnp.asarray(np.tril(np.ones((tm, tm), np.float32)), BF16)
    eq, ek, oq, ok = _aug_constants(heads)
    const = lambda r, c: pl.BlockSpec((r, c), lambda b, i: (0, 0))
    return pl.pallas_call(
        _inproj_kernel,
        grid=(B, S // tm),
        in_specs=[
            pl.BlockSpec((1, tm, D), lambda b, i: (b, i, 0)),
            const(1, D), const(D, n_main), const(D, LANES), const(1, LANES), const(tm, tm),
            const(PIECES * LANES, LANES), const(PIECES * LANES, LANES), const(1, LANES), const(1, LANES),
        ],
        out_specs=[
            pl.BlockSpec((1, tm, n_main), lambda b, i: (b, i, 0)),
            pl.BlockSpec((1, tm, LANES), lambda b, i: (b, i, 0)),
            pl.BlockSpec((1, tm, LANES), lambda b, i: (b, i, 0)),
        ],
        out_shape=[
            jax.ShapeDtypeStruct((B, S, n_main), BF16),
            jax.ShapeDtypeStruct((B, S, LANES), BF16),
            jax.ShapeDtypeStruct((B, S, LANES), BF16),
        ],
        scratch_shapes=[pltpu.VMEM((1, LANES), F32)],
        compiler_params=pltpu.CompilerParams(
            dimension_semantics=("arbitrary", "arbitrary"), vmem_limit_bytes=VMEM_LIMIT),
        name="inproj",
    )(x, g, w_main, w_f, b_f, tri, eq, ek, oq, ok)


def _head_mask(lane, hh):
    return lane < HEAD_DIM if hh == 0 else lane >= HEAD_DIM


def _value_aug(vv, lane, hh):
    ones = jnp.where(lane == SUM_LANE[hh], 1.0, 0.0).astype(vv.dtype)
    return jnp.where(_head_mask(lane, hh), vv, ones)


def _pair_output(acc0, acc1, lane):
    l0 = acc0[:, SUM_LANE[0]:SUM_LANE[0] + 1]
    l1 = acc1[:, SUM_LANE[1]:SUM_LANE[1] + 1]
    return jnp.where(_head_mask(lane, 0), acc0 * (1.0 / l0), acc1 * (1.0 / l1))


def _fox_kernel(q_ref, qa_ref, k_ref, ka_ref, v_ref, o_ref, *, tq, rb, pp, heads):
    grp = pl.program_id(1)
    lane = lax.broadcasted_iota(jnp.int32, (1, LANES), 1)
    k0 = PIECES * heads
    causal = lax.broadcasted_iota(jnp.int32, (rb, rb), 0) >= lax.broadcasted_iota(jnp.int32, (rb, rb), 1)

    def attend(n):
        units = [(pi, r, hh) for pi in range(pp) for r in range(tq // rb) for hh in range(2)]

        def spans(r):
            past = n * tq + r * rb
            return ([(0, past)] if past else []) + [(past, past + rb)]

        def logits(pi, r, hh):
            cols = slice(pi * PAIR, (pi + 1) * PAIR)
            rows = slice(r * rb, (r + 1) * rb)
            q = q_ref[0, rows, cols]
            qa = qa_ref[0, rows, :]
            a0 = PIECES * (2 * (grp * pp + pi) + hh)
            amask = ((lane >= a0) & (lane < a0 + PIECES)) | ((lane >= k0 + a0) & (lane < k0 + a0 + PIECES))
            q_aug = jnp.concatenate([jnp.where(_head_mask(lane, hh), q, jnp.zeros_like(q)),
                                     jnp.where(amask, qa, jnp.zeros_like(qa))], axis=-1)
            ss = []
            for lo, hi in spans(r):
                kaug = jnp.concatenate([k_ref[0, lo:hi, cols], ka_ref[0, lo:hi, :]], axis=-1)
                ss.append(_dot_nt(q_aug, kaug))
            ss[-1] = jnp.where(causal, ss[-1], NEG)
            return ss

        def softmax(ss):
            m = functools.reduce(jnp.maximum, [jnp.max(s, axis=-1, keepdims=True) for s in ss])
            return [jnp.exp2(s - m).astype(BF16) for s in ss]

        def weighted_values(pi, r, hh, ps):
            cols = slice(pi * PAIR, (pi + 1) * PAIR)
            acc = None
            for (lo, hi), p in zip(spans(r), ps):
                part = _dot(p, _value_aug(v_ref[0, lo:hi, cols], lane, hh))
                acc = part if acc is None else acc + part
            return acc

        ss, ps, accs = {}, {}, {}
        for t in range(len(units) + 2):
            if t < len(units):
                ss[t] = logits(*units[t])
            if 0 <= t - 1 < len(units):
                ps[t - 1] = softmax(ss.pop(t - 1))
            if 0 <= t - 2 < len(units):
                pi, r, hh = unit = units[t - 2]
                accs[unit] = weighted_values(*unit, ps.pop(t - 2))
                if hh == 1:
                    o_ref[0, r * rb:(r + 1) * rb, pi * PAIR:(pi + 1) * PAIR] = _pair_output(
                        accs.pop((pi, r, 0)), accs.pop((pi, r, 1)), lane).astype(o_ref.dtype)

    for n in range(k_ref.shape[1] // tq):
        pl.when(pl.program_id(2) == n)(functools.partial(attend, n))


def _fox(proj, qa, ka, heads, tq, rb, pp):
    B, S, _ = proj.shape
    groups = heads // 2 // pp
    w = pp * PAIR
    return pl.pallas_call(
        functools.partial(_fox_kernel, tq=tq, rb=rb, pp=pp, heads=heads),
        grid=(B, groups, S // tq),
        in_specs=[
            pl.BlockSpec((1, tq, w), lambda b, g, i: (b, i, g)),
            pl.BlockSpec((1, tq, LANES), lambda b, g, i: (b, i, 0)),
            pl.BlockSpec((1, S, w), lambda b, g, i: (b, 0, groups + g)),
            pl.BlockSpec((1, S, LANES), lambda b, g, i: (b, 0, 0)),
            pl.BlockSpec((1, S, w), lambda b, g, i: (b, 0, 2 * groups + g)),
        ],
        out_specs=pl.BlockSpec((1, tq, w), lambda b, g, i: (b, i, g)),
        out_shape=jax.ShapeDtypeStruct((B, S, heads * HEAD_DIM), BF16),
        compiler_params=pltpu.CompilerParams(
            dimension_semantics=("arbitrary", "arbitrary", "arbitrary"), vmem_limit_bytes=VMEM_LIMIT),
        name="fox_attn",
    )(proj, qa, proj, ka, proj)


def _chunk_kernel(q_ref, k_ref, v_ref, g_ref, o_ref, bias_ref, *, tq, pp):
    i = pl.program_id(2)
    lane = lax.broadcasted_iota(jnp.int32, (1, LANES), 1)

    @pl.when((pl.program_id(1) == 0) & (i == 0))
    def _():
        q_chunk = lax.broadcasted_iota(jnp.int32, (tq, tq), 0) // CHUNK
        k_chunk = lax.broadcasted_iota(jnp.int32, (tq, tq), 1) // CHUNK
        for which in range(3):
            back = which * (tq // CHUNK) + q_chunk - k_chunk
            vis = (back >= 0) & (back <= LEFT_CHUNKS)
            for hd in range(2 * pp):
                g = jnp.broadcast_to(g_ref[0, which, hd], (tq, 2 * tq))
                toep = pltpu.roll(g, 0, 1, stride=1, stride_axis=0)[:, :tq]
                bias_ref[hd, :, (2 - which) * tq:(3 - which) * tq] = jnp.where(vis, toep, NEG)

    def attend(n_blocks):
        span = n_blocks * tq
        ks = pl.multiple_of((i - (n_blocks - 1)) * tq, tq)
        units = [(pi, hh) for pi in range(pp) for hh in range(2)]

        def logits(pi, hh):
            cols = slice(pi * PAIR, (pi + 1) * PAIR)
            q = q_ref[0, :, cols]
            qh = jnp.where(_head_mask(lane, hh), q, jnp.zeros_like(q))
            return _dot_nt(qh, k_ref[0, pl.ds(ks, span), cols]) + bias_ref[2 * pi + hh, :, 3 * tq - span:]

        def softmax(s):
            return jnp.exp2(s - jnp.max(s, axis=-1, keepdims=True)).astype(BF16)

        def weighted_values(pi, hh, p):
            return _dot(p, _value_aug(v_ref[0, pl.ds(ks, span), pi * PAIR:(pi + 1) * PAIR], lane, hh))

        ss, ps, accs = {}, {}, {}
        for t in range(len(units) + 2):
            if t < len(units):
                ss[t] = logits(*units[t])
            if 0 <= t - 1 < len(units):
                ps[t - 1] = softmax(ss.pop(t - 1))
            if 0 <= t - 2 < len(units):
                pi, hh = unit = units[t - 2]
                accs[unit] = weighted_values(*unit, ps.pop(t - 2))
                if hh == 1:
                    o_ref[0, :, pi * PAIR:(pi + 1) * PAIR] = _pair_output(
                        accs.pop((pi, 0)), accs.pop((pi, 1)), lane).astype(o_ref.dtype)

    for n_blocks in (1, 2):
        pl.when(i == n_blocks - 1)(functools.partial(attend, n_blocks))
    pl.when(i >= 2)(functools.partial(attend, 3))


def _chunk_bias_rows(rel_table, tq, pp):
    heads = rel_table.shape[0]
    m = np.arange(2 * tq)
    offset = np.where(m < tq, -m, 2 * tq - m)
    idx = np.stack([np.clip(which * tq + offset, -MAX_REL, MAX_REL) + MAX_REL for which in range(3)])
    rows = rel_table.astype(F32)[:, idx] * LOG2E
    return rows.reshape(heads // (2 * pp), 2 * pp, 3, 1, 2 * tq).transpose(0, 2, 1, 3, 4)


def _chunk(proj, bias_rows, tq, pp, col0):
    B, S, _ = proj.shape
    groups = bias_rows.shape[0]
    w = pp * PAIR
    c0 = col0 // pp
    assert tq % CHUNK == 0 and LEFT_CHUNKS * CHUNK <= 2 * tq, "band must fit in three key blocks"
    return pl.pallas_call(
        functools.partial(_chunk_kernel, tq=tq, pp=pp),
        grid=(groups, B, S // tq),
        in_specs=[
            pl.BlockSpec((1, tq, w), lambda g, b, i: (b, i, c0 + g)),
            pl.BlockSpec((1, S, w), lambda g, b, i: (b, 0, c0 + groups + g)),
            pl.BlockSpec((1, S, w), lambda g, b, i: (b, 0, c0 + 2 * groups + g)),
            pl.BlockSpec((1, 3, 2 * pp, 1, 2 * tq), lambda g, b, i: (g, 0, 0, 0, 0)),
        ],
        out_specs=pl.BlockSpec((1, tq, w), lambda g, b, i: (b, i, g)),
        out_shape=jax.ShapeDtypeStruct((B, S, groups * w), BF16),
        scratch_shapes=[pltpu.VMEM((2 * pp, tq, 3 * tq), F32)],
        compiler_params=pltpu.CompilerParams(
            dimension_semantics=("arbitrary", "arbitrary", "arbitrary"), vmem_limit_bytes=VMEM_LIMIT),
        name="chunk_attn",
    )(proj, proj, proj, bias_rows)


def _memkv_kernel(mem_ref, g_ref, wk_ref, wv_ref, k_ref, v_ref):
    mb = _rms(mem_ref[0], g_ref[...]).astype(BF16)
    k_ref[0] = _dot(mb, wk_ref[...]).astype(BF16)
    v_ref[0] = _dot(mb, wv_ref[...]).astype(BF16)


def _memkv(mem, g, wk, wv):
    B, M, D = mem.shape
    full = lambda b: (0, 0)
    return pl.pallas_call(
        _memkv_kernel,
        grid=(B,),
        in_specs=[
            pl.BlockSpec((1, M, D), lambda b: (b, 0, 0)),
            pl.BlockSpec((1, D), full),
            pl.BlockSpec((D, D), full),
            pl.BlockSpec((D, D), full),
        ],
        out_specs=[pl.BlockSpec((1, M, D), lambda b: (b, 0, 0))] * 2,
        out_shape=[jax.ShapeDtypeStruct((B, M, D), BF16)] * 2,
        compiler_params=pltpu.CompilerParams(
            dimension_semantics=("arbitrary",), vmem_limit_bytes=VMEM_LIMIT),
        name="mem_kv",
    )(mem, g, wk, wv)


def _mix_mem_kernel(yf_ref, yc_ref, x_ref, gf_ref, gc_ref, wo_ref, gpost_ref,
                    gpre_ref, wq_ref, mk_ref, mv_ref, wmo_ref, gmpost_ref, o_ref):
    d_fox = yf_ref.shape[2]
    yf = _rms(yf_ref[0].astype(F32), gf_ref[...]).astype(BF16)
    yc = _rms(yc_ref[0].astype(F32), gc_ref[...]).astype(BF16)
    y = _dot(yf, wo_ref[0:d_fox, :]) + _dot(yc, wo_ref[d_fox:, :])
    x1 = x_ref[0] + _rms(y, gpost_ref[...])

    h = _rms(x1, gpre_ref[...]).astype(BF16)
    q = _dot(h, wq_ref[...]).astype(BF16)
    dh = q.shape[1] // MEM_HEADS
    outs = []
    for hh in range(MEM_HEADS):
        sl = slice(hh * dh, (hh + 1) * dh)
        s = _dot_nt(q[:, sl], mk_ref[0, :, sl])
        pr = jnp.exp2(s - jnp.max(s, axis=-1, keepdims=True))
        inv = 1.0 / jnp.sum(pr, axis=-1, keepdims=True)
        outs.append((_dot(pr.astype(BF16), mv_ref[0, :, sl]) * inv).astype(BF16))
    o = jnp.concatenate(outs, axis=-1)
    y2 = _dot(o, wmo_ref[...])
    o_ref[0] = x1 + _rms(y2, gmpost_ref[...])


def _mix_mem(yf, yc, x, gf, gc, wo, gpost, gpre, wq, mk, mv, wmo, gmpost, tm):
    B, S, D = x.shape
    M = mk.shape[1]
    dg = yf.shape[2]
    vec = lambda n: pl.BlockSpec((1, n), lambda b, i: (0, 0))
    mat = lambda r, c: pl.BlockSpec((r, c), lambda b, i: (0, 0))
    return pl.pallas_call(
        _mix_mem_kernel,
        grid=(B, S // tm),
        in_specs=[
            pl.BlockSpec((1, tm, dg), lambda b, i: (b, i, 0)),
            pl.BlockSpec((1, tm, dg), lambda b, i: (b, i, 0)),
            pl.BlockSpec((1, tm, D), lambda b, i: (b, i, 0)),
            vec(dg), vec(dg), mat(2 * dg, D), vec(D),
            vec(D), mat(D, D),
            pl.BlockSpec((1, M, D), lambda b, i: (b, 0, 0)),
            pl.BlockSpec((1, M, D), lambda b, i: (b, 0, 0)),
            mat(D, D), vec(D),
        ],
        out_specs=pl.BlockSpec((1, tm, D), lambda b, i: (b, i, 0)),
        out_shape=jax.ShapeDtypeStruct((B, S, D), F32),
        compiler_params=pltpu.CompilerParams(
            dimension_semantics=("arbitrary", "arbitrary"), vmem_limit_bytes=VMEM_LIMIT),
        name="mix_mem",
    )(yf, yc, x, gf, gc, wo, gpost, gpre, wq, mk, mv, wmo, gmpost)


def _mlp_kernel(x_ref, gpre_ref, w1_ref, w2_ref, gpost_ref, o_ref, *, ff_block):
    x = x_ref[...]
    h = _rms(x, gpre_ref[...]).astype(BF16)
    d_ff = w1_ref.shape[1]
    y = None
    for n in range(0, d_ff, ff_block):
        a = jnp.maximum(_dot(h, w1_ref[:, n:n + ff_block]), 0.0)
        part = _dot((a * a).astype(BF16), w2_ref[n:n + ff_block, :])
        y = part if y is None else y + part
    o_ref[...] = x + _rms(y, gpost_ref[...])


def _mlp(x, gpre, w1, w2, gpost, tm, ff_block):
    R, D = x.shape
    d_ff = w1.shape[1]
    return pl.pallas_call(
        functools.partial(_mlp_kernel, ff_block=ff_block),
        grid=(R // tm,),
        in_specs=[
            pl.BlockSpec((tm, D), lambda i: (i, 0)),
            pl.BlockSpec((1, D), lambda i: (0, 0)),
            pl.BlockSpec((D, d_ff), lambda i: (0, 0)),
            pl.BlockSpec((d_ff, D), lambda i: (0, 0)),
            pl.BlockSpec((1, D), lambda i: (0, 0)),
        ],
        out_specs=pl.BlockSpec((tm, D), lambda i: (i, 0)),
        out_shape=jax.ShapeDtypeStruct((R, D), F32),
        compiler_params=pltpu.CompilerParams(
            dimension_semantics=("arbitrary",), vmem_limit_bytes=VMEM_LIMIT),
        name="mlp",
    )(x, gpre, w1, w2, gpost)


def kernel(x, mem, w_in, b_fgt, rel_bias, g_fox_out, g_chk_out, w_out, g_mix_pre, g_mix_post,
           g_mem_kv, w_mq, w_mk, w_mv, w_mo, g_mem_pre, g_mem_post,
           w_ff1, w_ff2, g_ff_pre, g_ff_post):
    B, S, D = x.shape
    depth = w_in.shape[0]
    heads = b_fgt.shape[1]
    d_fox = heads * HEAD_DIM
    d_chk = rel_bias.shape[1] * HEAD_DIM
    assert d_fox == d_chk and heads % 2 == 0 and heads <= LANES
    pairs = heads // 2
    row = lambda v: v.reshape(1, -1)

    for l in range(depth):
        scale = HEAD_DIM ** -0.5 * LOG2E
        wl = w_in[l]
        o_f = 3 * d_fox
        w_main = jnp.concatenate([
            wl[:, :d_fox] * scale, wl[:, d_fox:o_f],
            wl[:, o_f + heads:o_f + heads + d_chk] * scale, wl[:, o_f + heads + d_chk:],
        ], axis=1).astype(BF16)
        w_f = jnp.pad(wl[:, o_f:o_f + heads], ((0, 0), (0, LANES - heads))).astype(BF16)
        b_f = jnp.pad(b_fgt[l], (0, LANES - heads)).reshape(1, LANES)

        proj, qa, ka = _inproj(x, row(g_mix_pre[l]), w_main, w_f, b_f, heads, tm=512)
        y_fox = _fox(proj, qa, ka, heads, tq=512, rb=256, pp=2)
        y_chk = _chunk(proj, _chunk_bias_rows(rel_bias[l], 256, 4), tq=256, pp=4, col0=3 * pairs)

        mk, mv = _memkv(mem, row(g_mem_kv[l]), w_mk[l].astype(BF16), w_mv[l].astype(BF16))
        mem_scale = (D // MEM_HEADS) ** -0.5 * LOG2E
        x = _mix_mem(y_fox, y_chk, x, row(g_fox_out[l]), row(g_chk_out[l]), w_out[l].astype(BF16),
                     row(g_mix_post[l]), row(g_mem_pre[l]), (w_mq[l] * mem_scale).astype(BF16),
                     mk, mv, w_mo[l].astype(BF16), row(g_mem_post[l]), tm=256)
        x = _mlp(x.reshape(B * S, D), row(g_ff_pre[l]), w_ff1[l].astype(BF16), w_ff2[l].astype(BF16),
                 row(g_ff_post[l]), tm=256, ff_block=1024).reshape(B, S, D)
    return x
```

```python
import functools
import math

import numpy as np
import jax
import jax.numpy as jnp
from jax import lax
from jax.experimental import pallas as pl
from jax.experimental.pallas import tpu as pltpu

F32 = jnp.float32
BF16 = jnp.bfloat16

EPS = 1e-6
HEAD_DIM = 64
PAIR = 2 * HEAD_DIM
CHUNK = 64
LEFT_CHUNKS = 8
MAX_REL = 128
MEM_HEADS = 4
NEG = -1e30
LOG2E = math.log2(math.e)
LANES = 128
VMEM_LIMIT = 56 * 1024 * 1024
PIECES = 3
SUM_LANE = (HEAD_DIM, 0)


def _rms(x, g):
    return x * lax.rsqrt(jnp.mean(x * x, axis=-1, keepdims=True) + EPS) * g


def _dot(a, b):
    return jnp.dot(a, b, preferred_element_type=F32)


def _dot_nt(a, b):
    return lax.dot_general(a, b, (((1,), (1,)), ((), ())), preferred_element_type=F32)


def _split3(x):
    hi = x.astype(BF16)
    r1 = x - hi.astype(F32)
    mid = r1.astype(BF16)
    lo = (r1 - mid.astype(F32)).astype(BF16)
    return hi, mid, lo


def _inproj_kernel(x_ref, g_ref, w_ref, bf_ref, tri_ref, e_ref, ones_ref,
                   proj_ref, qa_ref, ka_ref, carry_ref, *, heads, sub):
    @pl.when(pl.program_id(1) == 0)
    def _():
        carry_ref[...] = jnp.zeros_like(carry_ref)

    n_main = proj_ref.shape[2]
    halves = range(x_ref.shape[1] // sub)
    rows = [slice(h * sub, (h + 1) * sub) for h in halves]
    lane = lax.broadcasted_iota(jnp.int32, (1, LANES), 1)
    hbs = [_rms(x_ref[0, rows[h], :], g_ref[...]).astype(BF16) for h in halves]
    fs = []
    for h in halves:
        res = _dot(hbs[h], w_ref[...])
        proj_ref[0, rows[h], :] = res[:, :n_main].astype(BF16)
        fs.append(res[:, n_main:] + bf_ref[...])
    carry = carry_ref[...]
    for h in halves:
        f = fs[h]
        logf = jnp.minimum(f, 0.0) - jnp.log1p(jnp.exp(-jnp.abs(f)))
        hi, mid, lo = _split3(logf)
        packed = jnp.where(lane < heads, hi, jnp.where(lane < 2 * heads, mid, lo))
        cum = _dot(tri_ref[...], packed)
        c = (cum + pltpu.roll(cum, LANES - heads, 1) + pltpu.roll(cum, LANES - 2 * heads, 1)) + carry
        carry = c[sub - 1:sub, :]
        pieces = jnp.concatenate(_split3(c * LOG2E), axis=-1)
        sel = _dot(pieces, e_ref[...])
        qa_ref[0, rows[h], :] = (sel[:, :LANES] + ones_ref[0:1, :]).astype(BF16)
        ka_ref[0, rows[h], :] = (ones_ref[1:2, :] - sel[:, LANES:]).astype(BF16)
    carry_ref[...] = carry


def _aug_constants(heads):
    k0 = PIECES * heads
    assert 2 * k0 <= LANES
    e = np.zeros((PIECES * LANES, 2 * LANES), np.float32)
    for h in range(heads):
        for r in range(PIECES):
            e[r * LANES + h, PIECES * h + r] = 1.0
            e[r * LANES + h, LANES + k0 + PIECES * h + r] = 1.0
    ones = np.zeros((2, LANES), np.float32)
    ones[0, k0:2 * k0] = 1.0
    ones[1, 0:k0] = 1.0
    return jnp.asarray(e, BF16), jnp.asarray(ones)


def _inproj(x, g, w_all, b_f, heads, n_main, tm, sub):
    B, S, D = x.shape
    tri = jnp.asarray(np.tril(np.ones((sub, sub), np.float32)), BF16)
    e, ones = _aug_constants(heads)
    const = lambda r, c: pl.BlockSpec((r, c), lambda b, i: (0, 0))
    return pl.pallas_call(
        functools.partial(_inproj_kernel, heads=heads, sub=sub),
        grid=(B, S // tm),
        in_specs=[
            pl.BlockSpec((1, tm, D), lambda b, i: (b, i, 0)),
            const(1, D), const(D, n_main + LANES), const(1, LANES), const(sub, sub),
            const(PIECES * LANES, 2 * LANES), const(2, LANES),
        ],
        out_specs=[
            pl.BlockSpec((1, tm, n_main), lambda b, i: (b, i, 0)),
            pl.BlockSpec((1, tm, LANES), lambda b, i: (b, i, 0)),
            pl.BlockSpec((1, tm, LANES), lambda b, i: (b, i, 0)),
        ],
        out_shape=[
            jax.ShapeDtypeStruct((B, S, n_main), BF16),
            jax.ShapeDtypeStruct((B, S, LANES), BF16),
            jax.ShapeDtypeStruct((B, S, LANES), BF16),
        ],
        scratch_shapes=[pltpu.VMEM((1, LANES), F32)],
        compiler_params=pltpu.CompilerParams(
            dimension_semantics=("arbitrary", "arbitrary"), vmem_limit_bytes=VMEM_LIMIT),
        name="inproj",
    )(x, g, w_all, b_f, tri, e, ones)


def _head_mask(lane, hh):
    return lane < HEAD_DIM if hh == 0 else lane >= HEAD_DIM


def _value_aug(vv, lane, hh):
    ones = jnp.where(lane == SUM_LANE[hh], 1.0, 0.0).astype(vv.dtype)
    return jnp.where(_head_mask(lane, hh), vv, ones)


def _pair_output(acc0, acc1, lane):
    l0 = acc0[:, SUM_LANE[0]:SUM_LANE[0] + 1]
    l1 = acc1[:, SUM_LANE[1]:SUM_LANE[1] + 1]
    return jnp.where(_head_mask(lane, 0), acc0 * (1.0 / l0), acc1 * (1.0 / l1))


def _fox_kernel(q_ref, qa_ref, k_ref, ka_ref, v_ref, o_ref, *, tq, rb, pp, heads):
    grp = pl.program_id(1)
    lane = lax.broadcasted_iota(jnp.int32, (1, LANES), 1)
    k0 = PIECES * heads
    causal = lax.broadcasted_iota(jnp.int32, (rb, rb), 0) >= lax.broadcasted_iota(jnp.int32, (rb, rb), 1)

    def attend(n):
        units = [(pi, r, hh) for pi in range(pp) for r in range(tq // rb) for hh in range(2)]

        def spans(r):
            past = n * tq + r * rb
            return ([(0, past)] if past else []) + [(past, past + rb)]

        def logits(pi, r, hh):
            cols = slice(pi * PAIR, (pi + 1) * PAIR)
            rows = slice(r * rb, (r + 1) * rb)
            q = q_ref[0, rows, cols]
            qa = qa_ref[0, rows, :]
            a0 = PIECES * (2 * (grp * pp + pi) + hh)
            amask = ((lane >= a0) & (lane < a0 + PIECES)) | ((lane >= k0 + a0) & (lane < k0 + a0 + PIECES))
            q_aug = jnp.concatenate([jnp.where(_head_mask(lane, hh), q, jnp.zeros_like(q)),
                                     jnp.where(amask, qa, jnp.zeros_like(qa))], axis=-1)
            ss = []
            for lo, hi in spans(r):
                kaug = jnp.concatenate([k_ref[0, lo:hi, cols], ka_ref[0, lo:hi, :]], axis=-1)
                ss.append(_dot_nt(q_aug, kaug))
            ss[-1] = jnp.where(causal, ss[-1], NEG)
            return ss

        def softmax(ss):
            m = functools.reduce(jnp.maximum, [jnp.max(s, axis=-1, keepdims=True) for s in ss])
            return [jnp.exp2(s - m).astype(BF16) for s in ss]

        def weighted_values(pi, r, hh, ps):
            cols = slice(pi * PAIR, (pi + 1) * PAIR)
            acc = None
            for (lo, hi), p in zip(spans(r), ps):
                part = _dot(p, _value_aug(v_ref[0, lo:hi, cols], lane, hh))
                acc = part if acc is None else acc + part
            return acc

        ss, ps, accs = {}, {}, {}
        for t in range(len(units) + 2):
            if t < len(units):
                ss[t] = logits(*units[t])
            if 0 <= t - 1 < len(units):
                ps[t - 1] = softmax(ss.pop(t - 1))
            if 0 <= t - 2 < len(units):
                pi, r, hh = unit = units[t - 2]
                accs[unit] = weighted_values(*unit, ps.pop(t - 2))
                if hh == 1:
                    o_ref[0, r * rb:(r + 1) * rb, pi * PAIR:(pi + 1) * PAIR] = _pair_output(
                        accs.pop((pi, r, 0)), accs.pop((pi, r, 1)), lane).astype(o_ref.dtype)

    for n in range(k_ref.shape[1] // tq):
        pl.when(pl.program_id(2) == n)(functools.partial(attend, n))


def _fox(proj, qa, ka, heads, tq, rb, pp):
    B, S, _ = proj.shape
    groups = heads // 2 // pp
    w = pp * PAIR
    return pl.pallas_call(
        functools.partial(_fox_kernel, tq=tq, rb=rb, pp=pp, heads=heads),
        grid=(B, groups, S // tq),
        in_specs=[
            pl.BlockSpec((1, tq, w), lambda b, g, i: (b, i, g)),
            pl.BlockSpec((1, tq, LANES), lambda b, g, i: (b, i, 0)),
            pl.BlockSpec((1, S, w), lambda b, g, i: (b, 0, groups + g)),
            pl.BlockSpec((1, S, LANES), lambda b, g, i: (b, 0, 0)),
            pl.BlockSpec((1, S, w), lambda b, g, i: (b, 0, 2 * groups + g)),
        ],
        out_specs=pl.BlockSpec((1, tq, w), lambda b, g, i: (b, i, g)),
        out_shape=jax.ShapeDtypeStruct((B, S, heads * HEAD_DIM), BF16),
        compiler_params=pltpu.CompilerParams(
            dimension_semantics=("arbitrary", "arbitrary", "arbitrary"), vmem_limit_bytes=VMEM_LIMIT),
        name="fox_attn",
    )(proj, qa, proj, ka, proj)


def _chunk_kernel(q_ref, k_ref, v_ref, g_ref, o_ref, bias_ref, *, tq, pp):
    i = pl.program_id(2)
    lane = lax.broadcasted_iota(jnp.int32, (1, LANES), 1)

    @pl.when((pl.program_id(1) == 0) & (i == 0))
    def _():
        q_chunk = lax.broadcasted_iota(jnp.int32, (tq, tq), 0) // CHUNK
        k_chunk = lax.broadcasted_iota(jnp.int32, (tq, tq), 1) // CHUNK
        for which in range(3):
            back = which * (tq // CHUNK) + q_chunk - k_chunk
            vis = (back >= 0) & (back <= LEFT_CHUNKS)
            for hd in range(2 * pp):
                g = jnp.broadcast_to(g_ref[0, which, hd], (tq, 2 * tq))
                toep = pltpu.roll(g, 0, 1, stride=1, stride_axis=0)[:, :tq]
                bias_ref[hd, :, (2 - which) * tq:(3 - which) * tq] = jnp.where(vis, toep, NEG)

    def attend(n_blocks):
        span = n_blocks * tq
        ks = pl.multiple_of((i - (n_blocks - 1)) * tq, tq)
        units = [(pi, hh) for pi in range(pp) for hh in range(2)]

        def logits(pi, hh):
            cols = slice(pi * PAIR, (pi + 1) * PAIR)
            q = q_ref[0, :, cols]
            qh = jnp.where(_head_mask(lane, hh), q, jnp.zeros_like(q))
            return _dot_nt(qh, k_ref[0, pl.ds(ks, span), cols]) + bias_ref[2 * pi + hh, :, 3 * tq - span:]

        def softmax(s):
            return jnp.exp2(s - jnp.max(s, axis=-1, keepdims=True)).astype(BF16)

        def weighted_values(pi, hh, p):
            return _dot(p, _value_aug(v_ref[0, pl.ds(ks, span), pi * PAIR:(pi + 1) * PAIR], lane, hh))

        ss, ps, accs = {}, {}, {}
        for t in range(len(units) + 2):
            if t < len(units):
                ss[t] = logits(*units[t])
            if 0 <= t - 1 < len(units):
                ps[t - 1] = softmax(ss.pop(t - 1))
            if 0 <= t - 2 < len(units):
                pi, hh = unit = units[t - 2]
                accs[unit] = weighted_values(*unit, ps.pop(t - 2))
                if hh == 1:
                    o_ref[0, :, pi * PAIR:(pi + 1) * PAIR] = _pair_output(
                        accs.pop((pi, 0)), accs.pop((pi, 1)), lane).astype(o_ref.dtype)

    for n_blocks in (1, 2):
        pl.when(i == n_blocks - 1)(functools.partial(attend, n_blocks))
    pl.when(i >= 2)(functools.partial(attend, 3))


def _chunk_bias_rows(rel_table, tq, pp):
    heads = rel_table.shape[0]
    m = np.arange(2 * tq)
    offset = np.where(m < tq, -m, 2 * tq - m)
    idx = np.stack([np.clip(which * tq + offset, -MAX_REL, MAX_REL) + MAX_REL for which in range(3)])
    rows = rel_table.astype(F32)[:, idx] * LOG2E
    return rows.reshape(heads // (2 * pp), 2 * pp, 3, 1, 2 * tq).transpose(0, 2, 1, 3, 4)


def _chunk(proj, bias_rows, tq, pp, col0):
    B, S, _ = proj.shape
    groups = bias_rows.shape[0]
    w = pp * PAIR
    c0 = col0 // pp
    assert tq % CHUNK == 0 and LEFT_CHUNKS * CHUNK <= 2 * tq, "band must fit in three key blocks"
    return pl.pallas_call(
        functools.partial(_chunk_kernel, tq=tq, pp=pp),
        grid=(groups, B, S // tq),
        in_specs=[
            pl.BlockSpec((1, tq, w), lambda g, b, i: (b, i, c0 + g)),
            pl.BlockSpec((1, S, w), lambda g, b, i: (b, 0, c0 + groups + g)),
            pl.BlockSpec((1, S, w), lambda g, b, i: (b, 0, c0 + 2 * groups + g)),
            pl.BlockSpec((1, 3, 2 * pp, 1, 2 * tq), lambda g, b, i: (g, 0, 0, 0, 0)),
        ],
        out_specs=pl.BlockSpec((1, tq, w), lambda g, b, i: (b, i, g)),
        out_shape=jax.ShapeDtypeStruct((B, S, groups * w), BF16),
        scratch_shapes=[pltpu.VMEM((2 * pp, tq, 3 * tq), F32)],
        compiler_params=pltpu.CompilerParams(
            dimension_semantics=("arbitrary", "arbitrary", "arbitrary"), vmem_limit_bytes=VMEM_LIMIT),
        name="chunk_attn",
    )(proj, proj, proj, bias_rows)


def _memkv_kernel(mem_ref, g_ref, wk_ref, wv_ref, k_ref, v_ref):
    mb = _rms(mem_ref[0], g_ref[...]).astype(BF16)
    k_ref[0] = _dot(mb, wk_ref[...]).astype(BF16)
    v_ref[0] = _dot(mb, wv_ref[...]).astype(BF16)


def _memkv(mem, g, wk, wv):
    B, M, D = mem.shape
    full = lambda b: (0, 0)
    return pl.pallas_call(
        _memkv_kernel,
        grid=(B,),
        in_specs=[
            pl.BlockSpec((1, M, D), lambda b: (b, 0, 0)),
            pl.BlockSpec((1, D), full),
            pl.BlockSpec((D, D), full),
            pl.BlockSpec((D, D), full),
        ],
        out_specs=[pl.BlockSpec((1, M, D), lambda b: (b, 0, 0))] * 2,
        out_shape=[jax.ShapeDtypeStruct((B, M, D), BF16)] * 2,
        compiler_params=pltpu.CompilerParams(
            dimension_semantics=("arbitrary",), vmem_limit_bytes=VMEM_LIMIT),
        name="mem_kv",
    )(mem, g, wk, wv)


def _mix_mem_kernel(yf_ref, yc_ref, x_ref, gf_ref, gc_ref, wo_ref, gpost_ref,
                    gpre_ref, wq_ref, mk_ref, mv_ref, wmo_ref, gmpost_ref, o_ref, *, sub):
    d_fox = yf_ref.shape[2]
    halves = range(x_ref.shape[1] // sub)
    rows = [slice(h * sub, (h + 1) * sub) for h in halves]

    def mix(r):
        yf = _rms(yf_ref[0, r, :].astype(F32), gf_ref[...]).astype(BF16)
        yc = _rms(yc_ref[0, r, :].astype(F32), gc_ref[...]).astype(BF16)
        return _dot(yf, wo_ref[0:d_fox, :]) + _dot(yc, wo_ref[d_fox:, :])

    def query(x1):
        return _dot(_rms(x1, gpre_ref[...]).astype(BF16), wq_ref[...]).astype(BF16)

    def attend(q):
        dh = q.shape[1] // MEM_HEADS
        outs = []
        for hh in range(MEM_HEADS):
            sl = slice(hh * dh, (hh + 1) * dh)
            s = _dot_nt(q[:, sl], mk_ref[0, :, sl])
            pr = jnp.exp2(s - jnp.max(s, axis=-1, keepdims=True))
            inv = 1.0 / jnp.sum(pr, axis=-1, keepdims=True)
            outs.append((_dot(pr.astype(BF16), mv_ref[0, :, sl]) * inv).astype(BF16))
        return jnp.concatenate(outs, axis=-1)

    ys = [mix(rows[h]) for h in halves]
    x1s = [x_ref[0, rows[h], :] + _rms(ys[h], gpost_ref[...]) for h in halves]
    qs = [query(x1s[h]) for h in halves]
    os = [attend(qs[h]) for h in halves]
    y2s = [_dot(os[h], wmo_ref[...]) for h in halves]
    for h in halves:
        o_ref[0, rows[h], :] = x1s[h] + _rms(y2s[h], gmpost_ref[...])


def _mix_mem(yf, yc, x, gf, gc, wo, gpost, gpre, wq, mk, mv, wmo, gmpost, tm, sub):
    B, S, D = x.shape
    M = mk.shape[1]
    dg = yf.shape[2]
    vec = lambda n: pl.BlockSpec((1, n), lambda b, i: (0, 0))
    mat = lambda r, c: pl.BlockSpec((r, c), lambda b, i: (0, 0))
    return pl.pallas_call(
        functools.partial(_mix_mem_kernel, sub=sub),
        grid=(B, S // tm),
        in_specs=[
            pl.BlockSpec((1, tm, dg), lambda b, i: (b, i, 0)),
            pl.BlockSpec((1, tm, dg), lambda b, i: (b, i, 0)),
            pl.BlockSpec((1, tm, D), lambda b, i: (b, i, 0)),
            vec(dg), vec(dg), mat(2 * dg, D), vec(D),
            vec(D), mat(D, D),
            pl.BlockSpec((1, M, D), lambda b, i: (b, 0, 0)),
            pl.BlockSpec((1, M, D), lambda b, i: (b, 0, 0)),
            mat(D, D), vec(D),
        ],
        out_specs=pl.BlockSpec((1, tm, D), lambda b, i: (b, i, 0)),
        out_shape=jax.ShapeDtypeStruct((B, S, D), F32),
        compiler_params=pltpu.CompilerParams(
            dimension_semantics=("arbitrary", "arbitrary"), vmem_limit_bytes=VMEM_LIMIT),
        name="mix_mem",
    )(yf, yc, x, gf, gc, wo, gpost, gpre, wq, mk, mv, wmo, gmpost)


def _mlp_kernel(x_ref, gpre_ref, w1_ref, w2_ref, gpost_ref, o_ref, *, ff_block, sub):
    halves = range(x_ref.shape[0] // sub)
    rows = [slice(h * sub, (h + 1) * sub) for h in halves]
    hs = [_rms(x_ref[rows[h], :], gpre_ref[...]).astype(BF16) for h in halves]
    ys = [None for _ in halves]
    for n in range(0, w1_ref.shape[1], ff_block):
        for h in halves:
            a = jnp.maximum(_dot(hs[h], w1_ref[:, n:n + ff_block]), 0.0)
            part = _dot((a * a).astype(BF16), w2_ref[n:n + ff_block, :])
            ys[h] = part if ys[h] is None else ys[h] + part
    for h in halves:
        o_ref[rows[h], :] = x_ref[rows[h], :] + _rms(ys[h], gpost_ref[...])


def _mlp(x, gpre, w1, w2, gpost, tm, ff_block, sub):
    R, D = x.shape
    d_ff = w1.shape[1]
    return pl.pallas_call(
        functools.partial(_mlp_kernel, ff_block=ff_block, sub=sub),
        grid=(R // tm,),
        in_specs=[
            pl.BlockSpec((tm, D), lambda i: (i, 0)),
            pl.BlockSpec((1, D), lambda i: (0, 0)),
            pl.BlockSpec((D, d_ff), lambda i: (0, 0)),
            pl.BlockSpec((d_ff, D), lambda i: (0, 0)),
            pl.BlockSpec((1, D), lambda i: (0, 0)),
        ],
        out_specs=pl.BlockSpec((tm, D), lambda i: (i, 0)),
        out_shape=jax.ShapeDtypeStruct((R, D), F32),
        compiler_params=pltpu.CompilerParams(
            dimension_semantics=("arbitrary",), vmem_limit_bytes=VMEM_LIMIT),
        name="mlp",
    )(x, gpre, w1, w2, gpost)


def kernel(x, mem, w_in, b_fgt, rel_bias, g_fox_out, g_chk_out, w_out, g_mix_pre, g_mix_post,
           g_mem_kv, w_mq, w_mk, w_mv, w_mo, g_mem_pre, g_mem_post,
           w_ff1, w_ff2, g_ff_pre, g_ff_post):
    B, S, D = x.shape
    depth = w_in.shape[0]
    heads = b_fgt.shape[1]
    d_fox = heads * HEAD_DIM
    d_chk = rel_bias.shape[1] * HEAD_DIM
    assert d_fox == d_chk and heads % 2 == 0 and heads <= LANES
    pairs = heads // 2
    row = lambda v: v.reshape(1, -1)

    for l in range(depth):
        scale = HEAD_DIM ** -0.5 * LOG2E
        wl = w_in[l]
        o_f = 3 * d_fox
        w_f = wl[:, o_f:o_f + heads]
        w_all = jnp.concatenate([
            wl[:, :d_fox] * scale, wl[:, d_fox:o_f],
            wl[:, o_f + heads:o_f + heads + d_chk] * scale, wl[:, o_f + heads + d_chk:],
            jnp.pad(jnp.tile(w_f, (1, PIECES)), ((0, 0), (0, LANES - PIECES * heads))),
        ], axis=1).astype(BF16)
        b_f = jnp.pad(jnp.tile(b_fgt[l], PIECES), (0, LANES - PIECES * heads)).reshape(1, LANES)

        proj, qa, ka = _inproj(x, row(g_mix_pre[l]), w_all, b_f, heads, n_main=3 * (d_fox + d_chk),
                               tm=512, sub=256)
        y_fox = _fox(proj, qa, ka, heads, tq=512, rb=256, pp=2)
        y_chk = _chunk(proj, _chunk_bias_rows(rel_bias[l], 256, 4), tq=256, pp=4, col0=3 * pairs)

        mk, mv = _memkv(mem, row(g_mem_kv[l]), w_mk[l].astype(BF16), w_mv[l].astype(BF16))
        mem_scale = (D // MEM_HEADS) ** -0.5 * LOG2E
        x = _mix_mem(y_fox, y_chk, x, row(g_fox_out[l]), row(g_chk_out[l]), w_out[l].astype(BF16),
                     row(g_mix_post[l]), row(g_mem_pre[l]), (w_mq[l] * mem_scale).astype(BF16),
                     mk, mv, w_mo[l].astype(BF16), row(g_mem_post[l]), tm=512, sub=256)
        x = _mlp(x.reshape(B * S, D), row(g_ff_pre[l]), w_ff1[l].astype(BF16), w_ff2[l].astype(BF16),
                 row(g_ff_post[l]), tm=512, ff_block=1024, sub=256).reshape(B, S, D)
    return x
```

```python
import functools
import math

import numpy as np
import jax
import jax.numpy as jnp
from jax import lax
from jax.experimental import pallas as pl
from jax.experimental.pallas import tpu as pltpu

F32 = jnp.float32
BF16 = jnp.bfloat16

EPS = 1e-6
HEAD_DIM = 64
PAIR = 2 * HEAD_DIM
CHUNK = 64
LEFT_CHUNKS = 8
MAX_REL = 128
MEM_HEADS = 4
NEG = -1e30
LOG2E = math.log2(math.e)
LANES = 128
VMEM_LIMIT = 56 * 1024 * 1024
PIECES = 3
SUM_LANE = (HEAD_DIM, 0)


def _rms(x, g):
    return x * lax.rsqrt(jnp.mean(x * x, axis=-1, keepdims=True) + EPS) * g


def _dot(a, b):
    return jnp.dot(a, b, preferred_element_type=F32)


def _dot_nt(a, b):
    return lax.dot_general(a, b, (((1,), (1,)), ((), ())), preferred_element_type=F32)


def _split3(x):
    hi = x.astype(BF16)
    r1 = x - hi.astype(F32)
    mid = r1.astype(BF16)
    lo = (r1 - mid.astype(F32)).astype(BF16)
    return hi, mid, lo


def _inproj_kernel(x_ref, g_ref, w_ref, bf_ref, tri_ref, e_ref, ones_ref,
                   proj_ref, qa_ref, ka_ref, carry_ref, *, heads, sub):
    @pl.when(pl.program_id(1) == 0)
    def _():
        carry_ref[...] = jnp.zeros_like(carry_ref)

    n_main = proj_ref.shape[2]
    halves = range(x_ref.shape[1] // sub)
    rows = [slice(h * sub, (h + 1) * sub) for h in halves]
    lane = lax.broadcasted_iota(jnp.int32, (1, LANES), 1)
    hbs = [_rms(x_ref[0, rows[h], :], g_ref[...]).astype(BF16) for h in halves]
    fs = []
    for h in halves:
        res = _dot(hbs[h], w_ref[...])
        proj_ref[0, rows[h], :] = res[:, :n_main].astype(BF16)
        fs.append(res[:, n_main:] + bf_ref[...])
    carry = carry_ref[...]
    for h in halves:
        f = fs[h]
        logf = jnp.minimum(f, 0.0) - jnp.log1p(jnp.exp(-jnp.abs(f)))
        hi, mid, lo = _split3(logf)
        packed = jnp.where(lane < heads, hi, jnp.where(lane < 2 * heads, mid, lo))
        cum = _dot(tri_ref[...], packed)
        c = (cum + pltpu.roll(cum, LANES - heads, 1) + pltpu.roll(cum, LANES - 2 * heads, 1)) + carry
        carry = c[sub - 1:sub, :]
        pieces = jnp.concatenate(_split3(c * LOG2E), axis=-1)
        sel = _dot(pieces, e_ref[...])
        qa_ref[0, rows[h], :] = (sel[:, :LANES] + ones_ref[0:1, :]).astype(BF16)
        ka_ref[0, rows[h], :] = (ones_ref[1:2, :] - sel[:, LANES:]).astype(BF16)
    carry_ref[...] = carry


def _aug_constants(heads):
    k0 = PIECES * heads
    assert 2 * k0 <= LANES
    e = np.zeros((PIECES * LANES, 2 * LANES), np.float32)
    for h in range(heads):
        for r in range(PIECES):
            e[r * LANES + h, PIECES * h + r] = 1.0
            e[r * LANES + h, LANES + k0 + PIECES * h + r] = 1.0
    ones = np.zeros((2, LANES), np.float32)
    ones[0, k0:2 * k0] = 1.0
    ones[1, 0:k0] = 1.0
    return jnp.asarray(e, BF16), jnp.asarray(ones)


def _inproj(x, g, w_all, b_f, heads, n_main, tm, sub):
    B, S, D = x.shape
    tri = jnp.asarray(np.tril(np.ones((sub, sub), np.float32)), BF16)
    e, ones = _aug_constants(heads)
    const = lambda r, c: pl.BlockSpec((r, c), lambda b, i: (0, 0), pipeline_mode=pl.Buffered(1))
    return pl.pallas_call(
        functools.partial(_inproj_kernel, heads=heads, sub=sub),
        grid=(B, S // tm),
        in_specs=[
            pl.BlockSpec((1, tm, D), lambda b, i: (b, i, 0)),
            const(1, D), const(D, n_main + LANES), const(1, LANES), const(sub, sub),
            const(PIECES * LANES, 2 * LANES), const(2, LANES),
        ],
        out_specs=[
            pl.BlockSpec((1, tm, n_main), lambda b, i: (b, i, 0)),
            pl.BlockSpec((1, tm, LANES), lambda b, i: (b, i, 0)),
            pl.BlockSpec((1, tm, LANES), lambda b, i: (b, i, 0)),
        ],
        out_shape=[
            jax.ShapeDtypeStruct((B, S, n_main), BF16),
            jax.ShapeDtypeStruct((B, S, LANES), BF16),
            jax.ShapeDtypeStruct((B, S, LANES), BF16),
        ],
        scratch_shapes=[pltpu.VMEM((1, LANES), F32)],
        compiler_params=pltpu.CompilerParams(
            dimension_semantics=("arbitrary", "arbitrary"), vmem_limit_bytes=VMEM_LIMIT),
        name="inproj",
    )(x, g, w_all, b_f, tri, e, ones)


def _head_mask(lane, hh):
    return lane < HEAD_DIM if hh == 0 else lane >= HEAD_DIM


def _value_aug(vv, lane, hh):
    ones = jnp.where(lane == SUM_LANE[hh], 1.0, 0.0).astype(vv.dtype)
    return jnp.where(_head_mask(lane, hh), vv, ones)


def _pair_output(acc0, acc1, lane):
    l0 = acc0[:, SUM_LANE[0]:SUM_LANE[0] + 1]
    l1 = acc1[:, SUM_LANE[1]:SUM_LANE[1] + 1]
    return jnp.where(_head_mask(lane, 0), acc0 * (1.0 / l0), acc1 * (1.0 / l1))


def _fox_kernel(q_ref, qa_ref, k_ref, ka_ref, v_ref, o_ref, *, tq, rb, pp, heads):
    grp = pl.program_id(1)
    lane = lax.broadcasted_iota(jnp.int32, (1, LANES), 1)
    k0 = PIECES * heads
    causal = lax.broadcasted_iota(jnp.int32, (rb, rb), 0) >= lax.broadcasted_iota(jnp.int32, (rb, rb), 1)

    def attend(n):
        units = [(pi, r, hh) for pi in range(pp) for r in range(tq // rb) for hh in range(2)]

        def spans(r):
            past = n * tq + r * rb
            return ([(0, past)] if past else []) + [(past, past + rb)]

        def logits(pi, r, hh):
            cols = slice(pi * PAIR, (pi + 1) * PAIR)
            rows = slice(r * rb, (r + 1) * rb)
            q = q_ref[0, rows, cols]
            qa = qa_ref[0, rows, :]
            a0 = PIECES * (2 * (grp * pp + pi) + hh)
            amask = ((lane >= a0) & (lane < a0 + PIECES)) | ((lane >= k0 + a0) & (lane < k0 + a0 + PIECES))
            q_aug = jnp.concatenate([jnp.where(_head_mask(lane, hh), q, jnp.zeros_like(q)),
                                     jnp.where(amask, qa, jnp.zeros_like(qa))], axis=-1)
            ss = []
            for lo, hi in spans(r):
                kaug = jnp.concatenate([k_ref[0, lo:hi, cols], ka_ref[0, lo:hi, :]], axis=-1)
                ss.append(_dot_nt(q_aug, kaug))
            ss[-1] = jnp.where(causal, ss[-1], NEG)
            return ss

        def softmax(ss):
            m = functools.reduce(jnp.maximum, [jnp.max(s, axis=-1, keepdims=True) for s in ss])
            return [jnp.exp2(s - m).astype(BF16) for s in ss]

        def weighted_values(pi, r, hh, ps):
            cols = slice(pi * PAIR, (pi + 1) * PAIR)
            acc = None
            for (lo, hi), p in zip(spans(r), ps):
                part = _dot(p, _value_aug(v_ref[0, lo:hi, cols], lane, hh))
                acc = part if acc is None else acc + part
            return acc

        ss, ps, accs = {}, {}, {}
        for t in range(len(units) + 2):
            if t < len(units):
                ss[t] = logits(*units[t])
            if 0 <= t - 1 < len(units):
                ps[t - 1] = softmax(ss.pop(t - 1))
            if 0 <= t - 2 < len(units):
                pi, r, hh = unit = units[t - 2]
                accs[unit] = weighted_values(*unit, ps.pop(t - 2))
                if hh == 1:
                    o_ref[0, r * rb:(r + 1) * rb, pi * PAIR:(pi + 1) * PAIR] = _pair_output(
                        accs.pop((pi, r, 0)), accs.pop((pi, r, 1)), lane).astype(o_ref.dtype)

    for n in range(k_ref.shape[1] // tq):
        pl.when(pl.program_id(2) == n)(functools.partial(attend, n))


def _fox(proj, qa, ka, heads, tq, rb, pp):
    B, S, _ = proj.shape
    groups = heads // 2 // pp
    w = pp * PAIR
    return pl.pallas_call(
        functools.partial(_fox_kernel, tq=tq, rb=rb, pp=pp, heads=heads),
        grid=(B, groups, S // tq),
        in_specs=[
            pl.BlockSpec((1, tq, w), lambda b, g, i: (b, i, g)),
            pl.BlockSpec((1, tq, LANES), lambda b, g, i: (b, i, 0)),
            pl.BlockSpec((1, S, w), lambda b, g, i: (b, 0, groups + g)),
            pl.BlockSpec((1, S, LANES), lambda b, g, i: (b, 0, 0)),
            pl.BlockSpec((1, S, w), lambda b, g, i: (b, 0, 2 * groups + g)),
        ],
        out_specs=pl.BlockSpec((1, tq, w), lambda b, g, i: (b, i, g)),
        out_shape=jax.ShapeDtypeStruct((B, S, heads * HEAD_DIM), BF16),
        compiler_params=pltpu.CompilerParams(
            dimension_semantics=("arbitrary", "arbitrary", "arbitrary"), vmem_limit_bytes=VMEM_LIMIT),
        name="fox_attn",
    )(proj, qa, proj, ka, proj)


def _chunk_kernel(q_ref, k_ref, v_ref, g_ref, o_ref, bias_ref, *, tq, pp):
    i = pl.program_id(2)
    lane = lax.broadcasted_iota(jnp.int32, (1, LANES), 1)

    @pl.when((pl.program_id(1) == 0) & (i == 0))
    def _():
        q_chunk = lax.broadcasted_iota(jnp.int32, (tq, tq), 0) // CHUNK
        k_chunk = lax.broadcasted_iota(jnp.int32, (tq, tq), 1) // CHUNK
        for which in range(3):
            back = which * (tq // CHUNK) + q_chunk - k_chunk
            vis = (back >= 0) & (back <= LEFT_CHUNKS)
            for hd in range(2 * pp):
                g = jnp.broadcast_to(g_ref[0, which, hd], (tq, 2 * tq))
                toep = pltpu.roll(g, 0, 1, stride=1, stride_axis=0)[:, :tq]
                bias_ref[hd, :, (2 - which) * tq:(3 - which) * tq] = jnp.where(vis, toep, NEG)

    def attend(n_blocks):
        span = n_blocks * tq
        ks = pl.multiple_of((i - (n_blocks - 1)) * tq, tq)
        units = [(pi, hh) for pi in range(pp) for hh in range(2)]

        def logits(pi, hh):
            cols = slice(pi * PAIR, (pi + 1) * PAIR)
            q = q_ref[0, :, cols]
            qh = jnp.where(_head_mask(lane, hh), q, jnp.zeros_like(q))
            return _dot_nt(qh, k_ref[0, pl.ds(ks, span), cols]) + bias_ref[2 * pi + hh, :, 3 * tq - span:]

        def softmax(s):
            return jnp.exp2(s - jnp.max(s, axis=-1, keepdims=True)).astype(BF16)

        def weighted_values(pi, hh, p):
            return _dot(p, _value_aug(v_ref[0, pl.ds(ks, span), pi * PAIR:(pi + 1) * PAIR], lane, hh))

        ss, ps, accs = {}, {}, {}
        for t in range(len(units) + 2):
            if t < len(units):
                ss[t] = logits(*units[t])
            if 0 <= t - 1 < len(units):
                ps[t - 1] = softmax(ss.pop(t - 1))
            if 0 <= t - 2 < len(units):
                pi, hh = unit = units[t - 2]
                accs[unit] = weighted_values(*unit, ps.pop(t - 2))
                if hh == 1:
                    o_ref[0, :, pi * PAIR:(pi + 1) * PAIR] = _pair_output(
                        accs.pop((pi, 0)), accs.pop((pi, 1)), lane).astype(o_ref.dtype)

    for n_blocks in (1, 2):
        pl.when(i == n_blocks - 1)(functools.partial(attend, n_blocks))
    pl.when(i >= 2)(functools.partial(attend, 3))


def _chunk_bias_rows(rel_table, tq, pp):
    heads = rel_table.shape[0]
    m = np.arange(2 * tq)
    offset = np.where(m < tq, -m, 2 * tq - m)
    idx = np.stack([np.clip(which * tq + offset, -MAX_REL, MAX_REL) + MAX_REL for which in range(3)])
    rows = rel_table.astype(F32)[:, idx] * LOG2E
    return rows.reshape(heads // (2 * pp), 2 * pp, 3, 1, 2 * tq).transpose(0, 2, 1, 3, 4)


def _chunk(proj, bias_rows, tq, pp, col0):
    B, S, _ = proj.shape
    groups = bias_rows.shape[0]
    w = pp * PAIR
    c0 = col0 // pp
    assert tq % CHUNK == 0 and LEFT_CHUNKS * CHUNK <= 2 * tq, "band must fit in three key blocks"
    return pl.pallas_call(
        functools.partial(_chunk_kernel, tq=tq, pp=pp),
        grid=(groups, B, S // tq),
        in_specs=[
            pl.BlockSpec((1, tq, w), lambda g, b, i: (b, i, c0 + g)),
            pl.BlockSpec((1, S, w), lambda g, b, i: (b, 0, c0 + groups + g)),
            pl.BlockSpec((1, S, w), lambda g, b, i: (b, 0, c0 + 2 * groups + g)),
            pl.BlockSpec((1, 3, 2 * pp, 1, 2 * tq), lambda g, b, i: (g, 0, 0, 0, 0)),
        ],
        out_specs=pl.BlockSpec((1, tq, w), lambda g, b, i: (b, i, g)),
        out_shape=jax.ShapeDtypeStruct((B, S, groups * w), BF16),
        scratch_shapes=[pltpu.VMEM((2 * pp, tq, 3 * tq), F32)],
        compiler_params=pltpu.CompilerParams(
            dimension_semantics=("arbitrary", "arbitrary", "arbitrary"), vmem_limit_bytes=VMEM_LIMIT),
        name="chunk_attn",
    )(proj, proj, proj, bias_rows)


def _memkv_kernel(mem_ref, g_ref, wk_ref, wv_ref, k_ref, v_ref):
    mb = _rms(mem_ref[0], g_ref[...]).astype(BF16)
    k_ref[0] = _dot(mb, wk_ref[...]).astype(BF16)
    v_ref[0] = _dot(mb, wv_ref[...]).astype(BF16)


def _memkv(mem, g, wk, wv):
    B, M, D = mem.shape
    full = lambda b: (0, 0)
    return pl.pallas_call(
        _memkv_kernel,
        grid=(B,),
        in_specs=[
            pl.BlockSpec((1, M, D), lambda b: (b, 0, 0)),
            pl.BlockSpec((1, D), full),
            pl.BlockSpec((D, D), full),
            pl.BlockSpec((D, D), full),
        ],
        out_specs=[pl.BlockSpec((1, M, D), lambda b: (b, 0, 0))] * 2,
        out_shape=[jax.ShapeDtypeStruct((B, M, D), BF16)] * 2,
        compiler_params=pltpu.CompilerParams(
            dimension_semantics=("arbitrary",), vmem_limit_bytes=VMEM_LIMIT),
        name="mem_kv",
    )(mem, g, wk, wv)


def _mix_mem_kernel(yf_ref, yc_ref, x_ref, gf_ref, gc_ref, wo_ref, gpost_ref,
                    gpre_ref, wq_ref, mk_ref, mv_ref, wmo_ref, gmpost_ref, o_ref, *, sub):
    d_fox = yf_ref.shape[2]
    halves = range(x_ref.shape[1] // sub)
    rows = [slice(h * sub, (h + 1) * sub) for h in halves]

    def mix(r):
        yf = _rms(yf_ref[0, r, :].astype(F32), gf_ref[...]).astype(BF16)
        yc = _rms(yc_ref[0, r, :].astype(F32), gc_ref[...]).astype(BF16)
        return _dot(yf, wo_ref[0:d_fox, :]) + _dot(yc, wo_ref[d_fox:, :])

    def query(x1):
        return _dot(_rms(x1, gpre_ref[...]).astype(BF16), wq_ref[...]).astype(BF16)

    def attend(q):
        dh = q.shape[1] // MEM_HEADS
        outs = []
        for hh in range(MEM_HEADS):
            sl = slice(hh * dh, (hh + 1) * dh)
            s = _dot_nt(q[:, sl], mk_ref[0, :, sl])
            pr = jnp.exp2(s - jnp.max(s, axis=-1, keepdims=True))
            inv = 1.0 / jnp.sum(pr, axis=-1, keepdims=True)
            outs.append((_dot(pr.astype(BF16), mv_ref[0, :, sl]) * inv).astype(BF16))
        return jnp.concatenate(outs, axis=-1)

    ys = [mix(rows[h]) for h in halves]
    x1s = [x_ref[0, rows[h], :] + _rms(ys[h], gpost_ref[...]) for h in halves]
    qs = [query(x1s[h]) for h in halves]
    os = [attend(qs[h]) for h in halves]
    y2s = [_dot(os[h], wmo_ref[...]) for h in halves]
    for h in halves:
        o_ref[0, rows[h], :] = x1s[h] + _rms(y2s[h], gmpost_ref[...])


def _mix_mem(yf, yc, x, gf, gc, wo, gpost, gpre, wq, mk, mv, wmo, gmpost, tm, sub):
    B, S, D = x.shape
    M = mk.shape[1]
    dg = yf.shape[2]
    vec = lambda n: pl.BlockSpec((1, n), lambda b, i: (0, 0))
    mat = lambda r, c: pl.BlockSpec((r, c), lambda b, i: (0, 0), pipeline_mode=pl.Buffered(1))
    return pl.pallas_call(
        functools.partial(_mix_mem_kernel, sub=sub),
        grid=(B, S // tm),
        in_specs=[
            pl.BlockSpec((1, tm, dg), lambda b, i: (b, i, 0)),
            pl.BlockSpec((1, tm, dg), lambda b, i: (b, i, 0)),
            pl.BlockSpec((1, tm, D), lambda b, i: (b, i, 0)),
            vec(dg), vec(dg), mat(2 * dg, D), vec(D),
            vec(D), mat(D, D),
            pl.BlockSpec((1, M, D), lambda b, i: (b, 0, 0)),
            pl.BlockSpec((1, M, D), lambda b, i: (b, 0, 0)),
            mat(D, D), vec(D),
        ],
        out_specs=pl.BlockSpec((1, tm, D), lambda b, i: (b, i, 0)),
        out_shape=jax.ShapeDtypeStruct((B, S, D), F32),
        compiler_params=pltpu.CompilerParams(
            dimension_semantics=("arbitrary", "arbitrary"), vmem_limit_bytes=VMEM_LIMIT),
        name="mix_mem",
    )(yf, yc, x, gf, gc, wo, gpost, gpre, wq, mk, mv, wmo, gmpost)


def _mlp_kernel(x_ref, gpre_ref, w1_ref, w2_ref, gpost_ref, o_ref, *, ff_block, sub):
    halves = range(x_ref.shape[0] // sub)
    rows = [slice(h * sub, (h + 1) * sub) for h in halves]
    hs = [_rms(x_ref[rows[h], :], gpre_ref[...]).astype(BF16) for h in halves]
    ys = [None for _ in halves]
    for n in range(0, w1_ref.shape[1], ff_block):
        for h in halves:
            a = jnp.maximum(_dot(hs[h], w1_ref[:, n:n + ff_block]), 0.0)
            part = _dot((a * a).astype(BF16), w2_ref[n:n + ff_block, :])
            ys[h] = part if ys[h] is None else ys[h] + part
    for h in halves:
        o_ref[rows[h], :] = x_ref[rows[h], :] + _rms(ys[h], gpost_ref[...])


def _mlp(x, gpre, w1, w2, gpost, tm, ff_block, sub):
    R, D = x.shape
    d_ff = w1.shape[1]
    return pl.pallas_call(
        functools.partial(_mlp_kernel, ff_block=ff_block, sub=sub),
        grid=(R // tm,),
        in_specs=[
            pl.BlockSpec((tm, D), lambda i: (i, 0)),
            pl.BlockSpec((1, D), lambda i: (0, 0)),
            pl.BlockSpec((D, d_ff), lambda i: (0, 0), pipeline_mode=pl.Buffered(1)),
            pl.BlockSpec((d_ff, D), lambda i: (0, 0), pipeline_mode=pl.Buffered(1)),
            pl.BlockSpec((1, D), lambda i: (0, 0)),
        ],
        out_specs=pl.BlockSpec((tm, D), lambda i: (i, 0)),
        out_shape=jax.ShapeDtypeStruct((R, D), F32),
        compiler_params=pltpu.CompilerParams(
            dimension_semantics=("arbitrary",), vmem_limit_bytes=VMEM_LIMIT),
        name="mlp",
    )(x, gpre, w1, w2, gpost)


def kernel(x, mem, w_in, b_fgt, rel_bias, g_fox_out, g_chk_out, w_out, g_mix_pre, g_mix_post,
           g_mem_kv, w_mq, w_mk, w_mv, w_mo, g_mem_pre, g_mem_post,
           w_ff1, w_ff2, g_ff_pre, g_ff_post):
    B, S, D = x.shape
    depth = w_in.shape[0]
    heads = b_fgt.shape[1]
    d_fox = heads * HEAD_DIM
    d_chk = rel_bias.shape[1] * HEAD_DIM
    assert d_fox == d_chk and heads % 2 == 0 and heads <= LANES
    pairs = heads // 2
    row = lambda v: v.reshape(1, -1)

    for l in range(depth):
        scale = HEAD_DIM ** -0.5 * LOG2E
        wl = w_in[l]
        o_f = 3 * d_fox
        w_f = wl[:, o_f:o_f + heads]
        w_all = jnp.concatenate([
            wl[:, :d_fox] * scale, wl[:, d_fox:o_f],
            wl[:, o_f + heads:o_f + heads + d_chk] * scale, wl[:, o_f + heads + d_chk:],
            jnp.pad(jnp.tile(w_f, (1, PIECES)), ((0, 0), (0, LANES - PIECES * heads))),
        ], axis=1).astype(BF16)
        b_f = jnp.pad(jnp.tile(b_fgt[l], PIECES), (0, LANES - PIECES * heads)).reshape(1, LANES)

        proj, qa, ka = _inproj(x, row(g_mix_pre[l]), w_all, b_f, heads, n_main=3 * (d_fox + d_chk),
                               tm=1024, sub=256)
        y_fox = _fox(proj, qa, ka, heads, tq=512, rb=256, pp=2)
        y_chk = _chunk(proj, _chunk_bias_rows(rel_bias[l], 256, 4), tq=256, pp=4, col0=3 * pairs)

        mk, mv = _memkv(mem, row(g_mem_kv[l]), w_mk[l].astype(BF16), w_mv[l].astype(BF16))
        mem_scale = (D // MEM_HEADS) ** -0.5 * LOG2E
        x = _mix_mem(y_fox, y_chk, x, row(g_fox_out[l]), row(g_chk_out[l]), w_out[l].astype(BF16),
                     row(g_mix_post[l]), row(g_mem_pre[l]), (w_mq[l] * mem_scale).astype(BF16),
                     mk, mv, w_mo[l].astype(BF16), row(g_mem_post[l]), tm=1024, sub=256)
        x = _mlp(x.reshape(B * S, D), row(g_ff_pre[l]), w_ff1[l].astype(BF16), w_ff2[l].astype(BF16),
                 row(g_ff_post[l]), tm=1024, ff_block=1024, sub=256).reshape(B, S, D)
    return x
```

```python
import functools
import math

import numpy as np
import jax
import jax.numpy as jnp
from jax import lax
from jax.experimental import pallas as pl
from jax.experimental.pallas import tpu as pltpu

F32 = jnp.float32
BF16 = jnp.bfloat16

EPS = 1e-6
HEAD_DIM = 64
PAIR = 2 * HEAD_DIM
CHUNK = 64
LEFT_CHUNKS = 8
MAX_REL = 128
MEM_HEADS = 4
NEG = -1e30
LOG2E = math.log2(math.e)
LANES = 128
VMEM_LIMIT = 56 * 1024 * 1024
PIECES = 3
SUM_LANE = (HEAD_DIM, 0)


def _rms(x, g):
    return x * lax.rsqrt(jnp.mean(x * x, axis=-1, keepdims=True) + EPS) * g


def _dot(a, b):
    return jnp.dot(a, b, preferred_element_type=F32)


def _dot_nt(a, b):
    return lax.dot_general(a, b, (((1,), (1,)), ((), ())), preferred_element_type=F32)


def _split3(x):
    hi = x.astype(BF16)
    r1 = x - hi.astype(F32)
    mid = r1.astype(BF16)
    lo = (r1 - mid.astype(F32)).astype(BF16)
    return hi, mid, lo


def _inproj_kernel(x_ref, g_ref, w_ref, wvt_ref, bf_ref, tri_ref, e_ref, ones_ref,
                   proj_ref, vt_ref, qa_ref, ka_ref, carry_ref, *, heads, sub):
    @pl.when(pl.program_id(1) == 0)
    def _():
        carry_ref[...] = jnp.zeros_like(carry_ref)

    n_main = proj_ref.shape[2]
    halves = range(x_ref.shape[1] // sub)
    rows = [slice(h * sub, (h + 1) * sub) for h in halves]
    lane = lax.broadcasted_iota(jnp.int32, (1, LANES), 1)
    hbs = [_rms(x_ref[0, rows[h], :], g_ref[...]).astype(BF16) for h in halves]
    fs = []
    for h in halves:
        res = _dot(hbs[h], w_ref[...])
        proj_ref[0, rows[h], :] = res[:, :n_main].astype(BF16)
        vt_ref[0, :, rows[h]] = _dot_nt(wvt_ref[...], hbs[h]).astype(BF16)
        fs.append(res[:, n_main:] + bf_ref[...])
    carry = carry_ref[...]
    for h in halves:
        f = fs[h]
        logf = jnp.minimum(f, 0.0) - jnp.log1p(jnp.exp(-jnp.abs(f)))
        hi, mid, lo = _split3(logf)
        packed = jnp.where(lane < heads, hi, jnp.where(lane < 2 * heads, mid, lo))
        cum = _dot(tri_ref[...], packed)
        c = (cum + pltpu.roll(cum, LANES - heads, 1) + pltpu.roll(cum, LANES - 2 * heads, 1)) + carry
        carry = c[sub - 1:sub, :]
        pieces = jnp.concatenate(_split3(c * LOG2E), axis=-1)
        sel = _dot(pieces, e_ref[...])
        qa_ref[0, rows[h], :] = (sel[:, :LANES] + ones_ref[0:1, :]).astype(BF16)
        ka_ref[0, rows[h], :] = (ones_ref[1:2, :] - sel[:, LANES:]).astype(BF16)
    carry_ref[...] = carry


def _aug_constants(heads):
    k0 = PIECES * heads
    assert 2 * k0 <= LANES
    e = np.zeros((PIECES * LANES, 2 * LANES), np.float32)
    for h in range(heads):
        for r in range(PIECES):
            e[r * LANES + h, PIECES * h + r] = 1.0
            e[r * LANES + h, LANES + k0 + PIECES * h + r] = 1.0
    ones = np.zeros((2, LANES), np.float32)
    ones[0, k0:2 * k0] = 1.0
    ones[1, 0:k0] = 1.0
    return jnp.asarray(e, BF16), jnp.asarray(ones)


def _inproj(x, g, w_all, wv_t, b_f, heads, tm, sub):
    B, S, D = x.shape
    n_main = w_all.shape[1] - LANES
    n_val = wv_t.shape[0]
    tri = jnp.asarray(np.tril(np.ones((sub, sub), np.float32)), BF16)
    e, ones = _aug_constants(heads)
    const = lambda r, c: pl.BlockSpec((r, c), lambda b, i: (0, 0), pipeline_mode=pl.Buffered(1))
    return pl.pallas_call(
        functools.partial(_inproj_kernel, heads=heads, sub=sub),
        grid=(B, S // tm),
        in_specs=[
            pl.BlockSpec((1, tm, D), lambda b, i: (b, i, 0)),
            const(1, D), const(D, n_main + LANES), const(n_val, D), const(1, LANES), const(sub, sub),
            const(PIECES * LANES, 2 * LANES), const(2, LANES),
        ],
        out_specs=[
            pl.BlockSpec((1, tm, n_main), lambda b, i: (b, i, 0)),
            pl.BlockSpec((1, n_val, tm), lambda b, i: (b, 0, i)),
            pl.BlockSpec((1, tm, LANES), lambda b, i: (b, i, 0)),
            pl.BlockSpec((1, tm, LANES), lambda b, i: (b, i, 0)),
        ],
        out_shape=[
            jax.ShapeDtypeStruct((B, S, n_main), BF16),
            jax.ShapeDtypeStruct((B, n_val, S), BF16),
            jax.ShapeDtypeStruct((B, S, LANES), BF16),
            jax.ShapeDtypeStruct((B, S, LANES), BF16),
        ],
        scratch_shapes=[pltpu.VMEM((1, LANES), F32)],
        compiler_params=pltpu.CompilerParams(
            dimension_semantics=("arbitrary", "arbitrary"), vmem_limit_bytes=VMEM_LIMIT),
        name="inproj",
    )(x, g, w_all, wv_t, b_f, tri, e, ones)


def _head_mask(lane, hh):
    return lane < HEAD_DIM if hh == 0 else lane >= HEAD_DIM


def _value_aug_t(vt, feat, hh):
    ones = jnp.where(feat == SUM_LANE[hh], 1.0, 0.0).astype(vt.dtype)
    return jnp.where(_head_mask(feat, hh), vt, ones)


def _softmax_t(ss):
    m = functools.reduce(jnp.maximum, [jnp.max(s, axis=0, keepdims=True) for s in ss])
    return [jnp.exp2(s - m).astype(BF16) for s in ss]


def _pair_output_t(acc0, acc1, feat):
    l0 = acc0[SUM_LANE[0]:SUM_LANE[0] + 1, :]
    l1 = acc1[SUM_LANE[1]:SUM_LANE[1] + 1, :]
    return jnp.where(_head_mask(feat, 0), acc0 * (1.0 / l0), acc1 * (1.0 / l1)).T


def _fox_kernel(q_ref, qa_ref, k_ref, ka_ref, vt_ref, o_ref, *, tq, rb, pp, heads):
    grp = pl.program_id(1)
    lane = lax.broadcasted_iota(jnp.int32, (1, LANES), 1)
    feat = lax.broadcasted_iota(jnp.int32, (PAIR, 1), 0)
    k0 = PIECES * heads
    causal = lax.broadcasted_iota(jnp.int32, (rb, rb), 0) <= lax.broadcasted_iota(jnp.int32, (rb, rb), 1)

    def attend(n):
        units = [(pi, r, hh) for pi in range(pp) for r in range(tq // rb) for hh in range(2)]

        def spans(r):
            past = n * tq + r * rb
            return ([(0, past)] if past else []) + [(past, past + rb)]

        def logits(pi, r, hh):
            cols = slice(pi * PAIR, (pi + 1) * PAIR)
            rows = slice(r * rb, (r + 1) * rb)
            q = q_ref[0, rows, cols]
            qa = qa_ref[0, rows, :]
            a0 = PIECES * (2 * (grp * pp + pi) + hh)
            amask = ((lane >= a0) & (lane < a0 + PIECES)) | ((lane >= k0 + a0) & (lane < k0 + a0 + PIECES))
            q_aug = jnp.concatenate([jnp.where(_head_mask(lane, hh), q, jnp.zeros_like(q)),
                                     jnp.where(amask, qa, jnp.zeros_like(qa))], axis=-1)
            ss = []
            for lo, hi in spans(r):
                kaug = jnp.concatenate([k_ref[0, lo:hi, cols], ka_ref[0, lo:hi, :]], axis=-1)
                ss.append(_dot_nt(kaug, q_aug))
            ss[-1] = jnp.where(causal, ss[-1], NEG)
            return ss

        def weighted_values(pi, r, hh, ps):
            acc = None
            for (lo, hi), p in zip(spans(r), ps):
                part = _dot(_value_aug_t(vt_ref[0, pi * PAIR:(pi + 1) * PAIR, lo:hi], feat, hh), p)
                acc = part if acc is None else acc + part
            return acc

        ss, ps, accs = {}, {}, {}
        for t in range(len(units) + 2):
            if t < len(units):
                ss[t] = logits(*units[t])
            if 0 <= t - 1 < len(units):
                ps[t - 1] = _softmax_t(ss.pop(t - 1))
            if 0 <= t - 2 < len(units):
                pi, r, hh = unit = units[t - 2]
                accs[unit] = weighted_values(*unit, ps.pop(t - 2))
                if hh == 1:
                    o_ref[0, r * rb:(r + 1) * rb, pi * PAIR:(pi + 1) * PAIR] = _pair_output_t(
                        accs.pop((pi, r, 0)), accs.pop((pi, r, 1)), feat).astype(o_ref.dtype)

    for n in range(k_ref.shape[1] // tq):
        pl.when(pl.program_id(2) == n)(functools.partial(attend, n))


def _fox(proj, vt, qa, ka, heads, tq, rb, pp):
    B, S, _ = proj.shape
    groups = heads // 2 // pp
    w = pp * PAIR
    return pl.pallas_call(
        functools.partial(_fox_kernel, tq=tq, rb=rb, pp=pp, heads=heads),
        grid=(B, groups, S // tq),
        in_specs=[
            pl.BlockSpec((1, tq, w), lambda b, g, i: (b, i, g)),
            pl.BlockSpec((1, tq, LANES), lambda b, g, i: (b, i, 0)),
            pl.BlockSpec((1, S, w), lambda b, g, i: (b, 0, groups + g)),
            pl.BlockSpec((1, S, LANES), lambda b, g, i: (b, 0, 0)),
            pl.BlockSpec((1, w, S), lambda b, g, i: (b, g, 0)),
        ],
        out_specs=pl.BlockSpec((1, tq, w), lambda b, g, i: (b, i, g)),
        out_shape=jax.ShapeDtypeStruct((B, S, heads * HEAD_DIM), BF16),
        compiler_params=pltpu.CompilerParams(
            dimension_semantics=("arbitrary", "arbitrary", "arbitrary"), vmem_limit_bytes=VMEM_LIMIT),
        name="fox_attn",
    )(proj, qa, proj, ka, vt)


def _chunk_kernel(q_ref, k_ref, vt_ref, g_ref, o_ref, bias_ref, *, tq, pp):
    i = pl.program_id(2)
    lane = lax.broadcasted_iota(jnp.int32, (1, LANES), 1)
    feat = lax.broadcasted_iota(jnp.int32, (PAIR, 1), 0)

    @pl.when((pl.program_id(1) == 0) & (i == 0))
    def _():
        k_chunk = lax.broadcasted_iota(jnp.int32, (tq, tq), 0) // CHUNK
        q_chunk = lax.broadcasted_iota(jnp.int32, (tq, tq), 1) // CHUNK
        for which in range(3):
            back = which * (tq // CHUNK) + q_chunk - k_chunk
            vis = (back >= 0) & (back <= LEFT_CHUNKS)
            for hd in range(2 * pp):
                g = jnp.broadcast_to(g_ref[0, which, hd], (tq, 2 * tq))
                toep = pltpu.roll(g, 0, 1, stride=1, stride_axis=0)[:, :tq]
                bias_ref[hd, (2 - which) * tq:(3 - which) * tq, :] = jnp.where(vis, toep, NEG)

    def attend(n_blocks):
        span = n_blocks * tq
        ks = pl.multiple_of((i - (n_blocks - 1)) * tq, tq)
        units = [(pi, hh) for pi in range(pp) for hh in range(2)]

        def logits(pi, hh):
            cols = slice(pi * PAIR, (pi + 1) * PAIR)
            q = q_ref[0, :, cols]
            qh = jnp.where(_head_mask(lane, hh), q, jnp.zeros_like(q))
            return _dot_nt(k_ref[0, pl.ds(ks, span), cols], qh) + bias_ref[2 * pi + hh, 3 * tq - span:, :]

        def weighted_values(pi, hh, p):
            return _dot(_value_aug_t(vt_ref[0, pi * PAIR:(pi + 1) * PAIR, pl.ds(ks, span)], feat, hh), p)

        ss, ps, accs = {}, {}, {}
        for t in range(len(units) + 2):
            if t < len(units):
                ss[t] = logits(*units[t])
            if 0 <= t - 1 < len(units):
                ps[t - 1] = _softmax_t([ss.pop(t - 1)])[0]
            if 0 <= t - 2 < len(units):
                pi, hh = unit = units[t - 2]
                accs[unit] = weighted_values(*unit, ps.pop(t - 2))
                if hh == 1:
                    o_ref[0, :, pi * PAIR:(pi + 1) * PAIR] = _pair_output_t(
                        accs.pop((pi, 0)), accs.pop((pi, 1)), feat).astype(o_ref.dtype)

    for n_blocks in (1, 2):
        pl.when(i == n_blocks - 1)(functools.partial(attend, n_blocks))
    pl.when(i >= 2)(functools.partial(attend, 3))


def _chunk_bias_rows(rel_table, tq, pp):
    heads = rel_table.shape[0]
    m = np.arange(2 * tq)
    offset = np.where(m < tq, m, m - 2 * tq)
    idx = np.stack([np.clip(which * tq + offset, -MAX_REL, MAX_REL) + MAX_REL for which in range(3)])
    rows = rel_table.astype(F32)[:, idx] * LOG2E
    return rows.reshape(heads // (2 * pp), 2 * pp, 3, 1, 2 * tq).transpose(0, 2, 1, 3, 4)


def _chunk(proj, vt, bias_rows, tq, pp, col0, row0):
    B, S, _ = proj.shape
    groups = bias_rows.shape[0]
    w = pp * PAIR
    assert tq % CHUNK == 0 and LEFT_CHUNKS * CHUNK <= 2 * tq, "band must fit in three key blocks"
    return pl.pallas_call(
        functools.partial(_chunk_kernel, tq=tq, pp=pp),
        grid=(groups, B, S // tq),
        in_specs=[
            pl.BlockSpec((1, tq, w), lambda g, b, i: (b, i, col0 + g)),
            pl.BlockSpec((1, S, w), lambda g, b, i: (b, 0, col0 + groups + g)),
            pl.BlockSpec((1, w, S), lambda g, b, i: (b, row0 + g, 0)),
            pl.BlockSpec((1, 3, 2 * pp, 1, 2 * tq), lambda g, b, i: (g, 0, 0, 0, 0)),
        ],
        out_specs=pl.BlockSpec((1, tq, w), lambda g, b, i: (b, i, g)),
        out_shape=jax.ShapeDtypeStruct((B, S, groups * w), BF16),
        scratch_shapes=[pltpu.VMEM((2 * pp, 3 * tq, tq), F32)],
        compiler_params=pltpu.CompilerParams(
            dimension_semantics=("arbitrary", "arbitrary", "arbitrary"), vmem_limit_bytes=VMEM_LIMIT),
        name="chunk_attn",
    )(proj, proj, vt, bias_rows)


def _memkv_kernel(mem_ref, g_ref, wk_ref, wv_ref, k_ref, v_ref):
    mb = _rms(mem_ref[0], g_ref[...]).astype(BF16)
    k_ref[0] = _dot(mb, wk_ref[...]).astype(BF16)
    v_ref[0] = _dot(mb, wv_ref[...]).astype(BF16)


def _memkv(mem, g, wk, wv):
    B, M, D = mem.shape
    full = lambda b: (0, 0)
    return pl.pallas_call(
        _memkv_kernel,
        grid=(B,),
        in_specs=[
            pl.BlockSpec((1, M, D), lambda b: (b, 0, 0)),
            pl.BlockSpec((1, D), full),
            pl.BlockSpec((D, D), full),
            pl.BlockSpec((D, D), full),
        ],
        out_specs=[pl.BlockSpec((1, M, D), lambda b: (b, 0, 0))] * 2,
        out_shape=[jax.ShapeDtypeStruct((B, M, D), BF16)] * 2,
        compiler_params=pltpu.CompilerParams(
            dimension_semantics=("arbitrary",), vmem_limit_bytes=VMEM_LIMIT),
        name="mem_kv",
    )(mem, g, wk, wv)


def _mix_mem_kernel(yf_ref, yc_ref, x_ref, gf_ref, gc_ref, wo_ref, gpost_ref,
                    gpre_ref, wq_ref, mk_ref, mv_ref, wmo_ref, gmpost_ref, o_ref, *, sub):
    d_fox = yf_ref.shape[2]
    halves = range(x_ref.shape[1] // sub)
    rows = [slice(h * sub, (h + 1) * sub) for h in halves]

    def mix(r):
        yf = _rms(yf_ref[0, r, :].astype(F32), gf_ref[...]).astype(BF16)
        yc = _rms(yc_ref[0, r, :].astype(F32), gc_ref[...]).astype(BF16)
        return _dot(yf, wo_ref[0:d_fox, :]) + _dot(yc, wo_ref[d_fox:, :])

    def query(x1):
        return _dot(_rms(x1, gpre_ref[...]).astype(BF16), wq_ref[...]).astype(BF16)

    def attend(q):
        dh = q.shape[1] // MEM_HEADS
        outs = []
        for hh in range(MEM_HEADS):
            sl = slice(hh * dh, (hh + 1) * dh)
            s = _dot_nt(q[:, sl], mk_ref[0, :, sl])
            pr = jnp.exp2(s - jnp.max(s, axis=-1, keepdims=True))
            inv = 1.0 / jnp.sum(pr, axis=-1, keepdims=True)
            outs.append((_dot(pr.astype(BF16), mv_ref[0, :, sl]) * inv).astype(BF16))
        return jnp.concatenate(outs, axis=-1)

    ys = [mix(rows[h]) for h in halves]
    x1s = [x_ref[0, rows[h], :] + _rms(ys[h], gpost_ref[...]) for h in halves]
    qs = [query(x1s[h]) for h in halves]
    os = [attend(qs[h]) for h in halves]
    y2s = [_dot(os[h], wmo_ref[...]) for h in halves]
    for h in halves:
        o_ref[0, rows[h], :] = x1s[h] + _rms(y2s[h], gmpost_ref[...])


def _mix_mem(yf, yc, x, gf, gc, wo, gpost, gpre, wq, mk, mv, wmo, gmpost, tm, sub):
    B, S, D = x.shape
    M = mk.shape[1]
    dg = yf.shape[2]
    vec = lambda n: pl.BlockSpec((1, n), lambda b, i: (0, 0))
    mat = lambda r, c: pl.BlockSpec((r, c), lambda b, i: (0, 0), pipeline_mode=pl.Buffered(1))
    return pl.pallas_call(
        functools.partial(_mix_mem_kernel, sub=sub),
        grid=(B, S // tm),
        in_specs=[
            pl.BlockSpec((1, tm, dg), lambda b, i: (b, i, 0)),
            pl.BlockSpec((1, tm, dg), lambda b, i: (b, i, 0)),
            pl.BlockSpec((1, tm, D), lambda b, i: (b, i, 0)),
            vec(dg), vec(dg), mat(2 * dg, D), vec(D),
            vec(D), mat(D, D),
            pl.BlockSpec((1, M, D), lambda b, i: (b, 0, 0)),
            pl.BlockSpec((1, M, D), lambda b, i: (b, 0, 0)),
            mat(D, D), vec(D),
        ],
        out_specs=pl.BlockSpec((1, tm, D), lambda b, i: (b, i, 0)),
        out_shape=jax.ShapeDtypeStruct((B, S, D), F32),
        compiler_params=pltpu.CompilerParams(
            dimension_semantics=("arbitrary", "arbitrary"), vmem_limit_bytes=VMEM_LIMIT),
        name="mix_mem",
    )(yf, yc, x, gf, gc, wo, gpost, gpre, wq, mk, mv, wmo, gmpost)


def _mlp_kernel(x_ref, gpre_ref, w1_ref, w2_ref, gpost_ref, o_ref, *, ff_block, sub):
    halves = range(x_ref.shape[0] // sub)
    rows = [slice(h * sub, (h + 1) * sub) for h in halves]
    hs = [_rms(x_ref[rows[h], :], gpre_ref[...]).astype(BF16) for h in halves]
    ys = [None for _ in halves]
    for n in range(0, w1_ref.shape[1], ff_block):
        for h in halves:
            a = jnp.maximum(_dot(hs[h], w1_ref[:, n:n + ff_block]), 0.0)
            part = _dot((a * a).astype(BF16), w2_ref[n:n + ff_block, :])
            ys[h] = part if ys[h] is None else ys[h] + part
    for h in halves:
        o_ref[rows[h], :] = x_ref[rows[h], :] + _rms(ys[h], gpost_ref[...])


def _mlp(x, gpre, w1, w2, gpost, tm, ff_block, sub):
    R, D = x.shape
    d_ff = w1.shape[1]
    return pl.pallas_call(
        functools.partial(_mlp_kernel, ff_block=ff_block, sub=sub),
        grid=(R // tm,),
        in_specs=[
            pl.BlockSpec((tm, D), lambda i: (i, 0)),
            pl.BlockSpec((1, D), lambda i: (0, 0)),
            pl.BlockSpec((D, d_ff), lambda i: (0, 0), pipeline_mode=pl.Buffered(1)),
            pl.BlockSpec((d_ff, D), lambda i: (0, 0), pipeline_mode=pl.Buffered(1)),
            pl.BlockSpec((1, D), lambda i: (0, 0)),
        ],
        out_specs=pl.BlockSpec((tm, D), lambda i: (i, 0)),
        out_shape=jax.ShapeDtypeStruct((R, D), F32),
        compiler_params=pltpu.CompilerParams(
            dimension_semantics=("arbitrary",), vmem_limit_bytes=VMEM_LIMIT),
        name="mlp",
    )(x, gpre, w1, w2, gpost)


def kernel(x, mem, w_in, b_fgt, rel_bias, g_fox_out, g_chk_out, w_out, g_mix_pre, g_mix_post,
           g_mem_kv, w_mq, w_mk, w_mv, w_mo, g_mem_pre, g_mem_post,
           w_ff1, w_ff2, g_ff_pre, g_ff_post):
    B, S, D = x.shape
    depth = w_in.shape[0]
    heads = b_fgt.shape[1]
    d_fox = heads * HEAD_DIM
    d_chk = rel_bias.shape[1] * HEAD_DIM
    assert d_fox == d_chk and heads % 2 == 0 and heads <= LANES
    pairs = heads // 2
    row = lambda v: v.reshape(1, -1)

    for l in range(depth):
        scale = HEAD_DIM ** -0.5 * LOG2E
        wl = w_in[l]
        o_f = 3 * d_fox
        w_f = wl[:, o_f:o_f + heads]
        o_c = o_f + heads
        w_all = jnp.concatenate([
            wl[:, :d_fox] * scale, wl[:, d_fox:2 * d_fox],
            wl[:, o_c:o_c + d_chk] * scale, wl[:, o_c + d_chk:o_c + 2 * d_chk],
            jnp.pad(jnp.tile(w_f, (1, PIECES)), ((0, 0), (0, LANES - PIECES * heads))),
        ], axis=1).astype(BF16)
        wv_t = jnp.concatenate([wl[:, 2 * d_fox:o_f], wl[:, o_c + 2 * d_chk:]], axis=1).T.astype(BF16)
        b_f = jnp.pad(jnp.tile(b_fgt[l], PIECES), (0, LANES - PIECES * heads)).reshape(1, LANES)

        proj, vt, qa, ka = _inproj(x, row(g_mix_pre[l]), w_all, wv_t, b_f, heads, tm=1024, sub=256)
        fox_pp, chk_pp = 2, 4
        y_fox = _fox(proj, vt, qa, ka, heads, tq=512, rb=256, pp=fox_pp)
        y_chk = _chunk(proj, vt, _chunk_bias_rows(rel_bias[l], 256, chk_pp), tq=256, pp=chk_pp,
                       col0=2 * pairs // chk_pp, row0=pairs // chk_pp)

        mk, mv = _memkv(mem, row(g_mem_kv[l]), w_mk[l].astype(BF16), w_mv[l].astype(BF16))
        mem_scale = (D // MEM_HEADS) ** -0.5 * LOG2E
        x = _mix_mem(y_fox, y_chk, x, row(g_fox_out[l]), row(g_chk_out[l]), w_out[l].astype(BF16),
                     row(g_mix_post[l]), row(g_mem_pre[l]), (w_mq[l] * mem_scale).astype(BF16),
                     mk, mv, w_mo[l].astype(BF16), row(g_mem_post[l]), tm=1024, sub=256)
        x = _mlp(x.reshape(B * S, D), row(g_ff_pre[l]), w_ff1[l].astype(BF16), w_ff2[l].astype(BF16),
                 row(g_ff_post[l]), tm=1024, ff_block=1024, sub=256).reshape(B, S, D)
    return x
```

```python
import functools
import math

import numpy as np
import jax
import jax.numpy as jnp
from jax import lax
from jax.experimental import pallas as pl
from jax.experimental.pallas import tpu as pltpu

F32 = jnp.float32
BF16 = jnp.bfloat16

EPS = 1e-6
HEAD_DIM = 64
PAIR = 2 * HEAD_DIM
CHUNK = 64
LEFT_CHUNKS = 8
MAX_REL = 128
MEM_HEADS = 4
NEG = -1e30
LOG2E = math.log2(math.e)
LANES = 128
VMEM_LIMIT = 56 * 1024 * 1024
PIECES = 3
SUM_LANE = (HEAD_DIM, 0)


def _rms(x, g):
    return x * lax.rsqrt(jnp.mean(x * x, axis=-1, keepdims=True) + EPS) * g


def _dot(a, b):
    return jnp.dot(a, b, preferred_element_type=F32)


def _dot_nt(a, b):
    return lax.dot_general(a, b, (((1,), (1,)), ((), ())), preferred_element_type=F32)


def _split3(x):
    hi = x.astype(BF16)
    r1 = x - hi.astype(F32)
    mid = r1.astype(BF16)
    lo = (r1 - mid.astype(F32)).astype(BF16)
    return hi, mid, lo


def _inproj_kernel(x_ref, g_ref, w_ref, wvt_ref, bf_ref, tri_ref, e_ref, ones_ref,
                   proj_ref, vt_ref, qa_ref, ka_ref, carry_ref, *, heads, sub):
    @pl.when(pl.program_id(1) == 0)
    def _():
        carry_ref[...] = jnp.zeros_like(carry_ref)

    n_main = proj_ref.shape[2]
    halves = range(x_ref.shape[1] // sub)
    rows = [slice(h * sub, (h + 1) * sub) for h in halves]
    lane = lax.broadcasted_iota(jnp.int32, (1, LANES), 1)
    hbs = [_rms(x_ref[0, rows[h], :], g_ref[...]).astype(BF16) for h in halves]
    fs = []
    for h in halves:
        res = _dot(hbs[h], w_ref[...])
        proj_ref[0, rows[h], :] = res[:, :n_main].astype(BF16)
        vt_ref[0, :, rows[h]] = _dot_nt(wvt_ref[...], hbs[h]).astype(BF16)
        fs.append(res[:, n_main:] + bf_ref[...])
    carry = carry_ref[...]
    for h in halves:
        f = fs[h]
        logf = jnp.minimum(f, 0.0) - jnp.log1p(jnp.exp(-jnp.abs(f)))
        hi, mid, lo = _split3(logf)
        packed = jnp.where(lane < heads, hi, jnp.where(lane < 2 * heads, mid, lo))
        cum = _dot(tri_ref[...], packed)
        c = (cum + pltpu.roll(cum, LANES - heads, 1) + pltpu.roll(cum, LANES - 2 * heads, 1)) + carry
        carry = c[sub - 1:sub, :]
        pieces = jnp.concatenate(_split3(c * LOG2E), axis=-1)
        sel = _dot(pieces, e_ref[...])
        qa_ref[0, rows[h], :] = (sel[:, :LANES] + ones_ref[0:1, :]).astype(BF16)
        ka_ref[0, rows[h], :] = (ones_ref[1:2, :] - sel[:, LANES:]).astype(BF16)
    carry_ref[...] = carry


def _aug_constants(heads):
    k0 = PIECES * heads
    assert 2 * k0 <= LANES
    e = np.zeros((PIECES * LANES, 2 * LANES), np.float32)
    for h in range(heads):
        for r in range(PIECES):
            e[r * LANES + h, PIECES * h + r] = 1.0
            e[r * LANES + h, LANES + k0 + PIECES * h + r] = 1.0
    ones = np.zeros((2, LANES), np.float32)
    ones[0, k0:2 * k0] = 1.0
    ones[1, 0:k0] = 1.0
    return jnp.asarray(e, BF16), jnp.asarray(ones)


def _inproj(x, g, w_all, wv_t, b_f, heads, tm, sub):
    B, S, D = x.shape
    n_main = w_all.shape[1] - LANES
    n_val = wv_t.shape[0]
    tri = jnp.asarray(np.tril(np.ones((sub, sub), np.float32)), BF16)
    e, ones = _aug_constants(heads)
    const = lambda r, c: pl.BlockSpec((r, c), lambda b, i: (0, 0), pipeline_mode=pl.Buffered(1))
    return pl.pallas_call(
        functools.partial(_inproj_kernel, heads=heads, sub=sub),
        grid=(B, S // tm),
        in_specs=[
            pl.BlockSpec((1, tm, D), lambda b, i: (b, i, 0)),
            const(1, D), const(D, n_main + LANES), const(n_val, D), const(1, LANES), const(sub, sub),
            const(PIECES * LANES, 2 * LANES), const(2, LANES),
        ],
        out_specs=[
            pl.BlockSpec((1, tm, n_main), lambda b, i: (b, i, 0)),
            pl.BlockSpec((1, n_val, tm), lambda b, i: (b, 0, i)),
            pl.BlockSpec((1, tm, LANES), lambda b, i: (b, i, 0)),
            pl.BlockSpec((1, tm, LANES), lambda b, i: (b, i, 0)),
        ],
        out_shape=[
            jax.ShapeDtypeStruct((B, S, n_main), BF16),
            jax.ShapeDtypeStruct((B, n_val, S), BF16),
            jax.ShapeDtypeStruct((B, S, LANES), BF16),
            jax.ShapeDtypeStruct((B, S, LANES), BF16),
        ],
        scratch_shapes=[pltpu.VMEM((1, LANES), F32)],
        compiler_params=pltpu.CompilerParams(
            dimension_semantics=("arbitrary", "arbitrary"), vmem_limit_bytes=VMEM_LIMIT),
        name="inproj",
    )(x, g, w_all, wv_t, b_f, tri, e, ones)


def _head_mask(lane, hh):
    return lane < HEAD_DIM if hh == 0 else lane >= HEAD_DIM


def _value_aug_t(vt, feat, hh):
    ones = jnp.where(feat == SUM_LANE[hh], 1.0, 0.0).astype(vt.dtype)
    return jnp.where(_head_mask(feat, hh), vt, ones)


def _softmax_t(ss):
    m = functools.reduce(jnp.maximum, [jnp.max(s, axis=0, keepdims=True) for s in ss])
    return [jnp.exp2(s - m).astype(BF16) for s in ss]


def _pair_output_t(acc0, acc1, feat):
    l0 = acc0[SUM_LANE[0]:SUM_LANE[0] + 1, :]
    l1 = acc1[SUM_LANE[1]:SUM_LANE[1] + 1, :]
    return jnp.where(_head_mask(feat, 0), acc0 * (1.0 / l0), acc1 * (1.0 / l1)).T


def _fox_kernel(q_ref, qa_ref, k_ref, ka_ref, vt_ref, o_ref, s_ref, p_ref, *, tq, rb, pp, heads):
    grp = pl.program_id(1)
    lane = lax.broadcasted_iota(jnp.int32, (1, LANES), 1)
    feat = lax.broadcasted_iota(jnp.int32, (PAIR, 1), 0)
    k0 = PIECES * heads
    causal = lax.broadcasted_iota(jnp.int32, (rb, rb), 0) <= lax.broadcasted_iota(jnp.int32, (rb, rb), 1)
    slots = s_ref.shape[0]

    def attend(n):
        units = [(pi, r, hh) for pi in range(pp) for r in range(tq // rb) for hh in range(2)]

        def keys(r):
            past = n * tq + r * rb
            return past, past + rb

        def logits(u):
            pi, r, hh = units[u]
            past, total = keys(r)
            q = q_ref[0, r * rb:(r + 1) * rb, pi * PAIR:(pi + 1) * PAIR]
            qa = qa_ref[0, r * rb:(r + 1) * rb, :]
            a0 = PIECES * (2 * (grp * pp + pi) + hh)
            amask = ((lane >= a0) & (lane < a0 + PIECES)) | ((lane >= k0 + a0) & (lane < k0 + a0 + PIECES))
            q_aug = jnp.concatenate([jnp.where(_head_mask(lane, hh), q, jnp.zeros_like(q)),
                                     jnp.where(amask, qa, jnp.zeros_like(qa))], axis=-1)
            m = None
            for lo, hi in ([(0, past)] if past else []) + [(past, total)]:
                kaug = jnp.concatenate([k_ref[0, lo:hi, pi * PAIR:(pi + 1) * PAIR], ka_ref[0, lo:hi, :]], axis=-1)
                s = _dot_nt(kaug, q_aug)
                if lo == past:
                    s = jnp.where(causal, s, NEG)
                s_ref[u % slots, lo:hi, :] = s
                col_max = jnp.max(s, axis=0, keepdims=True)
                m = col_max if m is None else jnp.maximum(m, col_max)
            return m

        def weights(u, m):
            _, total = keys(units[u][1])
            p_ref[u % slots, 0:total, :] = jnp.exp2(s_ref[u % slots, 0:total, :] - m).astype(BF16)

        def values(u):
            pi, r, hh = units[u]
            _, total = keys(r)
            return _dot(_value_aug_t(vt_ref[0, pi * PAIR:(pi + 1) * PAIR, 0:total], feat, hh),
                        p_ref[u % slots, 0:total, :])

        ms, accs = {}, {}
        n_units = len(units)
        for t in range(0, n_units + 4, 2):
            for u in (t, t + 1):
                if u < n_units:
                    ms[u] = logits(u)
            for u in (t - 2, t - 1):
                if 0 <= u < n_units:
                    weights(u, ms.pop(u))
            for u in (t - 4, t - 3):
                if 0 <= u < n_units:
                    accs[units[u]] = values(u)
            if 0 <= t - 4 < n_units:
                pi, r, _ = units[t - 4]
                o_ref[0, r * rb:(r + 1) * rb, pi * PAIR:(pi + 1) * PAIR] = _pair_output_t(
                    accs.pop((pi, r, 0)), accs.pop((pi, r, 1)), feat).astype(o_ref.dtype)

    for n in range(k_ref.shape[1] // tq):
        pl.when(pl.program_id(2) == n)(functools.partial(attend, n))


def _fox(proj, vt, qa, ka, heads, tq, rb, pp):
    B, S, _ = proj.shape
    groups = heads // 2 // pp
    w = pp * PAIR
    return pl.pallas_call(
        functools.partial(_fox_kernel, tq=tq, rb=rb, pp=pp, heads=heads),
        grid=(B, groups, S // tq),
        in_specs=[
            pl.BlockSpec((1, tq, w), lambda b, g, i: (b, i, g)),
            pl.BlockSpec((1, tq, LANES), lambda b, g, i: (b, i, 0)),
            pl.BlockSpec((1, S, w), lambda b, g, i: (b, 0, groups + g)),
            pl.BlockSpec((1, S, LANES), lambda b, g, i: (b, 0, 0)),
            pl.BlockSpec((1, w, S), lambda b, g, i: (b, g, 0)),
        ],
        out_specs=pl.BlockSpec((1, tq, w), lambda b, g, i: (b, i, g)),
        out_shape=jax.ShapeDtypeStruct((B, S, heads * HEAD_DIM), BF16),
        scratch_shapes=[pltpu.VMEM((6, S, rb), F32), pltpu.VMEM((6, S, rb), BF16)],
        compiler_params=pltpu.CompilerParams(
            dimension_semantics=("arbitrary", "arbitrary", "arbitrary"), vmem_limit_bytes=VMEM_LIMIT),
        name="fox_attn",
    )(proj, qa, proj, ka, vt)


def _chunk_kernel(q_ref, k_ref, vt_ref, g_ref, o_ref, bias_ref, *, tq, pp):
    i = pl.program_id(2)
    lane = lax.broadcasted_iota(jnp.int32, (1, LANES), 1)
    feat = lax.broadcasted_iota(jnp.int32, (PAIR, 1), 0)

    @pl.when((pl.program_id(1) == 0) & (i == 0))
    def _():
        k_chunk = lax.broadcasted_iota(jnp.int32, (tq, tq), 0) // CHUNK
        q_chunk = lax.broadcasted_iota(jnp.int32, (tq, tq), 1) // CHUNK
        for which in range(3):
            back = which * (tq // CHUNK) + q_chunk - k_chunk
            vis = (back >= 0) & (back <= LEFT_CHUNKS)
            for hd in range(2 * pp):
                g = jnp.broadcast_to(g_ref[0, which, hd], (tq, 2 * tq))
                toep = pltpu.roll(g, 0, 1, stride=1, stride_axis=0)[:, :tq]
                bias_ref[hd, (2 - which) * tq:(3 - which) * tq, :] = jnp.where(vis, toep, NEG)

    def attend(n_blocks):
        span = n_blocks * tq
        ks = pl.multiple_of((i - (n_blocks - 1)) * tq, tq)
        units = [(pi, hh) for pi in range(pp) for hh in range(2)]

        def logits(pi, hh):
            cols = slice(pi * PAIR, (pi + 1) * PAIR)
            q = q_ref[0, :, cols]
            qh = jnp.where(_head_mask(lane, hh), q, jnp.zeros_like(q))
            return _dot_nt(k_ref[0, pl.ds(ks, span), cols], qh) + bias_ref[2 * pi + hh, 3 * tq - span:, :]

        def weighted_values(pi, hh, p):
            return _dot(_value_aug_t(vt_ref[0, pi * PAIR:(pi + 1) * PAIR, pl.ds(ks, span)], feat, hh), p)

        ss, ps, accs = {}, {}, {}
        for t in range(len(units) + 2):
            if t < len(units):
                ss[t] = logits(*units[t])
            if 0 <= t - 1 < len(units):
                ps[t - 1] = _softmax_t([ss.pop(t - 1)])[0]
            if 0 <= t - 2 < len(units):
                pi, hh = unit = units[t - 2]
                accs[unit] = weighted_values(*unit, ps.pop(t - 2))
                if hh == 1:
                    o_ref[0, :, pi * PAIR:(pi + 1) * PAIR] = _pair_output_t(
                        accs.pop((pi, 0)), accs.pop((pi, 1)), feat).astype(o_ref.dtype)

    for n_blocks in (1, 2):
        pl.when(i == n_blocks - 1)(functools.partial(attend, n_blocks))
    pl.when(i >= 2)(functools.partial(attend, 3))


def _chunk_bias_rows(rel_table, tq, pp):
    heads = rel_table.shape[0]
    m = np.arange(2 * tq)
    offset = np.where(m < tq, m, m - 2 * tq)
    idx = np.stack([np.clip(which * tq + offset, -MAX_REL, MAX_REL) + MAX_REL for which in range(3)])
    rows = rel_table.astype(F32)[:, idx] * LOG2E
    return rows.reshape(heads // (2 * pp), 2 * pp, 3, 1, 2 * tq).transpose(0, 2, 1, 3, 4)


def _chunk(proj, vt, bias_rows, tq, pp, col0, row0):
    B, S, _ = proj.shape
    groups = bias_rows.shape[0]
    w = pp * PAIR
    assert tq % CHUNK == 0 and LEFT_CHUNKS * CHUNK <= 2 * tq, "band must fit in three key blocks"
    return pl.pallas_call(
        functools.partial(_chunk_kernel, tq=tq, pp=pp),
        grid=(groups, B, S // tq),
        in_specs=[
            pl.BlockSpec((1, tq, w), lambda g, b, i: (b, i, col0 + g)),
            pl.BlockSpec((1, S, w), lambda g, b, i: (b, 0, col0 + groups + g)),
            pl.BlockSpec((1, w, S), lambda g, b, i: (b, row0 + g, 0)),
            pl.BlockSpec((1, 3, 2 * pp, 1, 2 * tq), lambda g, b, i: (g, 0, 0, 0, 0)),
        ],
        out_specs=pl.BlockSpec((1, tq, w), lambda g, b, i: (b, i, g)),
        out_shape=jax.ShapeDtypeStruct((B, S, groups * w), BF16),
        scratch_shapes=[pltpu.VMEM((2 * pp, 3 * tq, tq), F32)],
        compiler_params=pltpu.CompilerParams(
            dimension_semantics=("arbitrary", "arbitrary", "arbitrary"), vmem_limit_bytes=VMEM_LIMIT),
        name="chunk_attn",
    )(proj, proj, vt, bias_rows)


def _memkv_kernel(mem_ref, g_ref, wk_ref, wv_ref, k_ref, v_ref):
    mb = _rms(mem_ref[0], g_ref[...]).astype(BF16)
    k_ref[0] = _dot(mb, wk_ref[...]).astype(BF16)
    v_ref[0] = _dot(mb, wv_ref[...]).astype(BF16)


def _memkv(mem, g, wk, wv):
    B, M, D = mem.shape
    full = lambda b: (0, 0)
    return pl.pallas_call(
        _memkv_kernel,
        grid=(B,),
        in_specs=[
            pl.BlockSpec((1, M, D), lambda b: (b, 0, 0)),
            pl.BlockSpec((1, D), full),
            pl.BlockSpec((D, D), full),
            pl.BlockSpec((D, D), full),
        ],
        out_specs=[pl.BlockSpec((1, M, D), lambda b: (b, 0, 0))] * 2,
        out_shape=[jax.ShapeDtypeStruct((B, M, D), BF16)] * 2,
        compiler_params=pltpu.CompilerParams(
            dimension_semantics=("arbitrary",), vmem_limit_bytes=VMEM_LIMIT),
        name="mem_kv",
    )(mem, g, wk, wv)


def _mix_mem_kernel(yf_ref, yc_ref, x_ref, gf_ref, gc_ref, wo_ref, gpost_ref,
                    gpre_ref, wq_ref, mk_ref, mv_ref, wmo_ref, gmpost_ref, o_ref, *, sub):
    d_fox = yf_ref.shape[2]
    halves = range(x_ref.shape[1] // sub)
    rows = [slice(h * sub, (h + 1) * sub) for h in halves]

    def mix(r):
        yf = _rms(yf_ref[0, r, :].astype(F32), gf_ref[...]).astype(BF16)
        yc = _rms(yc_ref[0, r, :].astype(F32), gc_ref[...]).astype(BF16)
        return _dot(yf, wo_ref[0:d_fox, :]) + _dot(yc, wo_ref[d_fox:, :])

    def query(x1):
        return _dot(_rms(x1, gpre_ref[...]).astype(BF16), wq_ref[...]).astype(BF16)

    def attend(q):
        dh = q.shape[1] // MEM_HEADS
        outs = []
        for hh in range(MEM_HEADS):
            sl = slice(hh * dh, (hh + 1) * dh)
            s = _dot_nt(q[:, sl], mk_ref[0, :, sl])
            pr = jnp.exp2(s - jnp.max(s, axis=-1, keepdims=True))
            inv = 1.0 / jnp.sum(pr, axis=-1, keepdims=True)
            outs.append((_dot(pr.astype(BF16), mv_ref[0, :, sl]) * inv).astype(BF16))
        return jnp.concatenate(outs, axis=-1)

    ys = [mix(rows[h]) for h in halves]
    x1s = [x_ref[0, rows[h], :] + _rms(ys[h], gpost_ref[...]) for h in halves]
    qs = [query(x1s[h]) for h in halves]
    os = [attend(qs[h]) for h in halves]
    y2s = [_dot(os[h], wmo_ref[...]) for h in halves]
    for h in halves:
        o_ref[0, rows[h], :] = x1s[h] + _rms(y2s[h], gmpost_ref[...])


def _mix_mem(yf, yc, x, gf, gc, wo, gpost, gpre, wq, mk, mv, wmo, gmpost, tm, sub):
    B, S, D = x.shape
    M = mk.shape[1]
    dg = yf.shape[2]
    vec = lambda n: pl.BlockSpec((1, n), lambda b, i: (0, 0))
    mat = lambda r, c: pl.BlockSpec((r, c), lambda b, i: (0, 0), pipeline_mode=pl.Buffered(1))
    return pl.pallas_call(
        functools.partial(_mix_mem_kernel, sub=sub),
        grid=(B, S // tm),
        in_specs=[
            pl.BlockSpec((1, tm, dg), lambda b, i: (b, i, 0)),
            pl.BlockSpec((1, tm, dg), lambda b, i: (b, i, 0)),
            pl.BlockSpec((1, tm, D), lambda b, i: (b, i, 0)),
            vec(dg), vec(dg), mat(2 * dg, D), vec(D),
            vec(D), mat(D, D),
            pl.BlockSpec((1, M, D), lambda b, i: (b, 0, 0)),
            pl.BlockSpec((1, M, D), lambda b, i: (b, 0, 0)),
            mat(D, D), vec(D),
        ],
        out_specs=pl.BlockSpec((1, tm, D), lambda b, i: (b, i, 0)),
        out_shape=jax.ShapeDtypeStruct((B, S, D), F32),
        compiler_params=pltpu.CompilerParams(
            dimension_semantics=("arbitrary", "arbitrary"), vmem_limit_bytes=VMEM_LIMIT),
        name="mix_mem",
    )(yf, yc, x, gf, gc, wo, gpost, gpre, wq, mk, mv, wmo, gmpost)


def _mlp_kernel(x_ref, gpre_ref, w1_ref, w2_ref, gpost_ref, o_ref, *, ff_block, sub):
    halves = range(x_ref.shape[0] // sub)
    rows = [slice(h * sub, (h + 1) * sub) for h in halves]
    hs = [_rms(x_ref[rows[h], :], gpre_ref[...]).astype(BF16) for h in halves]
    ys = [None for _ in halves]
    for n in range(0, w1_ref.shape[1], ff_block):
        for h in halves:
            a = jnp.maximum(_dot(hs[h], w1_ref[:, n:n + ff_block]), 0.0)
            part = _dot((a * a).astype(BF16), w2_ref[n:n + ff_block, :])
            ys[h] = part if ys[h] is None else ys[h] + part
    for h in halves:
        o_ref[rows[h], :] = x_ref[rows[h], :] + _rms(ys[h], gpost_ref[...])


def _mlp(x, gpre, w1, w2, gpost, tm, ff_block, sub):
    R, D = x.shape
    d_ff = w1.shape[1]
    return pl.pallas_call(
        functools.partial(_mlp_kernel, ff_block=ff_block, sub=sub),
        grid=(R // tm,),
        in_specs=[
            pl.BlockSpec((tm, D), lambda i: (i, 0)),
            pl.BlockSpec((1, D), lambda i: (0, 0)),
            pl.BlockSpec((D, d_ff), lambda i: (0, 0), pipeline_mode=pl.Buffered(1)),
            pl.BlockSpec((d_ff, D), lambda i: (0, 0), pipeline_mode=pl.Buffered(1)),
            pl.BlockSpec((1, D), lambda i: (0, 0)),
        ],
        out_specs=pl.BlockSpec((tm, D), lambda i: (i, 0)),
        out_shape=jax.ShapeDtypeStruct((R, D), F32),
        compiler_params=pltpu.CompilerParams(
            dimension_semantics=("arbitrary",), vmem_limit_bytes=VMEM_LIMIT),
        name="mlp",
    )(x, gpre, w1, w2, gpost)


def kernel(x, mem, w_in, b_fgt, rel_bias, g_fox_out, g_chk_out, w_out, g_mix_pre, g_mix_post,
           g_mem_kv, w_mq, w_mk, w_mv, w_mo, g_mem_pre, g_mem_post,
           w_ff1, w_ff2, g_ff_pre, g_ff_post):
    B, S, D = x.shape
    depth = w_in.shape[0]
    heads = b_fgt.shape[1]
    d_fox = heads * HEAD_DIM
    d_chk = rel_bias.shape[1] * HEAD_DIM
    assert d_fox == d_chk and heads % 2 == 0 and heads <= LANES
    pairs = heads // 2
    row = lambda v: v.reshape(1, -1)

    for l in range(depth):
        scale = HEAD_DIM ** -0.5 * LOG2E
        wl = w_in[l]
        o_f = 3 * d_fox
        w_f = wl[:, o_f:o_f + heads]
        o_c = o_f + heads
        w_all = jnp.concatenate([
            wl[:, :d_fox] * scale, wl[:, d_fox:2 * d_fox],
            wl[:, o_c:o_c + d_chk] * scale, wl[:, o_c + d_chk:o_c + 2 * d_chk],
            jnp.pad(jnp.tile(w_f, (1, PIECES)), ((0, 0), (0, LANES - PIECES * heads))),
        ], axis=1).astype(BF16)
        wv_t = jnp.concatenate([wl[:, 2 * d_fox:o_f], wl[:, o_c + 2 * d_chk:]], axis=1).T.astype(BF16)
        b_f = jnp.pad(jnp.tile(b_fgt[l], PIECES), (0, LANES - PIECES * heads)).reshape(1, LANES)

        proj, vt, qa, ka = _inproj(x, row(g_mix_pre[l]), w_all, wv_t, b_f, heads, tm=1024, sub=256)
        fox_pp, chk_pp = 2, 4
        y_fox = _fox(proj, vt, qa, ka, heads, tq=512, rb=256, pp=fox_pp)
        y_chk = _chunk(proj, vt, _chunk_bias_rows(rel_bias[l], 256, chk_pp), tq=256, pp=chk_pp,
                       col0=2 * pairs // chk_pp, row0=pairs // chk_pp)

        mk, mv = _memkv(mem, row(g_mem_kv[l]), w_mk[l].astype(BF16), w_mv[l].astype(BF16))
        mem_scale = (D // MEM_HEADS) ** -0.5 * LOG2E
        x = _mix_mem(y_fox, y_chk, x, row(g_fox_out[l]), row(g_chk_out[l]), w_out[l].astype(BF16),
                     row(g_mix_post[l]), row(g_mem_pre[l]), (w_mq[l] * mem_scale).astype(BF16),
                     mk, mv, w_mo[l].astype(BF16), row(g_mem_post[l]), tm=1024, sub=256)
        x = _mlp(x.reshape(B * S, D), row(g_ff_pre[l]), w_ff1[l].astype(BF16), w_ff2[l].astype(BF16),
                 row(g_ff_post[l]), tm=1024, ff_block=1024, sub=256).reshape(B, S, D)
    return x
```

```python
import functools
import math

import numpy as np
import jax
import jax.numpy as jnp
from jax import lax
from jax.experimental import pallas as pl
from jax.experimental.pallas import tpu as pltpu

F32 = jnp.float32
BF16 = jnp.bfloat16

EPS = 1e-6
HEAD_DIM = 64
PAIR = 2 * HEAD_DIM
CHUNK = 64
LEFT_CHUNKS = 8
MAX_REL = 128
MEM_HEADS = 4
NEG = -1e30
LOG2E = math.log2(math.e)
LANES = 128
VMEM_LIMIT = 56 * 1024 * 1024
PIECES = 3
SUM_LANE = (HEAD_DIM, 0)


def _rms(x, g):
    return x * lax.rsqrt(jnp.mean(x * x, axis=-1, keepdims=True) + EPS) * g


def _dot(a, b):
    return jnp.dot(a, b, preferred_element_type=F32)


def _dot_nt(a, b):
    return lax.dot_general(a, b, (((1,), (1,)), ((), ())), preferred_element_type=F32)


def _split3(x):
    hi = x.astype(BF16)
    r1 = x - hi.astype(F32)
    mid = r1.astype(BF16)
    lo = (r1 - mid.astype(F32)).astype(BF16)
    return hi, mid, lo


def _inproj_kernel(x_ref, g_ref, w_ref, wvt_ref, bf_ref, tri_ref, e_ref, ones_ref,
                   proj_ref, vt_ref, qa_ref, ka_ref, carry_ref, *, heads, sub):
    @pl.when(pl.program_id(1) == 0)
    def _():
        carry_ref[...] = jnp.zeros_like(carry_ref)

    n_main = proj_ref.shape[2]
    halves = range(x_ref.shape[1] // sub)
    rows = [slice(h * sub, (h + 1) * sub) for h in halves]
    lane = lax.broadcasted_iota(jnp.int32, (1, LANES), 1)
    hbs = [_rms(x_ref[0, rows[h], :], g_ref[...]).astype(BF16) for h in halves]
    fs = []
    for h in halves:
        res = _dot(hbs[h], w_ref[...])
        proj_ref[0, rows[h], :] = res[:, :n_main].astype(BF16)
        vt_ref[0, :, rows[h]] = _dot_nt(wvt_ref[...], hbs[h]).astype(BF16)
        fs.append(res[:, n_main:] + bf_ref[...])
    carry = carry_ref[...]
    for h in halves:
        f = fs[h]
        logf = jnp.minimum(f, 0.0) - jnp.log1p(jnp.exp(-jnp.abs(f)))
        hi, mid, lo = _split3(logf)
        packed = jnp.where(lane < heads, hi, jnp.where(lane < 2 * heads, mid, lo))
        cum = _dot(tri_ref[...], packed)
        c = (cum + pltpu.roll(cum, LANES - heads, 1) + pltpu.roll(cum, LANES - 2 * heads, 1)) + carry
        carry = c[sub - 1:sub, :]
        pieces = jnp.concatenate(_split3(c * LOG2E), axis=-1)
        sel = _dot(pieces, e_ref[...])
        qa_ref[0, rows[h], :] = (sel[:, :LANES] + ones_ref[0:1, :]).astype(BF16)
        ka_ref[0, rows[h], :] = (ones_ref[1:2, :] - sel[:, LANES:]).astype(BF16)
    carry_ref[...] = carry


def _aug_constants(heads):
    k0 = PIECES * heads
    assert 2 * k0 <= LANES
    e = np.zeros((PIECES * LANES, 2 * LANES), np.float32)
    for h in range(heads):
        for r in range(PIECES):
            e[r * LANES + h, PIECES * h + r] = 1.0
            e[r * LANES + h, LANES + k0 + PIECES * h + r] = 1.0
    ones = np.zeros((2, LANES), np.float32)
    ones[0, k0:2 * k0] = 1.0
    ones[1, 0:k0] = 1.0
    return jnp.asarray(e, BF16), jnp.asarray(ones)


def _inproj(x, g, w_all, wv_t, b_f, heads, tm, sub):
    B, S, D = x.shape
    n_main = w_all.shape[1] - LANES
    n_val = wv_t.shape[0]
    tri = jnp.asarray(np.tril(np.ones((sub, sub), np.float32)), BF16)
    e, ones = _aug_constants(heads)
    const = lambda r, c: pl.BlockSpec((r, c), lambda b, i: (0, 0), pipeline_mode=pl.Buffered(1))
    return pl.pallas_call(
        functools.partial(_inproj_kernel, heads=heads, sub=sub),
        grid=(B, S // tm),
        in_specs=[
            pl.BlockSpec((1, tm, D), lambda b, i: (b, i, 0)),
            const(1, D), const(D, n_main + LANES), const(n_val, D), const(1, LANES), const(sub, sub),
            const(PIECES * LANES, 2 * LANES), const(2, LANES),
        ],
        out_specs=[
            pl.BlockSpec((1, tm, n_main), lambda b, i: (b, i, 0)),
            pl.BlockSpec((1, n_val, tm), lambda b, i: (b, 0, i)),
            pl.BlockSpec((1, tm, LANES), lambda b, i: (b, i, 0)),
            pl.BlockSpec((1, tm, LANES), lambda b, i: (b, i, 0)),
        ],
        out_shape=[
            jax.ShapeDtypeStruct((B, S, n_main), BF16),
            jax.ShapeDtypeStruct((B, n_val, S), BF16),
            jax.ShapeDtypeStruct((B, S, LANES), BF16),
            jax.ShapeDtypeStruct((B, S, LANES), BF16),
        ],
        scratch_shapes=[pltpu.VMEM((1, LANES), F32)],
        compiler_params=pltpu.CompilerParams(
            dimension_semantics=("arbitrary", "arbitrary"), vmem_limit_bytes=VMEM_LIMIT),
        name="inproj",
    )(x, g, w_all, wv_t, b_f, tri, e, ones)


def _head_mask(lane, hh):
    return lane < HEAD_DIM if hh == 0 else lane >= HEAD_DIM


def _value_aug_t(vt, feat, hh):
    ones = jnp.where(feat == SUM_LANE[hh], 1.0, 0.0).astype(vt.dtype)
    return jnp.where(_head_mask(feat, hh), vt, ones)


def _softmax_t(ss):
    m = functools.reduce(jnp.maximum, [jnp.max(s, axis=0, keepdims=True) for s in ss])
    return [jnp.exp2(s - m).astype(BF16) for s in ss]


def _pair_output_t(acc0, acc1, feat):
    l0 = acc0[SUM_LANE[0]:SUM_LANE[0] + 1, :]
    l1 = acc1[SUM_LANE[1]:SUM_LANE[1] + 1, :]
    return jnp.where(_head_mask(feat, 0), acc0 * (1.0 / l0), acc1 * (1.0 / l1)).T


def _fox_kernel(q_ref, qa_ref, k_ref, ka_ref, vt_ref, o_ref, s_ref, p_ref, *, tq, rb, pp, heads):
    grp = pl.program_id(1)
    lane = lax.broadcasted_iota(jnp.int32, (1, LANES), 1)
    feat = lax.broadcasted_iota(jnp.int32, (PAIR, 1), 0)
    k0 = PIECES * heads
    causal = lax.broadcasted_iota(jnp.int32, (rb, rb), 0) <= lax.broadcasted_iota(jnp.int32, (rb, rb), 1)
    slots = s_ref.shape[0]

    def attend(n):
        units = [(pi, r, hh) for pi in range(pp) for r in range(tq // rb) for hh in range(2)]

        def keys(r):
            past = n * tq + r * rb
            return past, past + rb

        def logits(u):
            pi, r, hh = units[u]
            past, total = keys(r)
            q = q_ref[0, r * rb:(r + 1) * rb, pi * PAIR:(pi + 1) * PAIR]
            qa = qa_ref[0, r * rb:(r + 1) * rb, :]
            a0 = PIECES * (2 * (grp * pp + pi) + hh)
            amask = ((lane >= a0) & (lane < a0 + PIECES)) | ((lane >= k0 + a0) & (lane < k0 + a0 + PIECES))
            q_aug = jnp.concatenate([jnp.where(_head_mask(lane, hh), q, jnp.zeros_like(q)),
                                     jnp.where(amask, qa, jnp.zeros_like(qa))], axis=-1)
            m = None
            for lo, hi in ([(0, past)] if past else []) + [(past, total)]:
                kaug = jnp.concatenate([k_ref[0, lo:hi, pi * PAIR:(pi + 1) * PAIR], ka_ref[0, lo:hi, :]], axis=-1)
                s = _dot_nt(kaug, q_aug)
                if lo == past:
                    s = jnp.where(causal, s, NEG)
                s_ref[u % slots, lo:hi, :] = s
                col_max = jnp.max(s, axis=0, keepdims=True)
                m = col_max if m is None else jnp.maximum(m, col_max)
            return m

        def weights(u, m):
            _, total = keys(units[u][1])
            p_ref[u % slots, 0:total, :] = jnp.exp2(s_ref[u % slots, 0:total, :] - m).astype(BF16)

        def values(u):
            pi, r, hh = units[u]
            _, total = keys(r)
            return _dot(_value_aug_t(vt_ref[0, pi * PAIR:(pi + 1) * PAIR, 0:total], feat, hh),
                        p_ref[u % slots, 0:total, :])

        ms, accs = {}, {}
        n_units = len(units)
        for t in range(0, n_units + 4, 2):
            for u in (t, t + 1):
                if u < n_units:
                    ms[u] = logits(u)
            for u in (t - 2, t - 1):
                if 0 <= u < n_units:
                    weights(u, ms.pop(u))
            for u in (t - 4, t - 3):
                if 0 <= u < n_units:
                    accs[units[u]] = values(u)
            if 0 <= t - 4 < n_units:
                pi, r, _ = units[t - 4]
                o_ref[0, r * rb:(r + 1) * rb, pi * PAIR:(pi + 1) * PAIR] = _pair_output_t(
                    accs.pop((pi, r, 0)), accs.pop((pi, r, 1)), feat).astype(o_ref.dtype)

    for n in range(k_ref.shape[1] // tq):
        pl.when(pl.program_id(2) == n)(functools.partial(attend, n))


def _fox(proj, vt, qa, ka, heads, tq, rb, pp):
    B, S, _ = proj.shape
    groups = heads // 2 // pp
    w = pp * PAIR
    return pl.pallas_call(
        functools.partial(_fox_kernel, tq=tq, rb=rb, pp=pp, heads=heads),
        grid=(B, groups, S // tq),
        in_specs=[
            pl.BlockSpec((1, tq, w), lambda b, g, i: (b, i, g)),
            pl.BlockSpec((1, tq, LANES), lambda b, g, i: (b, i, 0)),
            pl.BlockSpec((1, S, w), lambda b, g, i: (b, 0, groups + g)),
            pl.BlockSpec((1, S, LANES), lambda b, g, i: (b, 0, 0)),
            pl.BlockSpec((1, w, S), lambda b, g, i: (b, g, 0)),
        ],
        out_specs=pl.BlockSpec((1, tq, w), lambda b, g, i: (b, i, g)),
        out_shape=jax.ShapeDtypeStruct((B, S, heads * HEAD_DIM), BF16),
        scratch_shapes=[pltpu.VMEM((6, S, rb), F32), pltpu.VMEM((6, S, rb), BF16)],
        compiler_params=pltpu.CompilerParams(
            dimension_semantics=("arbitrary", "arbitrary", "arbitrary"), vmem_limit_bytes=VMEM_LIMIT),
        name="fox_attn",
    )(proj, qa, proj, ka, vt)


def _chunk_kernel(q_ref, k_ref, vt_ref, g_ref, o_ref, bias_ref, *, tq, rbs, pp):
    i = pl.program_id(2)
    lane = lax.broadcasted_iota(jnp.int32, (1, LANES), 1)
    feat = lax.broadcasted_iota(jnp.int32, (PAIR, 1), 0)

    @pl.when((pl.program_id(1) == 0) & (i == 0))
    def _():
        k_chunk = lax.broadcasted_iota(jnp.int32, (tq, tq), 0) // CHUNK
        q_chunk = lax.broadcasted_iota(jnp.int32, (tq, tq), 1) // CHUNK
        for which in range(3):
            back = which * (tq // CHUNK) + q_chunk - k_chunk
            vis = (back >= 0) & (back <= LEFT_CHUNKS)
            for hd in range(2 * pp):
                g = jnp.broadcast_to(g_ref[0, which, hd], (tq, 2 * tq))
                toep = pltpu.roll(g, 0, 1, stride=1, stride_axis=0)[:, :tq]
                bias_ref[hd, (2 - which) * tq:(3 - which) * tq, :] = jnp.where(vis, toep, NEG)

    def attend(first_step):
        units = [(pi, blk, hh) for pi in range(pp) for blk in range(rbs) for hh in range(2)]

        def key_span(blk):
            if first_step:
                n_blocks = min(3, blk + 1)
                return (blk - (n_blocks - 1)) * tq, n_blocks * tq
            return pl.multiple_of((i * rbs + blk - 2) * tq, tq), 3 * tq

        def logits(pi, blk, hh):
            cols = slice(pi * PAIR, (pi + 1) * PAIR)
            ks, span = key_span(blk)
            q = q_ref[0, blk * tq:(blk + 1) * tq, cols]
            qh = jnp.where(_head_mask(lane, hh), q, jnp.zeros_like(q))
            return _dot_nt(k_ref[0, pl.ds(ks, span), cols], qh) + bias_ref[2 * pi + hh, 3 * tq - span:, :]

        def weighted_values(pi, blk, hh, p):
            ks, span = key_span(blk)
            return _dot(_value_aug_t(vt_ref[0, pi * PAIR:(pi + 1) * PAIR, pl.ds(ks, span)], feat, hh), p)

        ss, ps, accs = {}, {}, {}
        n_units = len(units)
        for t in range(0, n_units + 4, 2):
            for u in (t, t + 1):
                if u < n_units:
                    ss[u] = logits(*units[u])
            for u in (t - 2, t - 1):
                if 0 <= u < n_units:
                    ps[u] = _softmax_t([ss.pop(u)])[0]
            for u in (t - 4, t - 3):
                if 0 <= u < n_units:
                    accs[units[u]] = weighted_values(*units[u], ps.pop(u))
            if 0 <= t - 4 < n_units:
                pi, blk, _ = units[t - 4]
                o_ref[0, blk * tq:(blk + 1) * tq, pi * PAIR:(pi + 1) * PAIR] = _pair_output_t(
                    accs.pop((pi, blk, 0)), accs.pop((pi, blk, 1)), feat).astype(o_ref.dtype)

    pl.when(i == 0)(functools.partial(attend, True))
    pl.when(i > 0)(functools.partial(attend, False))


def _chunk_bias_rows(rel_table, tq, pp):
    heads = rel_table.shape[0]
    m = np.arange(2 * tq)
    offset = np.where(m < tq, m, m - 2 * tq)
    idx = np.stack([np.clip(which * tq + offset, -MAX_REL, MAX_REL) + MAX_REL for which in range(3)])
    rows = rel_table.astype(F32)[:, idx] * LOG2E
    return rows.reshape(heads // (2 * pp), 2 * pp, 3, 1, 2 * tq).transpose(0, 2, 1, 3, 4)


def _chunk(proj, vt, bias_rows, tq, rbs, pp, col0, row0):
    B, S, _ = proj.shape
    groups = bias_rows.shape[0]
    w = pp * PAIR
    assert tq % CHUNK == 0 and LEFT_CHUNKS * CHUNK <= 2 * tq, "band must fit in three key blocks"
    assert rbs >= 2, "after the first step every query block must have two key blocks behind it"
    return pl.pallas_call(
        functools.partial(_chunk_kernel, tq=tq, rbs=rbs, pp=pp),
        grid=(groups, B, S // (rbs * tq)),
        in_specs=[
            pl.BlockSpec((1, rbs * tq, w), lambda g, b, i: (b, i, col0 + g)),
            pl.BlockSpec((1, S, w), lambda g, b, i: (b, 0, col0 + groups + g)),
            pl.BlockSpec((1, w, S), lambda g, b, i: (b, row0 + g, 0)),
            pl.BlockSpec((1, 3, 2 * pp, 1, 2 * tq), lambda g, b, i: (g, 0, 0, 0, 0)),
        ],
        out_specs=pl.BlockSpec((1, rbs * tq, w), lambda g, b, i: (b, i, g)),
        out_shape=jax.ShapeDtypeStruct((B, S, groups * w), BF16),
        scratch_shapes=[pltpu.VMEM((2 * pp, 3 * tq, tq), F32)],
        compiler_params=pltpu.CompilerParams(
            dimension_semantics=("arbitrary", "arbitrary", "arbitrary"), vmem_limit_bytes=VMEM_LIMIT),
        name="chunk_attn",
    )(proj, proj, vt, bias_rows)


def _memkv_kernel(mem_ref, g_ref, wk_ref, wv_ref, k_ref, v_ref):
    mb = _rms(mem_ref[0], g_ref[...]).astype(BF16)
    k_ref[0] = _dot(mb, wk_ref[...]).astype(BF16)
    v_ref[0] = _dot(mb, wv_ref[...]).astype(BF16)


def _memkv(mem, g, wk, wv):
    B, M, D = mem.shape
    full = lambda b: (0, 0)
    return pl.pallas_call(
        _memkv_kernel,
        grid=(B,),
        in_specs=[
            pl.BlockSpec((1, M, D), lambda b: (b, 0, 0)),
            pl.BlockSpec((1, D), full),
            pl.BlockSpec((D, D), full),
            pl.BlockSpec((D, D), full),
        ],
        out_specs=[pl.BlockSpec((1, M, D), lambda b: (b, 0, 0))] * 2,
        out_shape=[jax.ShapeDtypeStruct((B, M, D), BF16)] * 2,
        compiler_params=pltpu.CompilerParams(
            dimension_semantics=("arbitrary",), vmem_limit_bytes=VMEM_LIMIT),
        name="mem_kv",
    )(mem, g, wk, wv)


def _mix_mem_kernel(yf_ref, yc_ref, x_ref, gf_ref, gc_ref, wo_ref, gpost_ref,
                    gpre_ref, wq_ref, mk_ref, mv_ref, wmo_ref, gmpost_ref, o_ref, *, sub):
    d_fox = yf_ref.shape[2]
    halves = range(x_ref.shape[1] // sub)
    rows = [slice(h * sub, (h + 1) * sub) for h in halves]

    def mix(r):
        yf = _rms(yf_ref[0, r, :].astype(F32), gf_ref[...]).astype(BF16)
        yc = _rms(yc_ref[0, r, :].astype(F32), gc_ref[...]).astype(BF16)
        return _dot(yf, wo_ref[0:d_fox, :]) + _dot(yc, wo_ref[d_fox:, :])

    def query(x1):
        return _dot(_rms(x1, gpre_ref[...]).astype(BF16), wq_ref[...]).astype(BF16)

    def attend(q):
        dh = q.shape[1] // MEM_HEADS
        outs = []
        for hh in range(MEM_HEADS):
            sl = slice(hh * dh, (hh + 1) * dh)
            s = _dot_nt(q[:, sl], mk_ref[0, :, sl])
            pr = jnp.exp2(s - jnp.max(s, axis=-1, keepdims=True))
            inv = 1.0 / jnp.sum(pr, axis=-1, keepdims=True)
            outs.append((_dot(pr.astype(BF16), mv_ref[0, :, sl]) * inv).astype(BF16))
        return jnp.concatenate(outs, axis=-1)

    ys = [mix(rows[h]) for h in halves]
    x1s = [x_ref[0, rows[h], :] + _rms(ys[h], gpost_ref[...]) for h in halves]
    qs = [query(x1s[h]) for h in halves]
    os = [attend(qs[h]) for h in halves]
    y2s = [_dot(os[h], wmo_ref[...]) for h in halves]
    for h in halves:
        o_ref[0, rows[h], :] = x1s[h] + _rms(y2s[h], gmpost_ref[...])


def _mix_mem(yf, yc, x, gf, gc, wo, gpost, gpre, wq, mk, mv, wmo, gmpost, tm, sub):
    B, S, D = x.shape
    M = mk.shape[1]
    dg = yf.shape[2]
    vec = lambda n: pl.BlockSpec((1, n), lambda b, i: (0, 0))
    mat = lambda r, c: pl.BlockSpec((r, c), lambda b, i: (0, 0), pipeline_mode=pl.Buffered(1))
    return pl.pallas_call(
        functools.partial(_mix_mem_kernel, sub=sub),
        grid=(B, S // tm),
        in_specs=[
            pl.BlockSpec((1, tm, dg), lambda b, i: (b, i, 0)),
            pl.BlockSpec((1, tm, dg), lambda b, i: (b, i, 0)),
            pl.BlockSpec((1, tm, D), lambda b, i: (b, i, 0)),
            vec(dg), vec(dg), mat(2 * dg, D), vec(D),
            vec(D), mat(D, D),
            pl.BlockSpec((1, M, D), lambda b, i: (b, 0, 0)),
            pl.BlockSpec((1, M, D), lambda b, i: (b, 0, 0)),
            mat(D, D), vec(D),
        ],
        out_specs=pl.BlockSpec((1, tm, D), lambda b, i: (b, i, 0)),
        out_shape=jax.ShapeDtypeStruct((B, S, D), F32),
        compiler_params=pltpu.CompilerParams(
            dimension_semantics=("arbitrary", "arbitrary"), vmem_limit_bytes=VMEM_LIMIT),
        name="mix_mem",
    )(yf, yc, x, gf, gc, wo, gpost, gpre, wq, mk, mv, wmo, gmpost)


def _mlp_kernel(x_ref, gpre_ref, w1_ref, w2_ref, gpost_ref, o_ref, *, ff_block, sub):
    halves = range(x_ref.shape[0] // sub)
    rows = [slice(h * sub, (h + 1) * sub) for h in halves]
    hs = [_rms(x_ref[rows[h], :], gpre_ref[...]).astype(BF16) for h in halves]
    ys = [None for _ in halves]
    for n in range(0, w1_ref.shape[1], ff_block):
        for h in halves:
            a = jnp.maximum(_dot(hs[h], w1_ref[:, n:n + ff_block]), 0.0)
            part = _dot((a * a).astype(BF16), w2_ref[n:n + ff_block, :])
            ys[h] = part if ys[h] is None else ys[h] + part
    for h in halves:
        o_ref[rows[h], :] = x_ref[rows[h], :] + _rms(ys[h], gpost_ref[...])


def _mlp(x, gpre, w1, w2, gpost, tm, ff_block, sub):
    R, D = x.shape
    d_ff = w1.shape[1]
    return pl.pallas_call(
        functools.partial(_mlp_kernel, ff_block=ff_block, sub=sub),
        grid=(R // tm,),
        in_specs=[
            pl.BlockSpec((tm, D), lambda i: (i, 0)),
            pl.BlockSpec((1, D), lambda i: (0, 0)),
            pl.BlockSpec((D, d_ff), lambda i: (0, 0), pipeline_mode=pl.Buffered(1)),
            pl.BlockSpec((d_ff, D), lambda i: (0, 0), pipeline_mode=pl.Buffered(1)),
            pl.BlockSpec((1, D), lambda i: (0, 0)),
        ],
        out_specs=pl.BlockSpec((tm, D), lambda i: (i, 0)),
        out_shape=jax.ShapeDtypeStruct((R, D), F32),
        compiler_params=pltpu.CompilerParams(
            dimension_semantics=("arbitrary",), vmem_limit_bytes=VMEM_LIMIT),
        name="mlp",
    )(x, gpre, w1, w2, gpost)


def kernel(x, mem, w_in, b_fgt, rel_bias, g_fox_out, g_chk_out, w_out, g_mix_pre, g_mix_post,
           g_mem_kv, w_mq, w_mk, w_mv, w_mo, g_mem_pre, g_mem_post,
           w_ff1, w_ff2, g_ff_pre, g_ff_post):
    B, S, D = x.shape
    depth = w_in.shape[0]
    heads = b_fgt.shape[1]
    d_fox = heads * HEAD_DIM
    d_chk = rel_bias.shape[1] * HEAD_DIM
    assert d_fox == d_chk and heads % 2 == 0 and heads <= LANES
    pairs = heads // 2
    row = lambda v: v.reshape(1, -1)

    for l in range(depth):
        scale = HEAD_DIM ** -0.5 * LOG2E
        wl = w_in[l]
        o_f = 3 * d_fox
        w_f = wl[:, o_f:o_f + heads]
        o_c = o_f + heads
        w_all = jnp.concatenate([
            wl[:, :d_fox] * scale, wl[:, d_fox:2 * d_fox],
            wl[:, o_c:o_c + d_chk] * scale, wl[:, o_c + d_chk:o_c + 2 * d_chk],
            jnp.pad(jnp.tile(w_f, (1, PIECES)), ((0, 0), (0, LANES - PIECES * heads))),
        ], axis=1).astype(BF16)
        wv_t = jnp.concatenate([wl[:, 2 * d_fox:o_f], wl[:, o_c + 2 * d_chk:]], axis=1).T.astype(BF16)
        b_f = jnp.pad(jnp.tile(b_fgt[l], PIECES), (0, LANES - PIECES * heads)).reshape(1, LANES)

        proj, vt, qa, ka = _inproj(x, row(g_mix_pre[l]), w_all, wv_t, b_f, heads, tm=1024, sub=256)
        fox_pp, chk_pp = 4, 4
        y_fox = _fox(proj, vt, qa, ka, heads, tq=512, rb=256, pp=fox_pp)
        y_chk = _chunk(proj, vt, _chunk_bias_rows(rel_bias[l], 256, chk_pp), tq=256, rbs=2, pp=chk_pp,
                       col0=2 * pairs // chk_pp, row0=pairs // chk_pp)

        mk, mv = _memkv(mem, row(g_mem_kv[l]), w_mk[l].astype(BF16), w_mv[l].astype(BF16))
        mem_scale = (D // MEM_HEADS) ** -0.5 * LOG2E
        x = _mix_mem(y_fox, y_chk, x, row(g_fox_out[l]), row(g_chk_out[l]), w_out[l].astype(BF16),
                     row(g_mix_post[l]), row(g_mem_pre[l]), (w_mq[l] * mem_scale).astype(BF16),
                     mk, mv, w_mo[l].astype(BF16), row(g_mem_post[l]), tm=1024, sub=256)
        x = _mlp(x.reshape(B * S, D), row(g_ff_pre[l]), w_ff1[l].astype(BF16), w_ff2[l].astype(BF16),
                 row(g_ff_post[l]), tm=1024, ff_block=1024, sub=256).reshape(B, S, D)
    return x
```

```python
import functools
import math

import numpy as np
import jax
import jax.numpy as jnp
from jax import lax
from jax.experimental import pallas as pl
from jax.experimental.pallas import tpu as pltpu

F32 = jnp.float32
BF16 = jnp.bfloat16

EPS = 1e-6
HEAD_DIM = 64
PAIR = 2 * HEAD_DIM
CHUNK = 64
LEFT_CHUNKS = 8
MAX_REL = 128
MEM_HEADS = 4
NEG = -1e30
LOG2E = math.log2(math.e)
LANES = 128
VMEM_LIMIT = 56 * 1024 * 1024
PIECES = 3
SUM_LANE = (HEAD_DIM, 0)


def _rms(x, g):
    return x * lax.rsqrt(jnp.mean(x * x, axis=-1, keepdims=True) + EPS) * g


def _dot(a, b):
    return jnp.dot(a, b, preferred_element_type=F32)


def _dot_nt(a, b):
    return lax.dot_general(a, b, (((1,), (1,)), ((), ())), preferred_element_type=F32)


def _split3(x):
    hi = x.astype(BF16)
    r1 = x - hi.astype(F32)
    mid = r1.astype(BF16)
    lo = (r1 - mid.astype(F32)).astype(BF16)
    return hi, mid, lo


def _inproj_kernel(x_ref, g_ref, w_ref, wvt_ref, bf_ref, tri_ref, e_ref, ones_ref,
                   proj_ref, vt_ref, qa_ref, ka_ref, carry_ref, *, heads, sub):
    @pl.when(pl.program_id(1) == 0)
    def _():
        carry_ref[...] = jnp.zeros_like(carry_ref)

    n_main = proj_ref.shape[2]
    halves = range(x_ref.shape[1] // sub)
    rows = [slice(h * sub, (h + 1) * sub) for h in halves]
    lane = lax.broadcasted_iota(jnp.int32, (1, LANES), 1)
    hbs = [_rms(x_ref[0, rows[h], :], g_ref[...]).astype(BF16) for h in halves]
    fs = []
    for h in halves:
        res = _dot(hbs[h], w_ref[...])
        proj_ref[0, rows[h], :] = res[:, :n_main].astype(BF16)
        vt_ref[0, :, rows[h]] = _dot_nt(wvt_ref[...], hbs[h]).astype(BF16)
        fs.append(res[:, n_main:] + bf_ref[...])
    carry = carry_ref[...]
    for h in halves:
        f = fs[h]
        logf = jnp.minimum(f, 0.0) - jnp.log1p(jnp.exp(-jnp.abs(f)))
        hi, mid, lo = _split3(logf)
        packed = jnp.where(lane < heads, hi, jnp.where(lane < 2 * heads, mid, lo))
        cum = _dot(tri_ref[...], packed)
        c = (cum + pltpu.roll(cum, LANES - heads, 1) + pltpu.roll(cum, LANES - 2 * heads, 1)) + carry
        carry = c[sub - 1:sub, :]
        pieces = jnp.concatenate(_split3(c * LOG2E), axis=-1)
        sel = _dot(pieces, e_ref[...])
        qa_ref[0, rows[h], :] = (sel[:, :LANES] + ones_ref[0:1, :]).astype(BF16)
        ka_ref[0, rows[h], :] = (ones_ref[1:2, :] - sel[:, LANES:]).astype(BF16)
    carry_ref[...] = carry


def _aug_constants(heads):
    k0 = PIECES * heads
    assert 2 * k0 <= LANES
    e = np.zeros((PIECES * LANES, 2 * LANES), np.float32)
    for h in range(heads):
        for r in range(PIECES):
            e[r * LANES + h, PIECES * h + r] = 1.0
            e[r * LANES + h, LANES + k0 + PIECES * h + r] = 1.0
    ones = np.zeros((2, LANES), np.float32)
    ones[0, k0:2 * k0] = 1.0
    ones[1, 0:k0] = 1.0
    return jnp.asarray(e, BF16), jnp.asarray(ones)


def _inproj(x, g, w_all, wv_t, b_f, heads, tm, sub):
    B, S, D = x.shape
    n_main = w_all.shape[1] - LANES
    n_val = wv_t.shape[0]
    tri = jnp.asarray(np.tril(np.ones((sub, sub), np.float32)), BF16)
    e, ones = _aug_constants(heads)
    const = lambda r, c: pl.BlockSpec((r, c), lambda b, i: (0, 0), pipeline_mode=pl.Buffered(1))
    return pl.pallas_call(
        functools.partial(_inproj_kernel, heads=heads, sub=sub),
        grid=(B, S // tm),
        in_specs=[
            pl.BlockSpec((1, tm, D), lambda b, i: (b, i, 0)),
            const(1, D), const(D, n_main + LANES), const(n_val, D), const(1, LANES), const(sub, sub),
            const(PIECES * LANES, 2 * LANES), const(2, LANES),
        ],
        out_specs=[
            pl.BlockSpec((1, tm, n_main), lambda b, i: (b, i, 0)),
            pl.BlockSpec((1, n_val, tm), lambda b, i: (b, 0, i)),
            pl.BlockSpec((1, tm, LANES), lambda b, i: (b, i, 0)),
            pl.BlockSpec((1, tm, LANES), lambda b, i: (b, i, 0)),
        ],
        out_shape=[
            jax.ShapeDtypeStruct((B, S, n_main), BF16),
            jax.ShapeDtypeStruct((B, n_val, S), BF16),
            jax.ShapeDtypeStruct((B, S, LANES), BF16),
            jax.ShapeDtypeStruct((B, S, LANES), BF16),
        ],
        scratch_shapes=[pltpu.VMEM((1, LANES), F32)],
        compiler_params=pltpu.CompilerParams(
            dimension_semantics=("arbitrary", "arbitrary"), vmem_limit_bytes=VMEM_LIMIT),
        name="inproj",
    )(x, g, w_all, wv_t, b_f, tri, e, ones)


def _head_mask(lane, hh):
    return lane < HEAD_DIM if hh == 0 else lane >= HEAD_DIM


def _value_aug_t(vt, feat, hh):
    ones = jnp.where(feat == SUM_LANE[hh], 1.0, 0.0).astype(vt.dtype)
    return jnp.where(_head_mask(feat, hh), vt, ones)


def _softmax_t(ss):
    m = functools.reduce(jnp.maximum, [jnp.max(s, axis=0, keepdims=True) for s in ss])
    return [jnp.exp2(s - m).astype(BF16) for s in ss]


def _pair_output_t(acc0, acc1, feat):
    l0 = acc0[SUM_LANE[0]:SUM_LANE[0] + 1, :]
    l1 = acc1[SUM_LANE[1]:SUM_LANE[1] + 1, :]
    return jnp.where(_head_mask(feat, 0), acc0 * (1.0 / l0), acc1 * (1.0 / l1)).T


def _fox_kernel(q_ref, qa_ref, k_ref, ka_ref, vt_ref, o_ref, s_ref, p_ref, *, tq, rb, pp, heads):
    grp = pl.program_id(1)
    lane = lax.broadcasted_iota(jnp.int32, (1, LANES), 1)
    feat = lax.broadcasted_iota(jnp.int32, (PAIR, 1), 0)
    k0 = PIECES * heads
    causal = lax.broadcasted_iota(jnp.int32, (rb, rb), 0) <= lax.broadcasted_iota(jnp.int32, (rb, rb), 1)
    slots = s_ref.shape[0]

    def attend(n):
        units = [(pi, r, hh) for pi in range(pp) for r in range(tq // rb) for hh in range(2)]

        def keys(r):
            past = n * tq + r * rb
            return past, past + rb

        def logits(u):
            pi, r, hh = units[u]
            past, total = keys(r)
            q = q_ref[0, r * rb:(r + 1) * rb, pi * PAIR:(pi + 1) * PAIR]
            qa = qa_ref[0, r * rb:(r + 1) * rb, :]
            a0 = PIECES * (2 * (grp * pp + pi) + hh)
            amask = ((lane >= a0) & (lane < a0 + PIECES)) | ((lane >= k0 + a0) & (lane < k0 + a0 + PIECES))
            q_aug = jnp.concatenate([jnp.where(_head_mask(lane, hh), q, jnp.zeros_like(q)),
                                     jnp.where(amask, qa, jnp.zeros_like(qa))], axis=-1)
            m = None
            for lo, hi in ([(0, past)] if past else []) + [(past, total)]:
                kaug = jnp.concatenate([k_ref[0, lo:hi, pi * PAIR:(pi + 1) * PAIR], ka_ref[0, lo:hi, :]], axis=-1)
                s = _dot_nt(kaug, q_aug)
                if lo == past:
                    s = jnp.where(causal, s, NEG)
                s_ref[u % slots, lo:hi, :] = s
                col_max = jnp.max(s, axis=0, keepdims=True)
                m = col_max if m is None else jnp.maximum(m, col_max)
            return m

        def weights(u, m):
            _, total = keys(units[u][1])
            p_ref[u % slots, 0:total, :] = jnp.exp2((s_ref[u % slots, 0:total, :] - m).astype(BF16))

        def values(u):
            pi, r, hh = units[u]
            _, total = keys(r)
            return _dot(_value_aug_t(vt_ref[0, pi * PAIR:(pi + 1) * PAIR, 0:total], feat, hh),
                        p_ref[u % slots, 0:total, :])

        ms, accs = {}, {}
        n_units = len(units)
        for t in range(0, n_units + 4, 2):
            for u in (t, t + 1):
                if u < n_units:
                    ms[u] = logits(u)
            for u in (t - 2, t - 1):
                if 0 <= u < n_units:
                    weights(u, ms.pop(u))
            for u in (t - 4, t - 3):
                if 0 <= u < n_units:
                    accs[units[u]] = values(u)
            if 0 <= t - 4 < n_units:
                pi, r, _ = units[t - 4]
                o_ref[0, r * rb:(r + 1) * rb, pi * PAIR:(pi + 1) * PAIR] = _pair_output_t(
                    accs.pop((pi, r, 0)), accs.pop((pi, r, 1)), feat).astype(o_ref.dtype)

    for n in range(k_ref.shape[1] // tq):
        pl.when(pl.program_id(2) == n)(functools.partial(attend, n))


def _fox(proj, vt, qa, ka, heads, tq, rb, pp):
    B, S, _ = proj.shape
    groups = heads // 2 // pp
    w = pp * PAIR
    return pl.pallas_call(
        functools.partial(_fox_kernel, tq=tq, rb=rb, pp=pp, heads=heads),
        grid=(B, groups, S // tq),
        in_specs=[
            pl.BlockSpec((1, tq, w), lambda b, g, i: (b, i, g)),
            pl.BlockSpec((1, tq, LANES), lambda b, g, i: (b, i, 0)),
            pl.BlockSpec((1, S, w), lambda b, g, i: (b, 0, groups + g)),
            pl.BlockSpec((1, S, LANES), lambda b, g, i: (b, 0, 0)),
            pl.BlockSpec((1, w, S), lambda b, g, i: (b, g, 0)),
        ],
        out_specs=pl.BlockSpec((1, tq, w), lambda b, g, i: (b, i, g)),
        out_shape=jax.ShapeDtypeStruct((B, S, heads * HEAD_DIM), BF16),
        scratch_shapes=[pltpu.VMEM((6, S, rb), F32), pltpu.VMEM((6, S, rb), BF16)],
        compiler_params=pltpu.CompilerParams(
            dimension_semantics=("arbitrary", "arbitrary", "arbitrary"), vmem_limit_bytes=VMEM_LIMIT),
        name="fox_attn",
    )(proj, qa, proj, ka, vt)


def _chunk_kernel(q_ref, k_ref, vt_ref, g_ref, o_ref, bias_ref, *, tq, rbs, pp):
    i = pl.program_id(2)
    lane = lax.broadcasted_iota(jnp.int32, (1, LANES), 1)
    feat = lax.broadcasted_iota(jnp.int32, (PAIR, 1), 0)

    @pl.when((pl.program_id(1) == 0) & (i == 0))
    def _():
        k_chunk = lax.broadcasted_iota(jnp.int32, (tq, tq), 0) // CHUNK
        q_chunk = lax.broadcasted_iota(jnp.int32, (tq, tq), 1) // CHUNK
        for which in range(3):
            back = which * (tq // CHUNK) + q_chunk - k_chunk
            vis = (back >= 0) & (back <= LEFT_CHUNKS)
            for hd in range(2 * pp):
                g = jnp.broadcast_to(g_ref[0, which, hd], (tq, 2 * tq))
                toep = pltpu.roll(g, 0, 1, stride=1, stride_axis=0)[:, :tq]
                bias_ref[hd, (2 - which) * tq:(3 - which) * tq, :] = jnp.where(vis, toep, NEG)

    def attend(first_step):
        units = [(pi, blk, hh) for pi in range(pp) for blk in range(rbs) for hh in range(2)]

        def key_span(blk):
            if first_step:
                n_blocks = min(3, blk + 1)
                return (blk - (n_blocks - 1)) * tq, n_blocks * tq
            return pl.multiple_of((i * rbs + blk - 2) * tq, tq), 3 * tq

        def logits(pi, blk, hh):
            cols = slice(pi * PAIR, (pi + 1) * PAIR)
            ks, span = key_span(blk)
            q = q_ref[0, blk * tq:(blk + 1) * tq, cols]
            qh = jnp.where(_head_mask(lane, hh), q, jnp.zeros_like(q))
            return _dot_nt(k_ref[0, pl.ds(ks, span), cols], qh) + bias_ref[2 * pi + hh, 3 * tq - span:, :]

        def weighted_values(pi, blk, hh, p):
            ks, span = key_span(blk)
            return _dot(_value_aug_t(vt_ref[0, pi * PAIR:(pi + 1) * PAIR, pl.ds(ks, span)], feat, hh), p)

        ss, ps, accs = {}, {}, {}
        n_units = len(units)
        for t in range(0, n_units + 4, 2):
            for u in (t, t + 1):
                if u < n_units:
                    ss[u] = logits(*units[u])
            for u in (t - 2, t - 1):
                if 0 <= u < n_units:
                    ps[u] = _softmax_t([ss.pop(u)])[0]
            for u in (t - 4, t - 3):
                if 0 <= u < n_units:
                    accs[units[u]] = weighted_values(*units[u], ps.pop(u))
            if 0 <= t - 4 < n_units:
                pi, blk, _ = units[t - 4]
                o_ref[0, blk * tq:(blk + 1) * tq, pi * PAIR:(pi + 1) * PAIR] = _pair_output_t(
                    accs.pop((pi, blk, 0)), accs.pop((pi, blk, 1)), feat).astype(o_ref.dtype)

    pl.when(i == 0)(functools.partial(attend, True))
    pl.when(i > 0)(functools.partial(attend, False))


def _chunk_bias_rows(rel_table, tq, pp):
    heads = rel_table.shape[0]
    m = np.arange(2 * tq)
    offset = np.where(m < tq, m, m - 2 * tq)
    idx = np.stack([np.clip(which * tq + offset, -MAX_REL, MAX_REL) + MAX_REL for which in range(3)])
    rows = rel_table.astype(F32)[:, idx] * LOG2E
    return rows.reshape(heads // (2 * pp), 2 * pp, 3, 1, 2 * tq).transpose(0, 2, 1, 3, 4)


def _chunk(proj, vt, bias_rows, tq, rbs, pp, col0, row0):
    B, S, _ = proj.shape
    groups = bias_rows.shape[0]
    w = pp * PAIR
    assert tq % CHUNK == 0 and LEFT_CHUNKS * CHUNK <= 2 * tq, "band must fit in three key blocks"
    assert rbs >= 2, "after the first step every query block must have two key blocks behind it"
    return pl.pallas_call(
        functools.partial(_chunk_kernel, tq=tq, rbs=rbs, pp=pp),
        grid=(groups, B, S // (rbs * tq)),
        in_specs=[
            pl.BlockSpec((1, rbs * tq, w), lambda g, b, i: (b, i, col0 + g)),
            pl.BlockSpec((1, S, w), lambda g, b, i: (b, 0, col0 + groups + g)),
            pl.BlockSpec((1, w, S), lambda g, b, i: (b, row0 + g, 0)),
            pl.BlockSpec((1, 3, 2 * pp, 1, 2 * tq), lambda g, b, i: (g, 0, 0, 0, 0)),
        ],
        out_specs=pl.BlockSpec((1, rbs * tq, w), lambda g, b, i: (b, i, g)),
        out_shape=jax.ShapeDtypeStruct((B, S, groups * w), BF16),
        scratch_shapes=[pltpu.VMEM((2 * pp, 3 * tq, tq), F32)],
        compiler_params=pltpu.CompilerParams(
            dimension_semantics=("arbitrary", "arbitrary", "arbitrary"), vmem_limit_bytes=VMEM_LIMIT),
        name="chunk_attn",
    )(proj, proj, vt, bias_rows)


def _memkv_kernel(mem_ref, g_ref, wk_ref, wv_ref, k_ref, v_ref):
    mb = _rms(mem_ref[0], g_ref[...]).astype(BF16)
    k_ref[0] = _dot(mb, wk_ref[...]).astype(BF16)
    v_ref[0] = _dot(mb, wv_ref[...]).astype(BF16)


def _memkv(mem, g, wk, wv):
    B, M, D = mem.shape
    full = lambda b: (0, 0)
    return pl.pallas_call(
        _memkv_kernel,
        grid=(B,),
        in_specs=[
            pl.BlockSpec((1, M, D), lambda b: (b, 0, 0)),
            pl.BlockSpec((1, D), full),
            pl.BlockSpec((D, D), full),
            pl.BlockSpec((D, D), full),
        ],
        out_specs=[pl.BlockSpec((1, M, D), lambda b: (b, 0, 0))] * 2,
        out_shape=[jax.ShapeDtypeStruct((B, M, D), BF16)] * 2,
        compiler_params=pltpu.CompilerParams(
            dimension_semantics=("arbitrary",), vmem_limit_bytes=VMEM_LIMIT),
        name="mem_kv",
    )(mem, g, wk, wv)


def _mix_mem_kernel(yf_ref, yc_ref, x_ref, gf_ref, gc_ref, wo_ref, gpost_ref,
                    gpre_ref, wq_ref, mk_ref, mv_ref, wmo_ref, gmpost_ref, o_ref, *, sub):
    d_fox = yf_ref.shape[2]
    halves = range(x_ref.shape[1] // sub)
    rows = [slice(h * sub, (h + 1) * sub) for h in halves]

    def mix(r):
        yf = _rms(yf_ref[0, r, :].astype(F32), gf_ref[...]).astype(BF16)
        yc = _rms(yc_ref[0, r, :].astype(F32), gc_ref[...]).astype(BF16)
        return _dot(yf, wo_ref[0:d_fox, :]) + _dot(yc, wo_ref[d_fox:, :])

    def query(x1):
        return _dot(_rms(x1, gpre_ref[...]).astype(BF16), wq_ref[...]).astype(BF16)

    def attend(q):
        dh = q.shape[1] // MEM_HEADS
        outs = []
        for hh in range(MEM_HEADS):
            sl = slice(hh * dh, (hh + 1) * dh)
            s = _dot_nt(q[:, sl], mk_ref[0, :, sl])
            pr = jnp.exp2(s - jnp.max(s, axis=-1, keepdims=True))
            inv = 1.0 / jnp.sum(pr, axis=-1, keepdims=True)
            outs.append((_dot(pr.astype(BF16), mv_ref[0, :, sl]) * inv).astype(BF16))
        return jnp.concatenate(outs, axis=-1)

    ys = [mix(rows[h]) for h in halves]
    x1s = [x_ref[0, rows[h], :] + _rms(ys[h], gpost_ref[...]) for h in halves]
    qs = [query(x1s[h]) for h in halves]
    os = [attend(qs[h]) for h in halves]
    y2s = [_dot(os[h], wmo_ref[...]) for h in halves]
    for h in halves:
        o_ref[0, rows[h], :] = x1s[h] + _rms(y2s[h], gmpost_ref[...])


def _mix_mem(yf, yc, x, gf, gc, wo, gpost, gpre, wq, mk, mv, wmo, gmpost, tm, sub):
    B, S, D = x.shape
    M = mk.shape[1]
    dg = yf.shape[2]
    vec = lambda n: pl.BlockSpec((1, n), lambda b, i: (0, 0))
    mat = lambda r, c: pl.BlockSpec((r, c), lambda b, i: (0, 0), pipeline_mode=pl.Buffered(1))
    return pl.pallas_call(
        functools.partial(_mix_mem_kernel, sub=sub),
        grid=(B, S // tm),
        in_specs=[
            pl.BlockSpec((1, tm, dg), lambda b, i: (b, i, 0)),
            pl.BlockSpec((1, tm, dg), lambda b, i: (b, i, 0)),
            pl.BlockSpec((1, tm, D), lambda b, i: (b, i, 0)),
            vec(dg), vec(dg), mat(2 * dg, D), vec(D),
            vec(D), mat(D, D),
            pl.BlockSpec((1, M, D), lambda b, i: (b, 0, 0)),
            pl.BlockSpec((1, M, D), lambda b, i: (b, 0, 0)),
            mat(D, D), vec(D),
        ],
        out_specs=pl.BlockSpec((1, tm, D), lambda b, i: (b, i, 0)),
        out_shape=jax.ShapeDtypeStruct((B, S, D), F32),
        compiler_params=pltpu.CompilerParams(
            dimension_semantics=("arbitrary", "arbitrary"), vmem_limit_bytes=VMEM_LIMIT),
        name="mix_mem",
    )(yf, yc, x, gf, gc, wo, gpost, gpre, wq, mk, mv, wmo, gmpost)


def _mlp_kernel(x_ref, gpre_ref, w1_ref, w2_ref, gpost_ref, o_ref, *, ff_block, sub):
    halves = range(x_ref.shape[0] // sub)
    rows = [slice(h * sub, (h + 1) * sub) for h in halves]
    hs = [_rms(x_ref[rows[h], :], gpre_ref[...]).astype(BF16) for h in halves]
    ys = [None for _ in halves]
    for n in range(0, w1_ref.shape[1], ff_block):
        for h in halves:
            a = jnp.maximum(_dot(hs[h], w1_ref[:, n:n + ff_block]), 0.0)
            part = _dot((a * a).astype(BF16), w2_ref[n:n + ff_block, :])
            ys[h] = part if ys[h] is None else ys[h] + part
    for h in halves:
        o_ref[rows[h], :] = x_ref[rows[h], :] + _rms(ys[h], gpost_ref[...])


def _mlp(x, gpre, w1, w2, gpost, tm, ff_block, sub):
    R, D = x.shape
    d_ff = w1.shape[1]
    return pl.pallas_call(
        functools.partial(_mlp_kernel, ff_block=ff_block, sub=sub),
        grid=(R // tm,),
        in_specs=[
            pl.BlockSpec((tm, D), lambda i: (i, 0)),
            pl.BlockSpec((1, D), lambda i: (0, 0)),
            pl.BlockSpec((D, d_ff), lambda i: (0, 0), pipeline_mode=pl.Buffered(1)),
            pl.BlockSpec((d_ff, D), lambda i: (0, 0), pipeline_mode=pl.Buffered(1)),
            pl.BlockSpec((1, D), lambda i: (0, 0)),
        ],
        out_specs=pl.BlockSpec((tm, D), lambda i: (i, 0)),
        out_shape=jax.ShapeDtypeStruct((R, D), F32),
        compiler_params=pltpu.CompilerParams(
            dimension_semantics=("arbitrary",), vmem_limit_bytes=VMEM_LIMIT),
        name="mlp",
    )(x, gpre, w1, w2, gpost)


def kernel(x, mem, w_in, b_fgt, rel_bias, g_fox_out, g_chk_out, w_out, g_mix_pre, g_mix_post,
           g_mem_kv, w_mq, w_mk, w_mv, w_mo, g_mem_pre, g_mem_post,
           w_ff1, w_ff2, g_ff_pre, g_ff_post):
    B, S, D = x.shape
    depth = w_in.shape[0]
    heads = b_fgt.shape[1]
    d_fox = heads * HEAD_DIM
    d_chk = rel_bias.shape[1] * HEAD_DIM
    assert d_fox == d_chk and heads % 2 == 0 and heads <= LANES
    pairs = heads // 2
    row = lambda v: v.reshape(1, -1)

    for l in range(depth):
        scale = HEAD_DIM ** -0.5 * LOG2E
        wl = w_in[l]
        o_f = 3 * d_fox
        w_f = wl[:, o_f:o_f + heads]
        o_c = o_f + heads
        w_all = jnp.concatenate([
            wl[:, :d_fox] * scale, wl[:, d_fox:2 * d_fox],
            wl[:, o_c:o_c + d_chk] * scale, wl[:, o_c + d_chk:o_c + 2 * d_chk],
            jnp.pad(jnp.tile(w_f, (1, PIECES)), ((0, 0), (0, LANES - PIECES * heads))),
        ], axis=1).astype(BF16)
        wv_t = jnp.concatenate([wl[:, 2 * d_fox:o_f], wl[:, o_c + 2 * d_chk:]], axis=1).T.astype(BF16)
        b_f = jnp.pad(jnp.tile(b_fgt[l], PIECES), (0, LANES - PIECES * heads)).reshape(1, LANES)

        proj, vt, qa, ka = _inproj(x, row(g_mix_pre[l]), w_all, wv_t, b_f, heads, tm=1024, sub=256)
        fox_pp, chk_pp = 4, 4
        y_fox = _fox(proj, vt, qa, ka, heads, tq=512, rb=256, pp=fox_pp)
        y_chk = _chunk(proj, vt, _chunk_bias_rows(rel_bias[l], 256, chk_pp), tq=256, rbs=2, pp=chk_pp,
                       col0=2 * pairs // chk_pp, row0=pairs // chk_pp)

        mk, mv = _memkv(mem, row(g_mem_kv[l]), w_mk[l].astype(BF16), w_mv[l].astype(BF16))
        mem_scale = (D // MEM_HEADS) ** -0.5 * LOG2E
        x = _mix_mem(y_fox, y_chk, x, row(g_fox_out[l]), row(g_chk_out[l]), w_out[l].astype(BF16),
                     row(g_mix_post[l]), row(g_mem_pre[l]), (w_mq[l] * mem_scale).astype(BF16),
                     mk, mv, w_mo[l].astype(BF16), row(g_mem_post[l]), tm=1024, sub=256)
        x = _mlp(x.reshape(B * S, D), row(g_ff_pre[l]), w_ff1[l].astype(BF16), w_ff2[l].astype(BF16),
                 row(g_ff_post[l]), tm=1024, ff_block=1024, sub=256).reshape(B, S, D)
    return x
```

```python
import functools
import math

import numpy as np
import jax
import jax.numpy as jnp
from jax import lax
from jax.experimental import pallas as pl
from jax.experimental.pallas import tpu as pltpu

F32 = jnp.float32
BF16 = jnp.bfloat16

EPS = 1e-6
HEAD_DIM = 64
PAIR = 2 * HEAD_DIM
CHUNK = 64
LEFT_CHUNKS = 8
MAX_REL = 128
MEM_HEADS = 4
NEG = -1e30
LOG2E = math.log2(math.e)
LANES = 128
VMEM_LIMIT = 56 * 1024 * 1024
PIECES = 3
SUM_LANE = (HEAD_DIM, 0)


def _rms(x, g):
    return x * lax.rsqrt(jnp.mean(x * x, axis=-1, keepdims=True) + EPS) * g


def _dot(a, b):
    return jnp.dot(a, b, preferred_element_type=F32)


def _dot_nt(a, b):
    return lax.dot_general(a, b, (((1,), (1,)), ((), ())), preferred_element_type=F32)


def _split3(x):
    hi = x.astype(BF16)
    r1 = x - hi.astype(F32)
    mid = r1.astype(BF16)
    lo = (r1 - mid.astype(F32)).astype(BF16)
    return hi, mid, lo


def _inproj_kernel(x_ref, g_ref, w_ref, wvt_ref, bf_ref, tri_ref, e_ref, ones_ref,
                   proj_ref, vt_ref, qa_ref, ka_ref, carry_ref, *, heads, sub):
    @pl.when(pl.program_id(1) == 0)
    def _():
        carry_ref[...] = jnp.zeros_like(carry_ref)

    n_main = proj_ref.shape[2]
    halves = range(x_ref.shape[1] // sub)
    rows = [slice(h * sub, (h + 1) * sub) for h in halves]
    lane = lax.broadcasted_iota(jnp.int32, (1, LANES), 1)
    hbs = [_rms(x_ref[0, rows[h], :], g_ref[...]).astype(BF16) for h in halves]
    fs = []
    for h in halves:
        res = _dot(hbs[h], w_ref[...])
        proj_ref[0, rows[h], :] = res[:, :n_main].astype(BF16)
        vt_ref[0, :, rows[h]] = _dot_nt(wvt_ref[...], hbs[h]).astype(BF16)
        fs.append(res[:, n_main:] + bf_ref[...])
    carry = carry_ref[...]
    for h in halves:
        f = fs[h]
        logf = jnp.minimum(f, 0.0) - jnp.log1p(jnp.exp(-jnp.abs(f)))
        hi, mid, lo = _split3(logf)
        packed = jnp.where(lane < heads, hi, jnp.where(lane < 2 * heads, mid, lo))
        cum = _dot(tri_ref[...], packed)
        c = (cum + pltpu.roll(cum, LANES - heads, 1) + pltpu.roll(cum, LANES - 2 * heads, 1)) + carry
        carry = c[sub - 1:sub, :]
        pieces = jnp.concatenate(_split3(c * LOG2E), axis=-1)
        sel = _dot(pieces, e_ref[...])
        qa_ref[0, rows[h], :] = (sel[:, :LANES] + ones_ref[0:1, :]).astype(BF16)
        ka_ref[0, rows[h], :] = (ones_ref[1:2, :] - sel[:, LANES:]).astype(BF16)
    carry_ref[...] = carry


def _aug_constants(heads):
    k0 = PIECES * heads
    assert 2 * k0 <= LANES
    e = np.zeros((PIECES * LANES, 2 * LANES), np.float32)
    for h in range(heads):
        for r in range(PIECES):
            e[r * LANES + h, PIECES * h + r] = 1.0
            e[r * LANES + h, LANES + k0 + PIECES * h + r] = 1.0
    ones = np.zeros((2, LANES), np.float32)
    ones[0, k0:2 * k0] = 1.0
    ones[1, 0:k0] = 1.0
    return jnp.asarray(e, BF16), jnp.asarray(ones)


def _inproj(x, g, w_all, wv_t, b_f, heads, tm, sub):
    B, S, D = x.shape
    n_main = w_all.shape[1] - LANES
    n_val = wv_t.shape[0]
    tri = jnp.asarray(np.tril(np.ones((sub, sub), np.float32)), BF16)
    e, ones = _aug_constants(heads)
    const = lambda r, c: pl.BlockSpec((r, c), lambda b, i: (0, 0), pipeline_mode=pl.Buffered(1))
    return pl.pallas_call(
        functools.partial(_inproj_kernel, heads=heads, sub=sub),
        grid=(B, S // tm),
        in_specs=[
            pl.BlockSpec((1, tm, D), lambda b, i: (b, i, 0)),
            const(1, D), const(D, n_main + LANES), const(n_val, D), const(1, LANES), const(sub, sub),
            const(PIECES * LANES, 2 * LANES), const(2, LANES),
        ],
        out_specs=[
            pl.BlockSpec((1, tm, n_main), lambda b, i: (b, i, 0)),
            pl.BlockSpec((1, n_val, tm), lambda b, i: (b, 0, i)),
            pl.BlockSpec((1, tm, LANES), lambda b, i: (b, i, 0)),
            pl.BlockSpec((1, tm, LANES), lambda b, i: (b, i, 0)),
        ],
        out_shape=[
            jax.ShapeDtypeStruct((B, S, n_main), BF16),
            jax.ShapeDtypeStruct((B, n_val, S), BF16),
            jax.ShapeDtypeStruct((B, S, LANES), BF16),
            jax.ShapeDtypeStruct((B, S, LANES), BF16),
        ],
        scratch_shapes=[pltpu.VMEM((1, LANES), F32)],
        compiler_params=pltpu.CompilerParams(
            dimension_semantics=("arbitrary", "arbitrary"), vmem_limit_bytes=VMEM_LIMIT),
        name="inproj",
    )(x, g, w_all, wv_t, b_f, tri, e, ones)


def _head_mask(lane, hh):
    return lane < HEAD_DIM if hh == 0 else lane >= HEAD_DIM


def _value_aug_t(vt, feat, hh):
    ones = jnp.where(feat == SUM_LANE[hh], 1.0, 0.0).astype(vt.dtype)
    return jnp.where(_head_mask(feat, hh), vt, ones)


def _softmax_t(ss):
    m = functools.reduce(jnp.maximum, [jnp.max(s, axis=0, keepdims=True) for s in ss])
    return [jnp.exp2(s - m).astype(BF16) for s in ss]


def _pair_output_t(acc0, acc1, feat):
    l0 = acc0[SUM_LANE[0]:SUM_LANE[0] + 1, :]
    l1 = acc1[SUM_LANE[1]:SUM_LANE[1] + 1, :]
    return jnp.where(_head_mask(feat, 0), acc0 * (1.0 / l0), acc1 * (1.0 / l1)).T


def _fox_kernel(q_ref, qa_ref, k_ref, ka_ref, vt_ref, o_ref, s_ref, p_ref, *, tq, rb, pp, heads):
    grp = pl.program_id(1)
    lane = lax.broadcasted_iota(jnp.int32, (1, LANES), 1)
    feat = lax.broadcasted_iota(jnp.int32, (PAIR, 1), 0)
    k0 = PIECES * heads
    causal = lax.broadcasted_iota(jnp.int32, (rb, rb), 0) <= lax.broadcasted_iota(jnp.int32, (rb, rb), 1)
    slots = s_ref.shape[0]

    def attend(n):
        units = [(pi, r, hh) for pi in range(pp) for r in range(tq // rb) for hh in range(2)]

        def keys(r):
            past = n * tq + r * rb
            return past, past + rb

        def logits(u):
            pi, r, hh = units[u]
            past, total = keys(r)
            q = q_ref[0, r * rb:(r + 1) * rb, pi * PAIR:(pi + 1) * PAIR]
            qa = qa_ref[0, r * rb:(r + 1) * rb, :]
            a0 = PIECES * (2 * (grp * pp + pi) + hh)
            amask = ((lane >= a0) & (lane < a0 + PIECES)) | ((lane >= k0 + a0) & (lane < k0 + a0 + PIECES))
            q_aug = jnp.concatenate([jnp.where(_head_mask(lane, hh), q, jnp.zeros_like(q)),
                                     jnp.where(amask, qa, jnp.zeros_like(qa))], axis=-1)
            m = None
            for lo, hi in ([(0, past)] if past else []) + [(past, total)]:
                kaug = jnp.concatenate([k_ref[0, lo:hi, pi * PAIR:(pi + 1) * PAIR], ka_ref[0, lo:hi, :]], axis=-1)
                s = _dot_nt(kaug, q_aug)
                if lo == past:
                    s = jnp.where(causal, s, NEG)
                s_ref[u % slots, lo:hi, :] = s
                col_max = jnp.max(s, axis=0, keepdims=True)
                m = col_max if m is None else jnp.maximum(m, col_max)
            return m

        def weights(u, m):
            _, total = keys(units[u][1])
            p_ref[u % slots, 0:total, :] = jnp.exp2(s_ref[u % slots, 0:total, :] - m).astype(BF16)

        def values(u):
            pi, r, hh = units[u]
            _, total = keys(r)
            return _dot(_value_aug_t(vt_ref[0, pi * PAIR:(pi + 1) * PAIR, 0:total], feat, hh),
                        p_ref[u % slots, 0:total, :])

        ms, accs = {}, {}
        n_units = len(units)
        for t in range(0, n_units + 4, 2):
            for u in (t, t + 1):
                if u < n_units:
                    ms[u] = logits(u)
            for u in (t - 2, t - 1):
                if 0 <= u < n_units:
                    weights(u, ms.pop(u))
            for u in (t - 4, t - 3):
                if 0 <= u < n_units:
                    accs[units[u]] = values(u)
            if 0 <= t - 4 < n_units:
                pi, r, _ = units[t - 4]
                o_ref[0, r * rb:(r + 1) * rb, pi * PAIR:(pi + 1) * PAIR] = _pair_output_t(
                    accs.pop((pi, r, 0)), accs.pop((pi, r, 1)), feat).astype(o_ref.dtype)

    for n in range(k_ref.shape[1] // tq):
        pl.when(pl.program_id(2) == n)(functools.partial(attend, n))


def _fox(proj, vt, qa, ka, heads, tq, rb, pp):
    B, S, _ = proj.shape
    groups = heads // 2 // pp
    w = pp * PAIR
    return pl.pallas_call(
        functools.partial(_fox_kernel, tq=tq, rb=rb, pp=pp, heads=heads),
        grid=(B, groups, S // tq),
        in_specs=[
            pl.BlockSpec((1, tq, w), lambda b, g, i: (b, i, g)),
            pl.BlockSpec((1, tq, LANES), lambda b, g, i: (b, i, 0)),
            pl.BlockSpec((1, S, w), lambda b, g, i: (b, 0, groups + g)),
            pl.BlockSpec((1, S, LANES), lambda b, g, i: (b, 0, 0)),
            pl.BlockSpec((1, w, S), lambda b, g, i: (b, g, 0)),
        ],
        out_specs=pl.BlockSpec((1, tq, w), lambda b, g, i: (b, i, g)),
        out_shape=jax.ShapeDtypeStruct((B, S, heads * HEAD_DIM), BF16),
        scratch_shapes=[pltpu.VMEM((6, S, rb), F32), pltpu.VMEM((6, S, rb), BF16)],
        compiler_params=pltpu.CompilerParams(
            dimension_semantics=("arbitrary", "arbitrary", "arbitrary"), vmem_limit_bytes=VMEM_LIMIT),
        name="fox_attn",
    )(proj, qa, proj, ka, vt)


def _chunk_kernel(q_ref, k_ref, vt_ref, g_ref, o_ref, bias_ref, *, tq, rbs, pp):
    i = pl.program_id(2)
    lane = lax.broadcasted_iota(jnp.int32, (1, LANES), 1)
    feat = lax.broadcasted_iota(jnp.int32, (PAIR, 1), 0)

    @pl.when((pl.program_id(1) == 0) & (i == 0))
    def _():
        k_chunk = lax.broadcasted_iota(jnp.int32, (tq, tq), 0) // CHUNK
        q_chunk = lax.broadcasted_iota(jnp.int32, (tq, tq), 1) // CHUNK
        for which in range(3):
            back = which * (tq // CHUNK) + q_chunk - k_chunk
            vis = (back >= 0) & (back <= LEFT_CHUNKS)
            for hd in range(2 * pp):
                g = jnp.broadcast_to(g_ref[0, which, hd], (tq, 2 * tq))
                toep = pltpu.roll(g, 0, 1, stride=1, stride_axis=0)[:, :tq]
                bias_ref[hd, (2 - which) * tq:(3 - which) * tq, :] = jnp.where(vis, toep, NEG)

    def attend(first_step):
        units = [(pi, blk, hh) for pi in range(pp) for blk in range(rbs) for hh in range(2)]

        def key_span(blk):
            if first_step:
                n_blocks = min(3, blk + 1)
                return (blk - (n_blocks - 1)) * tq, n_blocks * tq
            return pl.multiple_of((i * rbs + blk - 2) * tq, tq), 3 * tq

        def logits(pi, blk, hh):
            cols = slice(pi * PAIR, (pi + 1) * PAIR)
            ks, span = key_span(blk)
            q = q_ref[0, blk * tq:(blk + 1) * tq, cols]
            qh = jnp.where(_head_mask(lane, hh), q, jnp.zeros_like(q))
            return _dot_nt(k_ref[0, pl.ds(ks, span), cols], qh) + bias_ref[2 * pi + hh, 3 * tq - span:, :]

        def weighted_values(pi, blk, hh, p):
            ks, span = key_span(blk)
            return _dot(_value_aug_t(vt_ref[0, pi * PAIR:(pi + 1) * PAIR, pl.ds(ks, span)], feat, hh), p)

        ss, ps, accs = {}, {}, {}
        n_units = len(units)
        for t in range(0, n_units + 4, 2):
            for u in (t, t + 1):
                if u < n_units:
                    ss[u] = logits(*units[u])
            for u in (t - 2, t - 1):
                if 0 <= u < n_units:
                    ps[u] = _softmax_t([ss.pop(u)])[0]
            for u in (t - 4, t - 3):
                if 0 <= u < n_units:
                    accs[units[u]] = weighted_values(*units[u], ps.pop(u))
            if 0 <= t - 4 < n_units:
                pi, blk, _ = units[t - 4]
                o_ref[0, blk * tq:(blk + 1) * tq, pi * PAIR:(pi + 1) * PAIR] = _pair_output_t(
                    accs.pop((pi, blk, 0)), accs.pop((pi, blk, 1)), feat).astype(o_ref.dtype)

    pl.when(i == 0)(functools.partial(attend, True))
    pl.when(i > 0)(functools.partial(attend, False))


def _chunk_bias_rows(rel_table, tq, pp):
    heads = rel_table.shape[0]
    m = np.arange(2 * tq)
    offset = np.where(m < tq, m, m - 2 * tq)
    idx = np.stack([np.clip(which * tq + offset, -MAX_REL, MAX_REL) + MAX_REL for which in range(3)])
    rows = rel_table.astype(F32)[:, idx] * LOG2E
    return rows.reshape(heads // (2 * pp), 2 * pp, 3, 1, 2 * tq).transpose(0, 2, 1, 3, 4)


def _chunk(proj, vt, bias_rows, tq, rbs, pp, col0, row0):
    B, S, _ = proj.shape
    groups = bias_rows.shape[0]
    w = pp * PAIR
    assert tq % CHUNK == 0 and LEFT_CHUNKS * CHUNK <= 2 * tq, "band must fit in three key blocks"
    assert rbs >= 2, "after the first step every query block must have two key blocks behind it"
    return pl.pallas_call(
        functools.partial(_chunk_kernel, tq=tq, rbs=rbs, pp=pp),
        grid=(groups, B, S // (rbs * tq)),
        in_specs=[
            pl.BlockSpec((1, rbs * tq, w), lambda g, b, i: (b, i, col0 + g)),
            pl.BlockSpec((1, S, w), lambda g, b, i: (b, 0, col0 + groups + g)),
            pl.BlockSpec((1, w, S), lambda g, b, i: (b, row0 + g, 0)),
            pl.BlockSpec((1, 3, 2 * pp, 1, 2 * tq), lambda g, b, i: (g, 0, 0, 0, 0)),
        ],
        out_specs=pl.BlockSpec((1, rbs * tq, w), lambda g, b, i: (b, i, g)),
        out_shape=jax.ShapeDtypeStruct((B, S, groups * w), BF16),
        scratch_shapes=[pltpu.VMEM((2 * pp, 3 * tq, tq), F32)],
        compiler_params=pltpu.CompilerParams(
            dimension_semantics=("arbitrary", "arbitrary", "arbitrary"), vmem_limit_bytes=VMEM_LIMIT),
        name="chunk_attn",
    )(proj, proj, vt, bias_rows)


def _mix_mem_kernel(yf_ref, yc_ref, x_ref, gf_ref, gc_ref, wo_ref, gpost_ref, gpre_ref, wq_ref,
                    mem_ref, gkv_ref, wk_ref, wv_ref, wmo_ref, gmpost_ref, o_ref, mk_ref, mv_ref, *, sub):
    @pl.when(pl.program_id(1) == 0)
    def _():
        mb = _rms(mem_ref[0], gkv_ref[...]).astype(BF16)
        mk_ref[...] = _dot(mb, wk_ref[...]).astype(BF16)
        mv_ref[...] = _dot(mb, wv_ref[...]).astype(BF16)

    d_fox = yf_ref.shape[2]
    halves = range(x_ref.shape[1] // sub)
    rows = [slice(h * sub, (h + 1) * sub) for h in halves]

    def mix(r):
        yf = _rms(yf_ref[0, r, :].astype(F32), gf_ref[...]).astype(BF16)
        yc = _rms(yc_ref[0, r, :].astype(F32), gc_ref[...]).astype(BF16)
        return _dot(yf, wo_ref[0:d_fox, :]) + _dot(yc, wo_ref[d_fox:, :])

    def query(x1):
        return _dot(_rms(x1, gpre_ref[...]).astype(BF16), wq_ref[...]).astype(BF16)

    def attend(q):
        dh = q.shape[1] // MEM_HEADS
        outs = []
        for hh in range(MEM_HEADS):
            sl = slice(hh * dh, (hh + 1) * dh)
            s = _dot_nt(q[:, sl], mk_ref[:, sl])
            pr = jnp.exp2(s - jnp.max(s, axis=-1, keepdims=True))
            inv = 1.0 / jnp.sum(pr, axis=-1, keepdims=True)
            outs.append((_dot(pr.astype(BF16), mv_ref[:, sl]) * inv).astype(BF16))
        return jnp.concatenate(outs, axis=-1)

    ys = [mix(rows[h]) for h in halves]
    x1s = [x_ref[0, rows[h], :] + _rms(ys[h], gpost_ref[...]) for h in halves]
    qs = [query(x1s[h]) for h in halves]
    os = [attend(qs[h]) for h in halves]
    y2s = [_dot(os[h], wmo_ref[...]) for h in halves]
    for h in halves:
        o_ref[0, rows[h], :] = x1s[h] + _rms(y2s[h], gmpost_ref[...])


def _mix_mem(yf, yc, x, gf, gc, wo, gpost, gpre, wq, mem, gkv, wk, wv, wmo, gmpost, tm, sub):
    B, S, D = x.shape
    M = mem.shape[1]
    dg = yf.shape[2]
    vec = lambda n: pl.BlockSpec((1, n), lambda b, i: (0, 0))
    mat = lambda r, c: pl.BlockSpec((r, c), lambda b, i: (0, 0), pipeline_mode=pl.Buffered(1))
    return pl.pallas_call(
        functools.partial(_mix_mem_kernel, sub=sub),
        grid=(B, S // tm),
        in_specs=[
            pl.BlockSpec((1, tm, dg), lambda b, i: (b, i, 0)),
            pl.BlockSpec((1, tm, dg), lambda b, i: (b, i, 0)),
            pl.BlockSpec((1, tm, D), lambda b, i: (b, i, 0)),
            vec(dg), vec(dg), mat(2 * dg, D), vec(D),
            vec(D), mat(D, D),
            pl.BlockSpec((1, M, D), lambda b, i: (b, 0, 0)),
            vec(D), mat(D, D), mat(D, D),
            mat(D, D), vec(D),
        ],
        out_specs=pl.BlockSpec((1, tm, D), lambda b, i: (b, i, 0)),
        out_shape=jax.ShapeDtypeStruct((B, S, D), F32),
        scratch_shapes=[pltpu.VMEM((M, D), BF16), pltpu.VMEM((M, D), BF16)],
        compiler_params=pltpu.CompilerParams(
            dimension_semantics=("arbitrary", "arbitrary"), vmem_limit_bytes=VMEM_LIMIT),
        name="mix_mem",
    )(yf, yc, x, gf, gc, wo, gpost, gpre, wq, mem, gkv, wk, wv, wmo, gmpost)


def _mlp_kernel(x_ref, gpre_ref, w1_ref, w2_ref, gpost_ref, o_ref, *, ff_block, sub):
    halves = range(x_ref.shape[0] // sub)
    rows = [slice(h * sub, (h + 1) * sub) for h in halves]
    hs = [_rms(x_ref[rows[h], :], gpre_ref[...]).astype(BF16) for h in halves]
    ys = [None for _ in halves]
    for n in range(0, w1_ref.shape[1], ff_block):
        for h in halves:
            a = jnp.maximum(_dot(hs[h], w1_ref[:, n:n + ff_block]), 0.0)
            part = _dot((a * a).astype(BF16), w2_ref[n:n + ff_block, :])
            ys[h] = part if ys[h] is None else ys[h] + part
    for h in halves:
        o_ref[rows[h], :] = x_ref[rows[h], :] + _rms(ys[h], gpost_ref[...])


def _mlp(x, gpre, w1, w2, gpost, tm, ff_block, sub):
    R, D = x.shape
    d_ff = w1.shape[1]
    return pl.pallas_call(
        functools.partial(_mlp_kernel, ff_block=ff_block, sub=sub),
        grid=(R // tm,),
        in_specs=[
            pl.BlockSpec((tm, D), lambda i: (i, 0)),
            pl.BlockSpec((1, D), lambda i: (0, 0)),
            pl.BlockSpec((D, d_ff), lambda i: (0, 0), pipeline_mode=pl.Buffered(1)),
            pl.BlockSpec((d_ff, D), lambda i: (0, 0), pipeline_mode=pl.Buffered(1)),
            pl.BlockSpec((1, D), lambda i: (0, 0)),
        ],
        out_specs=pl.BlockSpec((tm, D), lambda i: (i, 0)),
        out_shape=jax.ShapeDtypeStruct((R, D), F32),
        compiler_params=pltpu.CompilerParams(
            dimension_semantics=("arbitrary",), vmem_limit_bytes=VMEM_LIMIT),
        name="mlp",
    )(x, gpre, w1, w2, gpost)


def kernel(x, mem, w_in, b_fgt, rel_bias, g_fox_out, g_chk_out, w_out, g_mix_pre, g_mix_post,
           g_mem_kv, w_mq, w_mk, w_mv, w_mo, g_mem_pre, g_mem_post,
           w_ff1, w_ff2, g_ff_pre, g_ff_post):
    B, S, D = x.shape
    depth = w_in.shape[0]
    heads = b_fgt.shape[1]
    d_fox = heads * HEAD_DIM
    d_chk = rel_bias.shape[1] * HEAD_DIM
    assert d_fox == d_chk and heads % 2 == 0 and heads <= LANES
    pairs = heads // 2
    row = lambda v: v.reshape(1, -1)

    for l in range(depth):
        scale = HEAD_DIM ** -0.5 * LOG2E
        wl = w_in[l]
        o_f = 3 * d_fox
        w_f = wl[:, o_f:o_f + heads]
        o_c = o_f + heads
        w_all = jnp.concatenate([
            wl[:, :d_fox] * scale, wl[:, d_fox:2 * d_fox],
            wl[:, o_c:o_c + d_chk] * scale, wl[:, o_c + d_chk:o_c + 2 * d_chk],
            jnp.pad(jnp.tile(w_f, (1, PIECES)), ((0, 0), (0, LANES - PIECES * heads))),
        ], axis=1).astype(BF16)
        wv_t = jnp.concatenate([wl[:, 2 * d_fox:o_f], wl[:, o_c + 2 * d_chk:]], axis=1).T.astype(BF16)
        b_f = jnp.pad(jnp.tile(b_fgt[l], PIECES), (0, LANES - PIECES * heads)).reshape(1, LANES)

        proj, vt, qa, ka = _inproj(x, row(g_mix_pre[l]), w_all, wv_t, b_f, heads, tm=1024, sub=256)
        fox_pp, chk_pp = 4, 4
        y_fox = _fox(proj, vt, qa, ka, heads, tq=512, rb=256, pp=fox_pp)
        y_chk = _chunk(proj, vt, _chunk_bias_rows(rel_bias[l], 256, chk_pp), tq=256, rbs=2, pp=chk_pp,
                       col0=2 * pairs // chk_pp, row0=pairs // chk_pp)

        mem_scale = (D // MEM_HEADS) ** -0.5 * LOG2E
        x = _mix_mem(y_fox, y_chk, x, row(g_fox_out[l]), row(g_chk_out[l]), w_out[l].astype(BF16),
                     row(g_mix_post[l]), row(g_mem_pre[l]), (w_mq[l] * mem_scale).astype(BF16),
                     mem, row(g_mem_kv[l]), w_mk[l].astype(BF16), w_mv[l].astype(BF16),
                     w_mo[l].astype(BF16), row(g_mem_post[l]), tm=1024, sub=256)
        x = _mlp(x.reshape(B * S, D), row(g_ff_pre[l]), w_ff1[l].astype(BF16), w_ff2[l].astype(BF16),
                 row(g_ff_post[l]), tm=1024, ff_block=1024, sub=256).reshape(B, S, D)
    return x
```

```python
import functools
import math

import numpy as np
import jax
import jax.numpy as jnp
from jax import lax
from jax.experimental import pallas as pl
from jax.experimental.pallas import tpu as pltpu

F32 = jnp.float32
BF16 = jnp.bfloat16

EPS = 1e-6
HEAD_DIM = 64
PAIR = 2 * HEAD_DIM
CHUNK = 64
LEFT_CHUNKS = 8
MAX_REL = 128
MEM_HEADS = 4
NEG = -1e30
LOG2E = math.log2(math.e)
LANES = 128
VMEM_LIMIT = 56 * 1024 * 1024
PIECES = 3
SUM_LANE = (HEAD_DIM, 0)


def _rms(x, g):
    return x * lax.rsqrt(jnp.mean(x * x, axis=-1, keepdims=True) + EPS) * g


def _dot(a, b):
    return jnp.dot(a, b, preferred_element_type=F32)


def _dot_nt(a, b):
    return lax.dot_general(a, b, (((1,), (1,)), ((), ())), preferred_element_type=F32)


def _split3(x):
    hi = x.astype(BF16)
    r1 = x - hi.astype(F32)
    mid = r1.astype(BF16)
    lo = (r1 - mid.astype(F32)).astype(BF16)
    return hi, mid, lo


def _inproj_kernel(x_ref, g_ref, w_ref, wvt_ref, bf_ref, tri_ref, e_ref, ones_ref,
                   proj_ref, vt_ref, qa_ref, ka_ref, carry_ref, *, heads, sub):
    @pl.when(pl.program_id(1) == 0)
    def _():
        carry_ref[...] = jnp.zeros_like(carry_ref)

    n_main = proj_ref.shape[2]
    halves = range(x_ref.shape[1] // sub)
    rows = [slice(h * sub, (h + 1) * sub) for h in halves]
    lane = lax.broadcasted_iota(jnp.int32, (1, LANES), 1)
    hbs = [_rms(x_ref[0, rows[h], :], g_ref[...]).astype(BF16) for h in halves]
    fs = []
    for h in halves:
        res = _dot(hbs[h], w_ref[...])
        proj_ref[0, rows[h], :] = res[:, :n_main].astype(BF16)
        vt_ref[0, :, rows[h]] = _dot_nt(wvt_ref[...], hbs[h]).astype(BF16)
        fs.append(res[:, n_main:] + bf_ref[...])
    carry = carry_ref[...]
    for h in halves:
        f = fs[h]
        logf = jnp.minimum(f, 0.0) - jnp.log1p(jnp.exp(-jnp.abs(f)))
        hi, mid, lo = _split3(logf)
        packed = jnp.where(lane < heads, hi, jnp.where(lane < 2 * heads, mid, lo))
        cum = _dot(tri_ref[...], packed)
        c = (cum + pltpu.roll(cum, LANES - heads, 1) + pltpu.roll(cum, LANES - 2 * heads, 1)) + carry
        carry = c[sub - 1:sub, :]
        pieces = jnp.concatenate(_split3(c * LOG2E), axis=-1)
        sel = _dot(pieces, e_ref[...])
        qa_ref[0, rows[h], :] = (sel[:, :LANES] + ones_ref[0:1, :]).astype(BF16)
        ka_ref[0, rows[h], :] = (ones_ref[1:2, :] - sel[:, LANES:]).astype(BF16)
    carry_ref[...] = carry


def _aug_constants(heads):
    k0 = PIECES * heads
    assert 2 * k0 <= LANES
    e = np.zeros((PIECES * LANES, 2 * LANES), np.float32)
    for h in range(heads):
        for r in range(PIECES):
            e[r * LANES + h, PIECES * h + r] = 1.0
            e[r * LANES + h, LANES + k0 + PIECES * h + r] = 1.0
    ones = np.zeros((2, LANES), np.float32)
    ones[0, k0:2 * k0] = 1.0
    ones[1, 0:k0] = 1.0
    return jnp.asarray(e, BF16), jnp.asarray(ones)


def _inproj(x, g, w_all, wv_t, b_f, heads, tm, sub):
    B, S, D = x.shape
    n_main = w_all.shape[1] - LANES
    n_val = wv_t.shape[0]
    tri = jnp.asarray(np.tril(np.ones((sub, sub), np.float32)), BF16)
    e, ones = _aug_constants(heads)
    const = lambda r, c: pl.BlockSpec((r, c), lambda b, i: (0, 0), pipeline_mode=pl.Buffered(1))
    return pl.pallas_call(
        functools.partial(_inproj_kernel, heads=heads, sub=sub),
        grid=(B, S // tm),
        in_specs=[
            pl.BlockSpec((1, tm, D), lambda b, i: (b, i, 0)),
            const(1, D), const(D, n_main + LANES), const(n_val, D), const(1, LANES), const(sub, sub),
            const(PIECES * LANES, 2 * LANES), const(2, LANES),
        ],
        out_specs=[
            pl.BlockSpec((1, tm, n_main), lambda b, i: (b, i, 0)),
            pl.BlockSpec((1, n_val, tm), lambda b, i: (b, 0, i)),
            pl.BlockSpec((1, tm, LANES), lambda b, i: (b, i, 0)),
            pl.BlockSpec((1, tm, LANES), lambda b, i: (b, i, 0)),
        ],
        out_shape=[
            jax.ShapeDtypeStruct((B, S, n_main), BF16),
            jax.ShapeDtypeStruct((B, n_val, S), BF16),
            jax.ShapeDtypeStruct((B, S, LANES), BF16),
            jax.ShapeDtypeStruct((B, S, LANES), BF16),
        ],
        scratch_shapes=[pltpu.VMEM((1, LANES), F32)],
        compiler_params=pltpu.CompilerParams(
            dimension_semantics=("arbitrary", "arbitrary"), vmem_limit_bytes=VMEM_LIMIT),
        name="inproj",
    )(x, g, w_all, wv_t, b_f, tri, e, ones)


def _head_mask(lane, hh):
    return lane < HEAD_DIM if hh == 0 else lane >= HEAD_DIM


def _value_aug_t(vt, feat, hh):
    ones = jnp.where(feat == SUM_LANE[hh], 1.0, 0.0).astype(vt.dtype)
    return jnp.where(_head_mask(feat, hh), vt, ones)


def _softmax_t(ss):
    m = functools.reduce(jnp.maximum, [jnp.max(s, axis=0, keepdims=True) for s in ss])
    return [jnp.exp2(s - m).astype(BF16) for s in ss]


def _pair_output_t(acc0, acc1, feat):
    l0 = acc0[SUM_LANE[0]:SUM_LANE[0] + 1, :]
    l1 = acc1[SUM_LANE[1]:SUM_LANE[1] + 1, :]
    return jnp.where(_head_mask(feat, 0), acc0 * (1.0 / l0), acc1 * (1.0 / l1)).T


def _fox_kernel(q_ref, qa_ref, k_ref, ka_ref, vt_ref, o_ref, s_ref, p_ref, *, tq, rb, pp, heads):
    grp = pl.program_id(1)
    lane = lax.broadcasted_iota(jnp.int32, (1, LANES), 1)
    feat = lax.broadcasted_iota(jnp.int32, (PAIR, 1), 0)
    k0 = PIECES * heads
    causal = lax.broadcasted_iota(jnp.int32, (rb, rb), 0) <= lax.broadcasted_iota(jnp.int32, (rb, rb), 1)
    slots = s_ref.shape[0]

    def attend(n):
        units = [(pi, r, hh) for pi in range(pp) for r in range(tq // rb) for hh in range(2)]

        def keys(r):
            past = n * tq + r * rb
            return past, past + rb

        def logits(u):
            pi, r, hh = units[u]
            past, total = keys(r)
            q = q_ref[0, r * rb:(r + 1) * rb, pi * PAIR:(pi + 1) * PAIR]
            qa = qa_ref[0, r * rb:(r + 1) * rb, :]
            a0 = PIECES * (2 * (grp * pp + pi) + hh)
            amask = ((lane >= a0) & (lane < a0 + PIECES)) | ((lane >= k0 + a0) & (lane < k0 + a0 + PIECES))
            q_aug = jnp.concatenate([jnp.where(_head_mask(lane, hh), q, jnp.zeros_like(q)),
                                     jnp.where(amask, qa, jnp.zeros_like(qa))], axis=-1)
            m = None
            for lo, hi in ([(0, past)] if past else []) + [(past, total)]:
                kaug = jnp.concatenate([k_ref[0, lo:hi, pi * PAIR:(pi + 1) * PAIR], ka_ref[0, lo:hi, :]], axis=-1)
                s = _dot_nt(kaug, q_aug)
                if lo == past:
                    s = jnp.where(causal, s, NEG)
                s_ref[u % slots, lo:hi, :] = s
                col_max = jnp.max(s, axis=0, keepdims=True)
                m = col_max if m is None else jnp.maximum(m, col_max)
            return m

        def weights(u, m):
            _, total = keys(units[u][1])
            p_ref[u % slots, 0:total, :] = jnp.exp2(s_ref[u % slots, 0:total, :] - m).astype(BF16)

        def values(u):
            pi, r, hh = units[u]
            _, total = keys(r)
            return _dot(_value_aug_t(vt_ref[0, pi * PAIR:(pi + 1) * PAIR, 0:total], feat, hh),
                        p_ref[u % slots, 0:total, :])

        ms, accs = {}, {}
        n_units = len(units)
        for t in range(0, n_units + 4, 2):
            for u in (t, t + 1):
                if u < n_units:
                    ms[u] = logits(u)
            for u in (t - 2, t - 1):
                if 0 <= u < n_units:
                    weights(u, ms.pop(u))
            for u in (t - 4, t - 3):
                if 0 <= u < n_units:
                    accs[units[u]] = values(u)
            if 0 <= t - 4 < n_units:
                pi, r, _ = units[t - 4]
                o_ref[0, r * rb:(r + 1) * rb, pi * PAIR:(pi + 1) * PAIR] = _pair_output_t(
                    accs.pop((pi, r, 0)), accs.pop((pi, r, 1)), feat).astype(o_ref.dtype)

    for n in range(k_ref.shape[1] // tq):
        pl.when(pl.program_id(2) == n)(functools.partial(attend, n))


def _fox(proj, vt, qa, ka, heads, tq, rb, pp):
    B, S, _ = proj.shape
    groups = heads // 2 // pp
    w = pp * PAIR
    return pl.pallas_call(
        functools.partial(_fox_kernel, tq=tq, rb=rb, pp=pp, heads=heads),
        grid=(B, groups, S // tq),
        in_specs=[
            pl.BlockSpec((1, tq, w), lambda b, g, i: (b, i, g)),
            pl.BlockSpec((1, tq, LANES), lambda b, g, i: (b, i, 0)),
            pl.BlockSpec((1, S, w), lambda b, g, i: (b, 0, groups + g)),
            pl.BlockSpec((1, S, LANES), lambda b, g, i: (b, 0, 0)),
            pl.BlockSpec((1, w, S), lambda b, g, i: (b, g, 0)),
        ],
        out_specs=pl.BlockSpec((1, tq, w), lambda b, g, i: (b, i, g)),
        out_shape=jax.ShapeDtypeStruct((B, S, heads * HEAD_DIM), BF16),
        scratch_shapes=[pltpu.VMEM((6, S, rb), F32), pltpu.VMEM((6, S, rb), BF16)],
        compiler_params=pltpu.CompilerParams(
            dimension_semantics=("arbitrary", "arbitrary", "arbitrary"), vmem_limit_bytes=VMEM_LIMIT),
        name="fox_attn",
    )(proj, qa, proj, ka, vt)


def _chunk_kernel(q_ref, k_ref, vt_ref, g_ref, *refs, tq, rbs, pp, scales):
    n_w = len(scales)
    w_refs, o_ref, wb_refs, bias_ref = refs[:n_w], refs[n_w], refs[n_w + 1:2 * n_w + 1], refs[2 * n_w + 1]
    for w_ref, wb_ref, scale in zip(w_refs, wb_refs, scales):
        w = w_ref[...]
        wb_ref[...] = (w if scale == 1.0 else w * scale).astype(wb_ref.dtype)

    i = pl.program_id(2)
    lane = lax.broadcasted_iota(jnp.int32, (1, LANES), 1)
    feat = lax.broadcasted_iota(jnp.int32, (PAIR, 1), 0)

    @pl.when((pl.program_id(1) == 0) & (i == 0))
    def _():
        k_chunk = lax.broadcasted_iota(jnp.int32, (tq, tq), 0) // CHUNK
        q_chunk = lax.broadcasted_iota(jnp.int32, (tq, tq), 1) // CHUNK
        for which in range(3):
            back = which * (tq // CHUNK) + q_chunk - k_chunk
            vis = (back >= 0) & (back <= LEFT_CHUNKS)
            for hd in range(2 * pp):
                g = jnp.broadcast_to(g_ref[0, which, hd], (tq, 2 * tq))
                toep = pltpu.roll(g, 0, 1, stride=1, stride_axis=0)[:, :tq]
                bias_ref[hd, (2 - which) * tq:(3 - which) * tq, :] = jnp.where(vis, toep, NEG)

    def attend(first_step):
        units = [(pi, blk, hh) for pi in range(pp) for blk in range(rbs) for hh in range(2)]

        def key_span(blk):
            if first_step:
                n_blocks = min(3, blk + 1)
                return (blk - (n_blocks - 1)) * tq, n_blocks * tq
            return pl.multiple_of((i * rbs + blk - 2) * tq, tq), 3 * tq

        def logits(pi, blk, hh):
            cols = slice(pi * PAIR, (pi + 1) * PAIR)
            ks, span = key_span(blk)
            q = q_ref[0, blk * tq:(blk + 1) * tq, cols]
            qh = jnp.where(_head_mask(lane, hh), q, jnp.zeros_like(q))
            return _dot_nt(k_ref[0, pl.ds(ks, span), cols], qh) + bias_ref[2 * pi + hh, 3 * tq - span:, :]

        def weighted_values(pi, blk, hh, p):
            ks, span = key_span(blk)
            return _dot(_value_aug_t(vt_ref[0, pi * PAIR:(pi + 1) * PAIR, pl.ds(ks, span)], feat, hh), p)

        ss, ps, accs = {}, {}, {}
        n_units = len(units)
        for t in range(0, n_units + 4, 2):
            for u in (t, t + 1):
                if u < n_units:
                    ss[u] = logits(*units[u])
            for u in (t - 2, t - 1):
                if 0 <= u < n_units:
                    ps[u] = _softmax_t([ss.pop(u)])[0]
            for u in (t - 4, t - 3):
                if 0 <= u < n_units:
                    accs[units[u]] = weighted_values(*units[u], ps.pop(u))
            if 0 <= t - 4 < n_units:
                pi, blk, _ = units[t - 4]
                o_ref[0, blk * tq:(blk + 1) * tq, pi * PAIR:(pi + 1) * PAIR] = _pair_output_t(
                    accs.pop((pi, blk, 0)), accs.pop((pi, blk, 1)), feat).astype(o_ref.dtype)

    pl.when(i == 0)(functools.partial(attend, True))
    pl.when(i > 0)(functools.partial(attend, False))


def _chunk_bias_rows(rel_table, tq, pp):
    heads = rel_table.shape[0]
    m = np.arange(2 * tq)
    offset = np.where(m < tq, m, m - 2 * tq)
    idx = np.stack([np.clip(which * tq + offset, -MAX_REL, MAX_REL) + MAX_REL for which in range(3)])
    rows = rel_table.astype(F32)[:, idx] * LOG2E
    return rows.reshape(heads // (2 * pp), 2 * pp, 3, 1, 2 * tq).transpose(0, 2, 1, 3, 4)


def _chunk(proj, vt, bias_rows, weights, scales, tq, rbs, pp, col0, row0):
    B, S, _ = proj.shape
    groups = bias_rows.shape[0]
    w = pp * PAIR
    steps = S // (rbs * tq)
    assert tq % CHUNK == 0 and LEFT_CHUNKS * CHUNK <= 2 * tq, "band must fit in three key blocks"
    assert rbs >= 2, "after the first step every query block must have two key blocks behind it"
    assert groups == 1, "each weight slice is cast at exactly one grid step"
    slice_spec = lambda wt: pl.BlockSpec((wt.shape[0] // (B * steps), wt.shape[1]),
                                         lambda g, b, i: (b * steps + i, 0))
    out = pl.pallas_call(
        functools.partial(_chunk_kernel, tq=tq, rbs=rbs, pp=pp, scales=tuple(scales)),
        grid=(groups, B, steps),
        in_specs=[
            pl.BlockSpec((1, rbs * tq, w), lambda g, b, i: (b, i, col0 + g)),
            pl.BlockSpec((1, S, w), lambda g, b, i: (b, 0, col0 + groups + g)),
            pl.BlockSpec((1, w, S), lambda g, b, i: (b, row0 + g, 0)),
            pl.BlockSpec((1, 3, 2 * pp, 1, 2 * tq), lambda g, b, i: (g, 0, 0, 0, 0)),
        ] + [slice_spec(wt) for wt in weights],
        out_specs=[pl.BlockSpec((1, rbs * tq, w), lambda g, b, i: (b, i, g))]
        + [slice_spec(wt) for wt in weights],
        out_shape=[jax.ShapeDtypeStruct((B, S, groups * w), BF16)]
        + [jax.ShapeDtypeStruct(wt.shape, BF16) for wt in weights],
        scratch_shapes=[pltpu.VMEM((2 * pp, 3 * tq, tq), F32)],
        compiler_params=pltpu.CompilerParams(
            dimension_semantics=("arbitrary", "arbitrary", "arbitrary"), vmem_limit_bytes=VMEM_LIMIT),
        name="chunk_attn",
    )(proj, proj, vt, bias_rows, *weights)
    return out[0], out[1:]


def _mix_mem_kernel(yf_ref, yc_ref, x_ref, gf_ref, gc_ref, wo_ref, gpost_ref, gpre_ref, wq_ref,
                    mem_ref, gkv_ref, wk_ref, wv_ref, wmo_ref, gmpost_ref, o_ref, mk_ref, mv_ref, *, sub):
    @pl.when(pl.program_id(1) == 0)
    def _():
        mb = _rms(mem_ref[0], gkv_ref[...]).astype(BF16)
        mk_ref[...] = _dot(mb, wk_ref[...]).astype(BF16)
        mv_ref[...] = _dot(mb, wv_ref[...]).astype(BF16)

    d_fox = yf_ref.shape[2]
    halves = range(x_ref.shape[1] // sub)
    rows = [slice(h * sub, (h + 1) * sub) for h in halves]

    def mix(r):
        yf = _rms(yf_ref[0, r, :].astype(F32), gf_ref[...]).astype(BF16)
        yc = _rms(yc_ref[0, r, :].astype(F32), gc_ref[...]).astype(BF16)
        return _dot(yf, wo_ref[0:d_fox, :]) + _dot(yc, wo_ref[d_fox:, :])

    def query(x1):
        return _dot(_rms(x1, gpre_ref[...]).astype(BF16), wq_ref[...]).astype(BF16)

    def attend(q):
        dh = q.shape[1] // MEM_HEADS
        outs = []
        for hh in range(MEM_HEADS):
            sl = slice(hh * dh, (hh + 1) * dh)
            s = _dot_nt(q[:, sl], mk_ref[:, sl])
            pr = jnp.exp2(s - jnp.max(s, axis=-1, keepdims=True))
            inv = 1.0 / jnp.sum(pr, axis=-1, keepdims=True)
            outs.append((_dot(pr.astype(BF16), mv_ref[:, sl]) * inv).astype(BF16))
        return jnp.concatenate(outs, axis=-1)

    ys = [mix(rows[h]) for h in halves]
    x1s = [x_ref[0, rows[h], :] + _rms(ys[h], gpost_ref[...]) for h in halves]
    qs = [query(x1s[h]) for h in halves]
    os = [attend(qs[h]) for h in halves]
    y2s = [_dot(os[h], wmo_ref[...]) for h in halves]
    for h in halves:
        o_ref[0, rows[h], :] = x1s[h] + _rms(y2s[h], gmpost_ref[...])


def _mix_mem(yf, yc, x, gf, gc, wo, gpost, gpre, wq, mem, gkv, wk, wv, wmo, gmpost, tm, sub):
    B, S, D = x.shape
    M = mem.shape[1]
    dg = yf.shape[2]
    vec = lambda n: pl.BlockSpec((1, n), lambda b, i: (0, 0))
    mat = lambda r, c: pl.BlockSpec((r, c), lambda b, i: (0, 0), pipeline_mode=pl.Buffered(1))
    return pl.pallas_call(
        functools.partial(_mix_mem_kernel, sub=sub),
        grid=(B, S // tm),
        in_specs=[
            pl.BlockSpec((1, tm, dg), lambda b, i: (b, i, 0)),
            pl.BlockSpec((1, tm, dg), lambda b, i: (b, i, 0)),
            pl.BlockSpec((1, tm, D), lambda b, i: (b, i, 0)),
            vec(dg), vec(dg), mat(2 * dg, D), vec(D),
            vec(D), mat(D, D),
            pl.BlockSpec((1, M, D), lambda b, i: (b, 0, 0)),
            vec(D), mat(D, D), mat(D, D),
            mat(D, D), vec(D),
        ],
        out_specs=pl.BlockSpec((1, tm, D), lambda b, i: (b, i, 0)),
        out_shape=jax.ShapeDtypeStruct((B, S, D), F32),
        scratch_shapes=[pltpu.VMEM((M, D), BF16), pltpu.VMEM((M, D), BF16)],
        compiler_params=pltpu.CompilerParams(
            dimension_semantics=("arbitrary", "arbitrary"), vmem_limit_bytes=VMEM_LIMIT),
        name="mix_mem",
    )(yf, yc, x, gf, gc, wo, gpost, gpre, wq, mem, gkv, wk, wv, wmo, gmpost)


def _mlp_kernel(x_ref, gpre_ref, w1_ref, w2_ref, gpost_ref, o_ref, *, ff_block, sub):
    halves = range(x_ref.shape[0] // sub)
    rows = [slice(h * sub, (h + 1) * sub) for h in halves]
    hs = [_rms(x_ref[rows[h], :], gpre_ref[...]).astype(BF16) for h in halves]
    ys = [None for _ in halves]
    for n in range(0, w1_ref.shape[1], ff_block):
        for h in halves:
            a = jnp.maximum(_dot(hs[h], w1_ref[:, n:n + ff_block]), 0.0)
            part = _dot((a * a).astype(BF16), w2_ref[n:n + ff_block, :])
            ys[h] = part if ys[h] is None else ys[h] + part
    for h in halves:
        o_ref[rows[h], :] = x_ref[rows[h], :] + _rms(ys[h], gpost_ref[...])


def _mlp(x, gpre, w1, w2, gpost, tm, ff_block, sub):
    R, D = x.shape
    d_ff = w1.shape[1]
    return pl.pallas_call(
        functools.partial(_mlp_kernel, ff_block=ff_block, sub=sub),
        grid=(R // tm,),
        in_specs=[
            pl.BlockSpec((tm, D), lambda i: (i, 0)),
            pl.BlockSpec((1, D), lambda i: (0, 0)),
            pl.BlockSpec((D, d_ff), lambda i: (0, 0), pipeline_mode=pl.Buffered(1)),
            pl.BlockSpec((d_ff, D), lambda i: (0, 0), pipeline_mode=pl.Buffered(1)),
            pl.BlockSpec((1, D), lambda i: (0, 0)),
        ],
        out_specs=pl.BlockSpec((tm, D), lambda i: (i, 0)),
        out_shape=jax.ShapeDtypeStruct((R, D), F32),
        compiler_params=pltpu.CompilerParams(
            dimension_semantics=("arbitrary",), vmem_limit_bytes=VMEM_LIMIT),
        name="mlp",
    )(x, gpre, w1, w2, gpost)


def kernel(x, mem, w_in, b_fgt, rel_bias, g_fox_out, g_chk_out, w_out, g_mix_pre, g_mix_post,
           g_mem_kv, w_mq, w_mk, w_mv, w_mo, g_mem_pre, g_mem_post,
           w_ff1, w_ff2, g_ff_pre, g_ff_post):
    B, S, D = x.shape
    depth = w_in.shape[0]
    heads = b_fgt.shape[1]
    d_fox = heads * HEAD_DIM
    d_chk = rel_bias.shape[1] * HEAD_DIM
    assert d_fox == d_chk and heads % 2 == 0 and heads <= LANES
    pairs = heads // 2
    row = lambda v: v.reshape(1, -1)

    for l in range(depth):
        scale = HEAD_DIM ** -0.5 * LOG2E
        wl = w_in[l]
        o_f = 3 * d_fox
        w_f = wl[:, o_f:o_f + heads]
        o_c = o_f + heads
        w_all = jnp.concatenate([
            wl[:, :d_fox] * scale, wl[:, d_fox:2 * d_fox],
            wl[:, o_c:o_c + d_chk] * scale, wl[:, o_c + d_chk:o_c + 2 * d_chk],
            jnp.pad(jnp.tile(w_f, (1, PIECES)), ((0, 0), (0, LANES - PIECES * heads))),
        ], axis=1).astype(BF16)
        wv_t = jnp.concatenate([wl[:, 2 * d_fox:o_f], wl[:, o_c + 2 * d_chk:]], axis=1).T.astype(BF16)
        b_f = jnp.pad(jnp.tile(b_fgt[l], PIECES), (0, LANES - PIECES * heads)).reshape(1, LANES)

        proj, vt, qa, ka = _inproj(x, row(g_mix_pre[l]), w_all, wv_t, b_f, heads, tm=1024, sub=256)
        fox_pp, chk_pp = 4, 4
        y_fox = _fox(proj, vt, qa, ka, heads, tq=512, rb=256, pp=fox_pp)
        mem_scale = (D // MEM_HEADS) ** -0.5 * LOG2E
        later = [w_out[l], w_mq[l], w_mk[l], w_mv[l], w_mo[l], w_ff1[l], w_ff2[l]]
        y_chk, (wo_b, wq_b, wk_b, wv_b, wmo_b, w1_b, w2_b) = _chunk(
            proj, vt, _chunk_bias_rows(rel_bias[l], 256, chk_pp), later,
            [1.0, mem_scale, 1.0, 1.0, 1.0, 1.0, 1.0], tq=256, rbs=2, pp=chk_pp,
            col0=2 * pairs // chk_pp, row0=pairs // chk_pp)

        x = _mix_mem(y_fox, y_chk, x, row(g_fox_out[l]), row(g_chk_out[l]), wo_b,
                     row(g_mix_post[l]), row(g_mem_pre[l]), wq_b,
                     mem, row(g_mem_kv[l]), wk_b, wv_b, wmo_b, row(g_mem_post[l]), tm=1024, sub=256)
        x = _mlp(x.reshape(B * S, D), row(g_ff_pre[l]), w1_b, w2_b,
                 row(g_ff_post[l]), tm=1024, ff_block=1024, sub=256).reshape(B, S, D)
    return x
```

```python
import functools
import math

import numpy as np
import jax
import jax.numpy as jnp
from jax import lax
from jax.experimental import pallas as pl
from jax.experimental.pallas import tpu as pltpu

F32 = jnp.float32
BF16 = jnp.bfloat16

EPS = 1e-6
HEAD_DIM = 64
PAIR = 2 * HEAD_DIM
CHUNK = 64
LEFT_CHUNKS = 8
MAX_REL = 128
MEM_HEADS = 4
NEG = -1e30
LOG2E = math.log2(math.e)
LANES = 128
VMEM_LIMIT = 56 * 1024 * 1024
PIECES = 3
SUM_LANE = (HEAD_DIM, 0)


def _rms(x, g):
    return x * lax.rsqrt(jnp.mean(x * x, axis=-1, keepdims=True) + EPS) * g


def _dot(a, b):
    return jnp.dot(a, b, preferred_element_type=F32)


def _dot_nt(a, b):
    return lax.dot_general(a, b, (((1,), (1,)), ((), ())), preferred_element_type=F32)


def _split3(x):
    hi = x.astype(BF16)
    r1 = x - hi.astype(F32)
    mid = r1.astype(BF16)
    lo = (r1 - mid.astype(F32)).astype(BF16)
    return hi, mid, lo


def _inproj_kernel(x_ref, g_ref, w_ref, bf_ref, tri_ref, e_ref, ones_ref,
                   proj_ref, vt_ref, qa_ref, ka_ref, carry_ref, *, heads, sub):
    @pl.when(pl.program_id(1) == 0)
    def _():
        carry_ref[...] = jnp.zeros_like(carry_ref)

    n_main = proj_ref.shape[2]
    n_val = vt_ref.shape[1]
    halves = range(x_ref.shape[1] // sub)
    rows = [slice(h * sub, (h + 1) * sub) for h in halves]
    lane = lax.broadcasted_iota(jnp.int32, (1, LANES), 1)
    hbs = [_rms(x_ref[0, rows[h], :], g_ref[...]).astype(BF16) for h in halves]
    fs = []
    for h in halves:
        res = _dot(hbs[h], w_ref[...])
        proj_ref[0, rows[h], :] = res[:, :n_main].astype(BF16)
        vt_ref[0, :, rows[h]] = res[:, n_main:n_main + n_val].T.astype(BF16)
        fs.append(res[:, n_main + n_val:] + bf_ref[...])
    carry = carry_ref[...]
    for h in halves:
        f = fs[h]
        logf = jnp.minimum(f, 0.0) - jnp.log1p(jnp.exp(-jnp.abs(f)))
        hi, mid, lo = _split3(logf)
        packed = jnp.where(lane < heads, hi, jnp.where(lane < 2 * heads, mid, lo))
        cum = _dot(tri_ref[...], packed)
        c = (cum + pltpu.roll(cum, LANES - heads, 1) + pltpu.roll(cum, LANES - 2 * heads, 1)) + carry
        carry = c[sub - 1:sub, :]
        pieces = jnp.concatenate(_split3(c * LOG2E), axis=-1)
        sel = _dot(pieces, e_ref[...])
        qa_ref[0, rows[h], :] = (sel[:, :LANES] + ones_ref[0:1, :]).astype(BF16)
        ka_ref[0, rows[h], :] = (ones_ref[1:2, :] - sel[:, LANES:]).astype(BF16)
    carry_ref[...] = carry


def _aug_constants(heads):
    k0 = PIECES * heads
    assert 2 * k0 <= LANES
    e = np.zeros((PIECES * LANES, 2 * LANES), np.float32)
    for h in range(heads):
        for r in range(PIECES):
            e[r * LANES + h, PIECES * h + r] = 1.0
            e[r * LANES + h, LANES + k0 + PIECES * h + r] = 1.0
    ones = np.zeros((2, LANES), np.float32)
    ones[0, k0:2 * k0] = 1.0
    ones[1, 0:k0] = 1.0
    return jnp.asarray(e, BF16), jnp.asarray(ones)


def _inproj(x, g, w_all, b_f, heads, n_main, n_val, tm, sub):
    B, S, D = x.shape
    assert w_all.shape[1] == n_main + n_val + LANES
    tri = jnp.asarray(np.tril(np.ones((sub, sub), np.float32)), BF16)
    e, ones = _aug_constants(heads)
    const = lambda r, c: pl.BlockSpec((r, c), lambda b, i: (0, 0), pipeline_mode=pl.Buffered(1))
    return pl.pallas_call(
        functools.partial(_inproj_kernel, heads=heads, sub=sub),
        grid=(B, S // tm),
        in_specs=[
            pl.BlockSpec((1, tm, D), lambda b, i: (b, i, 0)),
            const(1, D), const(D, n_main + n_val + LANES), const(1, LANES), const(sub, sub),
            const(PIECES * LANES, 2 * LANES), const(2, LANES),
        ],
        out_specs=[
            pl.BlockSpec((1, tm, n_main), lambda b, i: (b, i, 0)),
            pl.BlockSpec((1, n_val, tm), lambda b, i: (b, 0, i)),
            pl.BlockSpec((1, tm, LANES), lambda b, i: (b, i, 0)),
            pl.BlockSpec((1, tm, LANES), lambda b, i: (b, i, 0)),
        ],
        out_shape=[
            jax.ShapeDtypeStruct((B, S, n_main), BF16),
            jax.ShapeDtypeStruct((B, n_val, S), BF16),
            jax.ShapeDtypeStruct((B, S, LANES), BF16),
            jax.ShapeDtypeStruct((B, S, LANES), BF16),
        ],
        scratch_shapes=[pltpu.VMEM((1, LANES), F32)],
        compiler_params=pltpu.CompilerParams(
            dimension_semantics=("arbitrary", "arbitrary"), vmem_limit_bytes=VMEM_LIMIT),
        name="inproj",
    )(x, g, w_all, b_f, tri, e, ones)


def _head_mask(lane, hh):
    return lane < HEAD_DIM if hh == 0 else lane >= HEAD_DIM


def _value_aug_t(vt, feat, hh):
    ones = jnp.where(feat == SUM_LANE[hh], 1.0, 0.0).astype(vt.dtype)
    return jnp.where(_head_mask(feat, hh), vt, ones)


def _softmax_t(ss):
    m = functools.reduce(jnp.maximum, [jnp.max(s, axis=0, keepdims=True) for s in ss])
    return [jnp.exp2(s - m).astype(BF16) for s in ss]


def _pair_output_t(acc0, acc1, feat):
    l0 = acc0[SUM_LANE[0]:SUM_LANE[0] + 1, :]
    l1 = acc1[SUM_LANE[1]:SUM_LANE[1] + 1, :]
    return jnp.where(_head_mask(feat, 0), acc0 * (1.0 / l0), acc1 * (1.0 / l1)).T


def _fox_kernel(q_ref, qa_ref, k_ref, ka_ref, vt_ref, o_ref, s_ref, p_ref, *, tq, rb, pp, heads):
    grp = pl.program_id(1)
    lane = lax.broadcasted_iota(jnp.int32, (1, LANES), 1)
    feat = lax.broadcasted_iota(jnp.int32, (PAIR, 1), 0)
    k0 = PIECES * heads
    causal = lax.broadcasted_iota(jnp.int32, (rb, rb), 0) <= lax.broadcasted_iota(jnp.int32, (rb, rb), 1)
    slots = s_ref.shape[0]

    def attend(n):
        units = [(pi, r, hh) for pi in range(pp) for r in range(tq // rb) for hh in range(2)]

        def keys(r):
            past = n * tq + r * rb
            return past, past + rb

        def logits(u):
            pi, r, hh = units[u]
            past, total = keys(r)
            q = q_ref[0, r * rb:(r + 1) * rb, pi * PAIR:(pi + 1) * PAIR]
            qa = qa_ref[0, r * rb:(r + 1) * rb, :]
            a0 = PIECES * (2 * (grp * pp + pi) + hh)
            amask = ((lane >= a0) & (lane < a0 + PIECES)) | ((lane >= k0 + a0) & (lane < k0 + a0 + PIECES))
            q_aug = jnp.concatenate([jnp.where(_head_mask(lane, hh), q, jnp.zeros_like(q)),
                                     jnp.where(amask, qa, jnp.zeros_like(qa))], axis=-1)
            m = None
            for lo, hi in ([(0, past)] if past else []) + [(past, total)]:
                kaug = jnp.concatenate([k_ref[0, lo:hi, pi * PAIR:(pi + 1) * PAIR], ka_ref[0, lo:hi, :]], axis=-1)
                s = _dot_nt(kaug, q_aug)
                if lo == past:
                    s = jnp.where(causal, s, NEG)
                s_ref[u % slots, lo:hi, :] = s
                col_max = jnp.max(s, axis=0, keepdims=True)
                m = col_max if m is None else jnp.maximum(m, col_max)
            return m

        def weights(u, m):
            _, total = keys(units[u][1])
            p_ref[u % slots, 0:total, :] = jnp.exp2(s_ref[u % slots, 0:total, :] - m).astype(BF16)

        def values(u):
            pi, r, hh = units[u]
            _, total = keys(r)
            return _dot(_value_aug_t(vt_ref[0, pi * PAIR:(pi + 1) * PAIR, 0:total], feat, hh),
                        p_ref[u % slots, 0:total, :])

        ms, accs = {}, {}
        n_units = len(units)
        for t in range(0, n_units + 4, 2):
            for u in (t, t + 1):
                if u < n_units:
                    ms[u] = logits(u)
            for u in (t - 2, t - 1):
                if 0 <= u < n_units:
                    weights(u, ms.pop(u))
            for u in (t - 4, t - 3):
                if 0 <= u < n_units:
                    accs[units[u]] = values(u)
            if 0 <= t - 4 < n_units:
                pi, r, _ = units[t - 4]
                o_ref[0, r * rb:(r + 1) * rb, pi * PAIR:(pi + 1) * PAIR] = _pair_output_t(
                    accs.pop((pi, r, 0)), accs.pop((pi, r, 1)), feat).astype(o_ref.dtype)

    for n in range(k_ref.shape[1] // tq):
        pl.when(pl.program_id(2) == n)(functools.partial(attend, n))


def _fox(proj, vt, qa, ka, heads, tq, rb, pp):
    B, S, _ = proj.shape
    groups = heads // 2 // pp
    w = pp * PAIR
    return pl.pallas_call(
        functools.partial(_fox_kernel, tq=tq, rb=rb, pp=pp, heads=heads),
        grid=(B, groups, S // tq),
        in_specs=[
            pl.BlockSpec((1, tq, w), lambda b, g, i: (b, i, g)),
            pl.BlockSpec((1, tq, LANES), lambda b, g, i: (b, i, 0)),
            pl.BlockSpec((1, S, w), lambda b, g, i: (b, 0, groups + g)),
            pl.BlockSpec((1, S, LANES), lambda b, g, i: (b, 0, 0)),
            pl.BlockSpec((1, w, S), lambda b, g, i: (b, g, 0)),
        ],
        out_specs=pl.BlockSpec((1, tq, w), lambda b, g, i: (b, i, g)),
        out_shape=jax.ShapeDtypeStruct((B, S, heads * HEAD_DIM), BF16),
        scratch_shapes=[pltpu.VMEM((6, S, rb), F32), pltpu.VMEM((6, S, rb), BF16)],
        compiler_params=pltpu.CompilerParams(
            dimension_semantics=("arbitrary", "arbitrary", "arbitrary"), vmem_limit_bytes=VMEM_LIMIT),
        name="fox_attn",
    )(proj, qa, proj, ka, vt)


def _chunk_kernel(q_ref, k_ref, vt_ref, g_ref, *refs, tq, rbs, pp, scales):
    n_w = len(scales)
    w_refs, o_ref, wb_refs, bias_ref = refs[:n_w], refs[n_w], refs[n_w + 1:2 * n_w + 1], refs[2 * n_w + 1]
    for w_ref, wb_ref, scale in zip(w_refs, wb_refs, scales):
        w = w_ref[...]
        wb_ref[...] = (w if scale == 1.0 else w * scale).astype(wb_ref.dtype)

    i = pl.program_id(2)
    lane = lax.broadcasted_iota(jnp.int32, (1, LANES), 1)
    feat = lax.broadcasted_iota(jnp.int32, (PAIR, 1), 0)

    @pl.when((pl.program_id(1) == 0) & (i == 0))
    def _():
        k_chunk = lax.broadcasted_iota(jnp.int32, (tq, tq), 0) // CHUNK
        q_chunk = lax.broadcasted_iota(jnp.int32, (tq, tq), 1) // CHUNK
        for which in range(3):
            back = which * (tq // CHUNK) + q_chunk - k_chunk
            vis = (back >= 0) & (back <= LEFT_CHUNKS)
            for hd in range(2 * pp):
                g = jnp.broadcast_to(g_ref[0, which, hd], (tq, 2 * tq))
                toep = pltpu.roll(g, 0, 1, stride=1, stride_axis=0)[:, :tq]
                bias_ref[hd, (2 - which) * tq:(3 - which) * tq, :] = jnp.where(vis, toep, NEG)

    def attend(first_step):
        units = [(pi, blk, hh) for pi in range(pp) for blk in range(rbs) for hh in range(2)]

        def key_span(blk):
            if first_step:
                n_blocks = min(3, blk + 1)
                return (blk - (n_blocks - 1)) * tq, n_blocks * tq
            return pl.multiple_of((i * rbs + blk - 2) * tq, tq), 3 * tq

        def logits(pi, blk, hh):
            cols = slice(pi * PAIR, (pi + 1) * PAIR)
            ks, span = key_span(blk)
            q = q_ref[0, blk * tq:(blk + 1) * tq, cols]
            qh = jnp.where(_head_mask(lane, hh), q, jnp.zeros_like(q))
            return _dot_nt(k_ref[0, pl.ds(ks, span), cols], qh) + bias_ref[2 * pi + hh, 3 * tq - span:, :]

        def weighted_values(pi, blk, hh, p):
            ks, span = key_span(blk)
            return _dot(_value_aug_t(vt_ref[0, pi * PAIR:(pi + 1) * PAIR, pl.ds(ks, span)], feat, hh), p)

        ss, ps, accs = {}, {}, {}
        n_units = len(units)
        for t in range(0, n_units + 4, 2):
            for u in (t, t + 1):
                if u < n_units:
                    ss[u] = logits(*units[u])
            for u in (t - 2, t - 1):
                if 0 <= u < n_units:
                    ps[u] = _softmax_t([ss.pop(u)])[0]
            for u in (t - 4, t - 3):
                if 0 <= u < n_units:
                    accs[units[u]] = weighted_values(*units[u], ps.pop(u))
            if 0 <= t - 4 < n_units:
                pi, blk, _ = units[t - 4]
                o_ref[0, blk * tq:(blk + 1) * tq, pi * PAIR:(pi + 1) * PAIR] = _pair_output_t(
                    accs.pop((pi, blk, 0)), accs.pop((pi, blk, 1)), feat).astype(o_ref.dtype)

    pl.when(i == 0)(functools.partial(attend, True))
    pl.when(i > 0)(functools.partial(attend, False))


def _chunk_bias_rows(rel_table, tq, pp):
    heads = rel_table.shape[0]
    m = np.arange(2 * tq)
    offset = np.where(m < tq, m, m - 2 * tq)
    idx = np.stack([np.clip(which * tq + offset, -MAX_REL, MAX_REL) + MAX_REL for which in range(3)])
    rows = rel_table.astype(F32)[:, idx] * LOG2E
    return rows.reshape(heads // (2 * pp), 2 * pp, 3, 1, 2 * tq).transpose(0, 2, 1, 3, 4)


def _chunk(proj, vt, bias_rows, weights, scales, tq, rbs, pp, col0, row0):
    B, S, _ = proj.shape
    groups = bias_rows.shape[0]
    w = pp * PAIR
    steps = S // (rbs * tq)
    assert tq % CHUNK == 0 and LEFT_CHUNKS * CHUNK <= 2 * tq, "band must fit in three key blocks"
    assert rbs >= 2, "after the first step every query block must have two key blocks behind it"
    assert groups == 1, "each weight slice is cast at exactly one grid step"
    slice_spec = lambda wt: pl.BlockSpec((wt.shape[0] // (B * steps), wt.shape[1]),
                                         lambda g, b, i: (b * steps + i, 0))
    out = pl.pallas_call(
        functools.partial(_chunk_kernel, tq=tq, rbs=rbs, pp=pp, scales=tuple(scales)),
        grid=(groups, B, steps),
        in_specs=[
            pl.BlockSpec((1, rbs * tq, w), lambda g, b, i: (b, i, col0 + g)),
            pl.BlockSpec((1, S, w), lambda g, b, i: (b, 0, col0 + groups + g)),
            pl.BlockSpec((1, w, S), lambda g, b, i: (b, row0 + g, 0)),
            pl.BlockSpec((1, 3, 2 * pp, 1, 2 * tq), lambda g, b, i: (g, 0, 0, 0, 0)),
        ] + [slice_spec(wt) for wt in weights],
        out_specs=[pl.BlockSpec((1, rbs * tq, w), lambda g, b, i: (b, i, g))]
        + [slice_spec(wt) for wt in weights],
        out_shape=[jax.ShapeDtypeStruct((B, S, groups * w), BF16)]
        + [jax.ShapeDtypeStruct(wt.shape, BF16) for wt in weights],
        scratch_shapes=[pltpu.VMEM((2 * pp, 3 * tq, tq), F32)],
        compiler_params=pltpu.CompilerParams(
            dimension_semantics=("arbitrary", "arbitrary", "arbitrary"), vmem_limit_bytes=VMEM_LIMIT),
        name="chunk_attn",
    )(proj, proj, vt, bias_rows, *weights)
    return out[0], out[1:]


def _mix_mem_kernel(yf_ref, yc_ref, x_ref, gf_ref, gc_ref, wo_ref, gpost_ref, gpre_ref, wq_ref,
                    mem_ref, gkv_ref, wk_ref, wv_ref, wmo_ref, gmpost_ref, o_ref, mk_ref, mv_ref, *, sub):
    @pl.when(pl.program_id(1) == 0)
    def _():
        mb = _rms(mem_ref[0], gkv_ref[...]).astype(BF16)
        mk_ref[...] = _dot(mb, wk_ref[...]).astype(BF16)
        mv_ref[...] = _dot(mb, wv_ref[...]).astype(BF16)

    d_fox = yf_ref.shape[2]
    halves = range(x_ref.shape[1] // sub)
    rows = [slice(h * sub, (h + 1) * sub) for h in halves]

    def mix(r):
        yf = _rms(yf_ref[0, r, :].astype(F32), gf_ref[...]).astype(BF16)
        yc = _rms(yc_ref[0, r, :].astype(F32), gc_ref[...]).astype(BF16)
        return _dot(yf, wo_ref[0:d_fox, :]) + _dot(yc, wo_ref[d_fox:, :])

    def query(x1):
        return _dot(_rms(x1, gpre_ref[...]).astype(BF16), wq_ref[...]).astype(BF16)

    def attend(q):
        dh = q.shape[1] // MEM_HEADS
        outs = []
        for hh in range(MEM_HEADS):
            sl = slice(hh * dh, (hh + 1) * dh)
            s = _dot_nt(q[:, sl], mk_ref[:, sl])
            pr = jnp.exp2(s - jnp.max(s, axis=-1, keepdims=True))
            inv = 1.0 / jnp.sum(pr, axis=-1, keepdims=True)
            outs.append((_dot(pr.astype(BF16), mv_ref[:, sl]) * inv).astype(BF16))
        return jnp.concatenate(outs, axis=-1)

    ys = [mix(rows[h]) for h in halves]
    x1s = [x_ref[0, rows[h], :] + _rms(ys[h], gpost_ref[...]) for h in halves]
    qs = [query(x1s[h]) for h in halves]
    os = [attend(qs[h]) for h in halves]
    y2s = [_dot(os[h], wmo_ref[...]) for h in halves]
    for h in halves:
        o_ref[0, rows[h], :] = x1s[h] + _rms(y2s[h], gmpost_ref[...])


def _mix_mem(yf, yc, x, gf, gc, wo, gpost, gpre, wq, mem, gkv, wk, wv, wmo, gmpost, tm, sub):
    B, S, D = x.shape
    M = mem.shape[1]
    dg = yf.shape[2]
    vec = lambda n: pl.BlockSpec((1, n), lambda b, i: (0, 0))
    mat = lambda r, c: pl.BlockSpec((r, c), lambda b, i: (0, 0), pipeline_mode=pl.Buffered(1))
    return pl.pallas_call(
        functools.partial(_mix_mem_kernel, sub=sub),
        grid=(B, S // tm),
        in_specs=[
            pl.BlockSpec((1, tm, dg), lambda b, i: (b, i, 0)),
            pl.BlockSpec((1, tm, dg), lambda b, i: (b, i, 0)),
            pl.BlockSpec((1, tm, D), lambda b, i: (b, i, 0)),
            vec(dg), vec(dg), mat(2 * dg, D), vec(D),
            vec(D), mat(D, D),
            pl.BlockSpec((1, M, D), lambda b, i: (b, 0, 0)),
            vec(D), mat(D, D), mat(D, D),
            mat(D, D), vec(D),
        ],
        out_specs=pl.BlockSpec((1, tm, D), lambda b, i: (b, i, 0)),
        out_shape=jax.ShapeDtypeStruct((B, S, D), F32),
        scratch_shapes=[pltpu.VMEM((M, D), BF16), pltpu.VMEM((M, D), BF16)],
        compiler_params=pltpu.CompilerParams(
            dimension_semantics=("arbitrary", "arbitrary"), vmem_limit_bytes=VMEM_LIMIT),
        name="mix_mem",
    )(yf, yc, x, gf, gc, wo, gpost, gpre, wq, mem, gkv, wk, wv, wmo, gmpost)


def _mlp_kernel(x_ref, gpre_ref, w1_ref, w2_ref, gpost_ref, o_ref, *, ff_block, sub):
    halves = range(x_ref.shape[0] // sub)
    rows = [slice(h * sub, (h + 1) * sub) for h in halves]
    hs = [_rms(x_ref[rows[h], :], gpre_ref[...]).astype(BF16) for h in halves]
    ys = [None for _ in halves]
    for n in range(0, w1_ref.shape[1], ff_block):
        for h in halves:
            a = jnp.maximum(_dot(hs[h], w1_ref[:, n:n + ff_block]), 0.0)
            part = _dot((a * a).astype(BF16), w2_ref[n:n + ff_block, :])
            ys[h] = part if ys[h] is None else ys[h] + part
    for h in halves:
        o_ref[rows[h], :] = x_ref[rows[h], :] + _rms(ys[h], gpost_ref[...])


def _mlp(x, gpre, w1, w2, gpost, tm, ff_block, sub):
    R, D = x.shape
    d_ff = w1.shape[1]
    return pl.pallas_call(
        functools.partial(_mlp_kernel, ff_block=ff_block, sub=sub),
        grid=(R // tm,),
        in_specs=[
            pl.BlockSpec((tm, D), lambda i: (i, 0)),
            pl.BlockSpec((1, D), lambda i: (0, 0)),
            pl.BlockSpec((D, d_ff), lambda i: (0, 0), pipeline_mode=pl.Buffered(1)),
            pl.BlockSpec((d_ff, D), lambda i: (0, 0), pipeline_mode=pl.Buffered(1)),
            pl.BlockSpec((1, D), lambda i: (0, 0)),
        ],
        out_specs=pl.BlockSpec((tm, D), lambda i: (i, 0)),
        out_shape=jax.ShapeDtypeStruct((R, D), F32),
        compiler_params=pltpu.CompilerParams(
            dimension_semantics=("arbitrary",), vmem_limit_bytes=VMEM_LIMIT),
        name="mlp",
    )(x, gpre, w1, w2, gpost)


def kernel(x, mem, w_in, b_fgt, rel_bias, g_fox_out, g_chk_out, w_out, g_mix_pre, g_mix_post,
           g_mem_kv, w_mq, w_mk, w_mv, w_mo, g_mem_pre, g_mem_post,
           w_ff1, w_ff2, g_ff_pre, g_ff_post):
    B, S, D = x.shape
    depth = w_in.shape[0]
    heads = b_fgt.shape[1]
    d_fox = heads * HEAD_DIM
    d_chk = rel_bias.shape[1] * HEAD_DIM
    assert d_fox == d_chk and heads % 2 == 0 and heads <= LANES
    pairs = heads // 2
    row = lambda v: v.reshape(1, -1)

    for l in range(depth):
        scale = HEAD_DIM ** -0.5 * LOG2E
        wl = w_in[l]
        o_f = 3 * d_fox
        w_f = wl[:, o_f:o_f + heads]
        o_c = o_f + heads
        w_all = jnp.concatenate([
            wl[:, :d_fox] * scale, wl[:, d_fox:2 * d_fox],
            wl[:, o_c:o_c + d_chk] * scale, wl[:, o_c + d_chk:o_c + 2 * d_chk],
            wl[:, 2 * d_fox:o_f], wl[:, o_c + 2 * d_chk:],
            jnp.pad(jnp.tile(w_f, (1, PIECES)), ((0, 0), (0, LANES - PIECES * heads))),
        ], axis=1).astype(BF16)
        b_f = jnp.pad(jnp.tile(b_fgt[l], PIECES), (0, LANES - PIECES * heads)).reshape(1, LANES)

        proj, vt, qa, ka = _inproj(x, row(g_mix_pre[l]), w_all, b_f, heads, n_main=2 * (d_fox + d_chk),
                                   n_val=d_fox + d_chk, tm=1024, sub=256)
        fox_pp, chk_pp = 4, 4
        y_fox = _fox(proj, vt, qa, ka, heads, tq=512, rb=256, pp=fox_pp)
        mem_scale = (D // MEM_HEADS) ** -0.5 * LOG2E
        later = [w_out[l], w_mq[l], w_mk[l], w_mv[l], w_mo[l], w_ff1[l], w_ff2[l]]
        y_chk, (wo_b, wq_b, wk_b, wv_b, wmo_b, w1_b, w2_b) = _chunk(
            proj, vt, _chunk_bias_rows(rel_bias[l], 256, chk_pp), later,
            [1.0, mem_scale, 1.0, 1.0, 1.0, 1.0, 1.0], tq=256, rbs=2, pp=chk_pp,
            col0=2 * pairs // chk_pp, row0=pairs // chk_pp)

        x = _mix_mem(y_fox, y_chk, x, row(g_fox_out[l]), row(g_chk_out[l]), wo_b,
                     row(g_mix_post[l]), row(g_mem_pre[l]), wq_b,
                     mem, row(g_mem_kv[l]), wk_b, wv_b, wmo_b, row(g_mem_post[l]), tm=1024, sub=256)
        x = _mlp(x.reshape(B * S, D), row(g_ff_pre[l]), w1_b, w2_b,
                 row(g_ff_post[l]), tm=1024, ff_block=1024, sub=256).reshape(B, S, D)
    return x
```

```python
import functools
import math

import numpy as np
import jax
import jax.numpy as jnp
from jax import lax
from jax.experimental import pallas as pl
from jax.experimental.pallas import tpu as pltpu

F32 = jnp.float32
BF16 = jnp.bfloat16

EPS = 1e-6
HEAD_DIM = 64
PAIR = 2 * HEAD_DIM
CHUNK = 64
LEFT_CHUNKS = 8
MAX_REL = 128
MEM_HEADS = 4
NEG = -1e30
LOG2E = math.log2(math.e)
LANES = 128
VMEM_LIMIT = 56 * 1024 * 1024
PIECES = 3
SUM_LANE = (HEAD_DIM, 0)


def _rms(x, g):
    return x * lax.rsqrt(jnp.mean(x * x, axis=-1, keepdims=True) + EPS) * g


def _dot(a, b):
    return jnp.dot(a, b, preferred_element_type=F32)


def _dot_nt(a, b):
    return lax.dot_general(a, b, (((1,), (1,)), ((), ())), preferred_element_type=F32)


def _split3(x):
    hi = x.astype(BF16)
    r1 = x - hi.astype(F32)
    mid = r1.astype(BF16)
    lo = (r1 - mid.astype(F32)).astype(BF16)
    return hi, mid, lo


def _wprep_kernel(wt_ref, o_ref, *, d_fox, d_chk, heads, scale):
    o_f = 3 * d_fox
    o_c = o_f + heads
    row = lambda lo, n: wt_ref[0, lo:lo + n, :]
    gates = row(o_f, heads)
    pad = jnp.zeros((LANES - PIECES * heads, gates.shape[1]), gates.dtype)
    o_ref[...] = jnp.concatenate(
        [row(0, d_fox) * scale, row(d_fox, d_fox), row(o_c, d_chk) * scale, row(o_c + d_chk, d_chk)]
        + [gates] * PIECES + [pad, row(2 * d_fox, d_fox), row(o_c + 2 * d_chk, d_chk)],
        axis=0).astype(o_ref.dtype)


def _wprep(wt, layer, d_fox, d_chk, heads, scale, tc):
    _, n_in, D = wt.shape
    n_out = 3 * (d_fox + d_chk) + LANES
    assert heads % 8 == 0, "row pieces must stay sublane-tile aligned"
    return pl.pallas_call(
        functools.partial(_wprep_kernel, d_fox=d_fox, d_chk=d_chk, heads=heads, scale=scale),
        grid=(D // tc,),
        in_specs=[pl.BlockSpec((1, n_in, tc), lambda i: (layer, 0, i))],
        out_specs=pl.BlockSpec((n_out, tc), lambda i: (0, i)),
        out_shape=jax.ShapeDtypeStruct((n_out, D), BF16),
        compiler_params=pltpu.CompilerParams(dimension_semantics=("arbitrary",), vmem_limit_bytes=VMEM_LIMIT),
        name="w_prep",
    )(wt)


def _inproj_kernel(x_ref, g_ref, wt_ref, bf_ref, tri_ref, e_ref, ones_ref,
                   proj_ref, vt_ref, qa_ref, ka_ref, carry_ref, *, heads, sub):
    @pl.when(pl.program_id(1) == 0)
    def _():
        carry_ref[...] = jnp.zeros_like(carry_ref)

    n_main = proj_ref.shape[2]
    halves = range(x_ref.shape[1] // sub)
    rows = [slice(h * sub, (h + 1) * sub) for h in halves]
    lane = lax.broadcasted_iota(jnp.int32, (1, LANES), 1)
    hbs = [_rms(x_ref[0, rows[h], :], g_ref[...]).astype(BF16) for h in halves]
    fs = []
    for h in halves:
        res = _dot_nt(hbs[h], wt_ref[0:n_main + LANES, :])
        proj_ref[0, rows[h], :] = res[:, :n_main].astype(BF16)
        vt_ref[0, :, rows[h]] = _dot_nt(wt_ref[n_main + LANES:, :], hbs[h]).astype(BF16)
        fs.append(res[:, n_main:] + bf_ref[...])
    carry = carry_ref[...]
    for h in halves:
        f = fs[h]
        logf = jnp.minimum(f, 0.0) - jnp.log1p(jnp.exp(-jnp.abs(f)))
        hi, mid, lo = _split3(logf)
        packed = jnp.where(lane < heads, hi, jnp.where(lane < 2 * heads, mid, lo))
        cum = _dot(tri_ref[...], packed)
        c = (cum + pltpu.roll(cum, LANES - heads, 1) + pltpu.roll(cum, LANES - 2 * heads, 1)) + carry
        carry = c[sub - 1:sub, :]
        pieces = jnp.concatenate(_split3(c * LOG2E), axis=-1)
        sel = _dot(pieces, e_ref[...])
        qa_ref[0, rows[h], :] = (sel[:, :LANES] + ones_ref[0:1, :]).astype(BF16)
        ka_ref[0, rows[h], :] = (ones_ref[1:2, :] - sel[:, LANES:]).astype(BF16)
    carry_ref[...] = carry


def _aug_constants(heads):
    k0 = PIECES * heads
    assert 2 * k0 <= LANES
    e = np.zeros((PIECES * LANES, 2 * LANES), np.float32)
    for h in range(heads):
        for r in range(PIECES):
            e[r * LANES + h, PIECES * h + r] = 1.0
            e[r * LANES + h, LANES + k0 + PIECES * h + r] = 1.0
    ones = np.zeros((2, LANES), np.float32)
    ones[0, k0:2 * k0] = 1.0
    ones[1, 0:k0] = 1.0
    return jnp.asarray(e, BF16), jnp.asarray(ones)


def _inproj(x, g, wt_all, b_f, heads, n_val, tm, sub):
    B, S, D = x.shape
    n_main = wt_all.shape[0] - n_val - LANES
    tri = jnp.asarray(np.tril(np.ones((sub, sub), np.float32)), BF16)
    e, ones = _aug_constants(heads)
    const = lambda r, c: pl.BlockSpec((r, c), lambda b, i: (0, 0), pipeline_mode=pl.Buffered(1))
    return pl.pallas_call(
        functools.partial(_inproj_kernel, heads=heads, sub=sub),
        grid=(B, S // tm),
        in_specs=[
            pl.BlockSpec((1, tm, D), lambda b, i: (b, i, 0)),
            const(1, D), const(n_main + LANES + n_val, D), const(1, LANES), const(sub, sub),
            const(PIECES * LANES, 2 * LANES), const(2, LANES),
        ],
        out_specs=[
            pl.BlockSpec((1, tm, n_main), lambda b, i: (b, i, 0)),
            pl.BlockSpec((1, n_val, tm), lambda b, i: (b, 0, i)),
            pl.BlockSpec((1, tm, LANES), lambda b, i: (b, i, 0)),
            pl.BlockSpec((1, tm, LANES), lambda b, i: (b, i, 0)),
        ],
        out_shape=[
            jax.ShapeDtypeStruct((B, S, n_main), BF16),
            jax.ShapeDtypeStruct((B, n_val, S), BF16),
            jax.ShapeDtypeStruct((B, S, LANES), BF16),
            jax.ShapeDtypeStruct((B, S, LANES), BF16),
        ],
        scratch_shapes=[pltpu.VMEM((1, LANES), F32)],
        compiler_params=pltpu.CompilerParams(
            dimension_semantics=("arbitrary", "arbitrary"), vmem_limit_bytes=VMEM_LIMIT),
        name="inproj",
    )(x, g, wt_all, b_f, tri, e, ones)


def _head_mask(lane, hh):
    return lane < HEAD_DIM if hh == 0 else lane >= HEAD_DIM


def _value_aug_t(vt, feat, hh):
    ones = jnp.where(feat == SUM_LANE[hh], 1.0, 0.0).astype(vt.dtype)
    return jnp.where(_head_mask(feat, hh), vt, ones)


def _softmax_t(ss):
    m = functools.reduce(jnp.maximum, [jnp.max(s, axis=0, keepdims=True) for s in ss])
    return [jnp.exp2(s - m).astype(BF16) for s in ss]


def _pair_output_t(acc0, acc1, feat):
    l0 = acc0[SUM_LANE[0]:SUM_LANE[0] + 1, :]
    l1 = acc1[SUM_LANE[1]:SUM_LANE[1] + 1, :]
    return jnp.where(_head_mask(feat, 0), acc0 * (1.0 / l0), acc1 * (1.0 / l1)).T


def _fox_kernel(q_ref, qa_ref, k_ref, ka_ref, vt_ref, o_ref, s_ref, p_ref, *, tq, rb, pp, heads):
    grp = pl.program_id(1)
    lane = lax.broadcasted_iota(jnp.int32, (1, LANES), 1)
    feat = lax.broadcasted_iota(jnp.int32, (PAIR, 1), 0)
    k0 = PIECES * heads
    causal = lax.broadcasted_iota(jnp.int32, (rb, rb), 0) <= lax.broadcasted_iota(jnp.int32, (rb, rb), 1)
    slots = s_ref.shape[0]

    def attend(n):
        units = [(pi, r, hh) for pi in range(pp) for r in range(tq // rb) for hh in range(2)]

        def keys(r):
            past = n * tq + r * rb
            return past, past + rb

        def logits(u):
            pi, r, hh = units[u]
            past, total = keys(r)
            q = q_ref[0, r * rb:(r + 1) * rb, pi * PAIR:(pi + 1) * PAIR]
            qa = qa_ref[0, r * rb:(r + 1) * rb, :]
            a0 = PIECES * (2 * (grp * pp + pi) + hh)
            amask = ((lane >= a0) & (lane < a0 + PIECES)) | ((lane >= k0 + a0) & (lane < k0 + a0 + PIECES))
            q_aug = jnp.concatenate([jnp.where(_head_mask(lane, hh), q, jnp.zeros_like(q)),
                                     jnp.where(amask, qa, jnp.zeros_like(qa))], axis=-1)
            m = None
            for lo, hi in ([(0, past)] if past else []) + [(past, total)]:
                kaug = jnp.concatenate([k_ref[0, lo:hi, pi * PAIR:(pi + 1) * PAIR], ka_ref[0, lo:hi, :]], axis=-1)
                s = _dot_nt(kaug, q_aug)
                if lo == past:
                    s = jnp.where(causal, s, NEG)
                s_ref[u % slots, lo:hi, :] = s
                col_max = jnp.max(s, axis=0, keepdims=True)
                m = col_max if m is None else jnp.maximum(m, col_max)
            return m

        def weights(u, m):
            _, total = keys(units[u][1])
            p_ref[u % slots, 0:total, :] = jnp.exp2(s_ref[u % slots, 0:total, :] - m).astype(BF16)

        def values(u):
            pi, r, hh = units[u]
            _, total = keys(r)
            return _dot(_value_aug_t(vt_ref[0, pi * PAIR:(pi + 1) * PAIR, 0:total], feat, hh),
                        p_ref[u % slots, 0:total, :])

        ms, accs = {}, {}
        n_units = len(units)
        for t in range(0, n_units + 4, 2):
            for u in (t, t + 1):
                if u < n_units:
                    ms[u] = logits(u)
            for u in (t - 2, t - 1):
                if 0 <= u < n_units:
                    weights(u, ms.pop(u))
            for u in (t - 4, t - 3):
                if 0 <= u < n_units:
                    accs[units[u]] = values(u)
            if 0 <= t - 4 < n_units:
                pi, r, _ = units[t - 4]
                o_ref[0, r * rb:(r + 1) * rb, pi * PAIR:(pi + 1) * PAIR] = _pair_output_t(
                    accs.pop((pi, r, 0)), accs.pop((pi, r, 1)), feat).astype(o_ref.dtype)

    for n in range(k_ref.shape[1] // tq):
        pl.when(pl.program_id(2) == n)(functools.partial(attend, n))


def _fox(proj, vt, qa, ka, heads, tq, rb, pp):
    B, S, _ = proj.shape
    groups = heads // 2 // pp
    w = pp * PAIR
    return pl.pallas_call(
        functools.partial(_fox_kernel, tq=tq, rb=rb, pp=pp, heads=heads),
        grid=(B, groups, S // tq),
        in_specs=[
            pl.BlockSpec((1, tq, w), lambda b, g, i: (b, i, g)),
            pl.BlockSpec((1, tq, LANES), lambda b, g, i: (b, i, 0)),
            pl.BlockSpec((1, S, w), lambda b, g, i: (b, 0, groups + g)),
            pl.BlockSpec((1, S, LANES), lambda b, g, i: (b, 0, 0)),
            pl.BlockSpec((1, w, S), lambda b, g, i: (b, g, 0)),
        ],
        out_specs=pl.BlockSpec((1, tq, w), lambda b, g, i: (b, i, g)),
        out_shape=jax.ShapeDtypeStruct((B, S, heads * HEAD_DIM), BF16),
        scratch_shapes=[pltpu.VMEM((6, S, rb), F32), pltpu.VMEM((6, S, rb), BF16)],
        compiler_params=pltpu.CompilerParams(
            dimension_semantics=("arbitrary", "arbitrary", "arbitrary"), vmem_limit_bytes=VMEM_LIMIT),
        name="fox_attn",
    )(proj, qa, proj, ka, vt)


def _chunk_kernel(q_ref, k_ref, vt_ref, g_ref, *refs, tq, rbs, pp, scales):
    n_w = len(scales)
    w_refs, o_ref, wb_refs, bias_ref = refs[:n_w], refs[n_w], refs[n_w + 1:2 * n_w + 1], refs[2 * n_w + 1]
    for w_ref, wb_ref, scale in zip(w_refs, wb_refs, scales):
        w = w_ref[...]
        wb_ref[...] = (w if scale == 1.0 else w * scale).astype(wb_ref.dtype)

    i = pl.program_id(2)
    lane = lax.broadcasted_iota(jnp.int32, (1, LANES), 1)
    feat = lax.broadcasted_iota(jnp.int32, (PAIR, 1), 0)

    @pl.when((pl.program_id(1) == 0) & (i == 0))
    def _():
        k_chunk = lax.broadcasted_iota(jnp.int32, (tq, tq), 0) // CHUNK
        q_chunk = lax.broadcasted_iota(jnp.int32, (tq, tq), 1) // CHUNK
        for which in range(3):
            back = which * (tq // CHUNK) + q_chunk - k_chunk
            vis = (back >= 0) & (back <= LEFT_CHUNKS)
            for hd in range(2 * pp):
                g = jnp.broadcast_to(g_ref[0, which, hd], (tq, 2 * tq))
                toep = pltpu.roll(g, 0, 1, stride=1, stride_axis=0)[:, :tq]
                bias_ref[hd, (2 - which) * tq:(3 - which) * tq, :] = jnp.where(vis, toep, NEG)

    def attend(first_step):
        units = [(pi, blk, hh) for pi in range(pp) for blk in range(rbs) for hh in range(2)]

        def key_span(blk):
            if first_step:
                n_blocks = min(3, blk + 1)
                return (blk - (n_blocks - 1)) * tq, n_blocks * tq
            return pl.multiple_of((i * rbs + blk - 2) * tq, tq), 3 * tq

        def logits(pi, blk, hh):
            cols = slice(pi * PAIR, (pi + 1) * PAIR)
            ks, span = key_span(blk)
            q = q_ref[0, blk * tq:(blk + 1) * tq, cols]
            qh = jnp.where(_head_mask(lane, hh), q, jnp.zeros_like(q))
            return _dot_nt(k_ref[0, pl.ds(ks, span), cols], qh) + bias_ref[2 * pi + hh, 3 * tq - span:, :]

        def weighted_values(pi, blk, hh, p):
            ks, span = key_span(blk)
            return _dot(_value_aug_t(vt_ref[0, pi * PAIR:(pi + 1) * PAIR, pl.ds(ks, span)], feat, hh), p)

        ss, ps, accs = {}, {}, {}
        n_units = len(units)
        for t in range(0, n_units + 4, 2):
            for u in (t, t + 1):
                if u < n_units:
                    ss[u] = logits(*units[u])
            for u in (t - 2, t - 1):
                if 0 <= u < n_units:
                    ps[u] = _softmax_t([ss.pop(u)])[0]
            for u in (t - 4, t - 3):
                if 0 <= u < n_units:
                    accs[units[u]] = weighted_values(*units[u], ps.pop(u))
            if 0 <= t - 4 < n_units:
                pi, blk, _ = units[t - 4]
                o_ref[0, blk * tq:(blk + 1) * tq, pi * PAIR:(pi + 1) * PAIR] = _pair_output_t(
                    accs.pop((pi, blk, 0)), accs.pop((pi, blk, 1)), feat).astype(o_ref.dtype)

    pl.when(i == 0)(functools.partial(attend, True))
    pl.when(i > 0)(functools.partial(attend, False))


def _chunk_bias_rows(rel_table, tq, pp):
    heads = rel_table.shape[0]
    m = np.arange(2 * tq)
    offset = np.where(m < tq, m, m - 2 * tq)
    idx = np.stack([np.clip(which * tq + offset, -MAX_REL, MAX_REL) + MAX_REL for which in range(3)])
    rows = rel_table.astype(F32)[:, idx] * LOG2E
    return rows.reshape(heads // (2 * pp), 2 * pp, 3, 1, 2 * tq).transpose(0, 2, 1, 3, 4)


def _chunk(proj, vt, bias_rows, weights, scales, tq, rbs, pp, col0, row0):
    B, S, _ = proj.shape
    groups = bias_rows.shape[0]
    w = pp * PAIR
    steps = S // (rbs * tq)
    assert tq % CHUNK == 0 and LEFT_CHUNKS * CHUNK <= 2 * tq, "band must fit in three key blocks"
    assert rbs >= 2, "after the first step every query block must have two key blocks behind it"
    assert groups == 1, "each weight slice is cast at exactly one grid step"
    slice_spec = lambda wt: pl.BlockSpec((wt.shape[0] // (B * steps), wt.shape[1]),
                                         lambda g, b, i: (b * steps + i, 0))
    out = pl.pallas_call(
        functools.partial(_chunk_kernel, tq=tq, rbs=rbs, pp=pp, scales=tuple(scales)),
        grid=(groups, B, steps),
        in_specs=[
            pl.BlockSpec((1, rbs * tq, w), lambda g, b, i: (b, i, col0 + g)),
            pl.BlockSpec((1, S, w), lambda g, b, i: (b, 0, col0 + groups + g)),
            pl.BlockSpec((1, w, S), lambda g, b, i: (b, row0 + g, 0)),
            pl.BlockSpec((1, 3, 2 * pp, 1, 2 * tq), lambda g, b, i: (g, 0, 0, 0, 0)),
        ] + [slice_spec(wt) for wt in weights],
        out_specs=[pl.BlockSpec((1, rbs * tq, w), lambda g, b, i: (b, i, g))]
        + [slice_spec(wt) for wt in weights],
        out_shape=[jax.ShapeDtypeStruct((B, S, groups * w), BF16)]
        + [jax.ShapeDtypeStruct(wt.shape, BF16) for wt in weights],
        scratch_shapes=[pltpu.VMEM((2 * pp, 3 * tq, tq), F32)],
        compiler_params=pltpu.CompilerParams(
            dimension_semantics=("arbitrary", "arbitrary", "arbitrary"), vmem_limit_bytes=VMEM_LIMIT),
        name="chunk_attn",
    )(proj, proj, vt, bias_rows, *weights)
    return out[0], out[1:]


def _mix_mem_kernel(yf_ref, yc_ref, x_ref, gf_ref, gc_ref, wo_ref, gpost_ref, gpre_ref, wq_ref,
                    mem_ref, gkv_ref, wk_ref, wv_ref, wmo_ref, gmpost_ref, o_ref, mk_ref, mv_ref, *, sub):
    @pl.when(pl.program_id(1) == 0)
    def _():
        mb = _rms(mem_ref[0], gkv_ref[...]).astype(BF16)
        mk_ref[...] = _dot(mb, wk_ref[...]).astype(BF16)
        mv_ref[...] = _dot(mb, wv_ref[...]).astype(BF16)

    d_fox = yf_ref.shape[2]
    halves = range(x_ref.shape[1] // sub)
    rows = [slice(h * sub, (h + 1) * sub) for h in halves]

    def mix(r):
        yf = _rms(yf_ref[0, r, :].astype(F32), gf_ref[...]).astype(BF16)
        yc = _rms(yc_ref[0, r, :].astype(F32), gc_ref[...]).astype(BF16)
        return _dot(yf, wo_ref[0:d_fox, :]) + _dot(yc, wo_ref[d_fox:, :])

    def query(x1):
        return _dot(_rms(x1, gpre_ref[...]).astype(BF16), wq_ref[...]).astype(BF16)

    def attend(q):
        dh = q.shape[1] // MEM_HEADS
        outs = []
        for hh in range(MEM_HEADS):
            sl = slice(hh * dh, (hh + 1) * dh)
            s = _dot_nt(q[:, sl], mk_ref[:, sl])
            pr = jnp.exp2(s - jnp.max(s, axis=-1, keepdims=True))
            inv = 1.0 / jnp.sum(pr, axis=-1, keepdims=True)
            outs.append((_dot(pr.astype(BF16), mv_ref[:, sl]) * inv).astype(BF16))
        return jnp.concatenate(outs, axis=-1)

    ys = [mix(rows[h]) for h in halves]
    x1s = [x_ref[0, rows[h], :] + _rms(ys[h], gpost_ref[...]) for h in halves]
    qs = [query(x1s[h]) for h in halves]
    os = [attend(qs[h]) for h in halves]
    y2s = [_dot(os[h], wmo_ref[...]) for h in halves]
    for h in halves:
        o_ref[0, rows[h], :] = x1s[h] + _rms(y2s[h], gmpost_ref[...])


def _mix_mem(yf, yc, x, gf, gc, wo, gpost, gpre, wq, mem, gkv, wk, wv, wmo, gmpost, tm, sub):
    B, S, D = x.shape
    M = mem.shape[1]
    dg = yf.shape[2]
    vec = lambda n: pl.BlockSpec((1, n), lambda b, i: (0, 0))
    mat = lambda r, c: pl.BlockSpec((r, c), lambda b, i: (0, 0), pipeline_mode=pl.Buffered(1))
    return pl.pallas_call(
        functools.partial(_mix_mem_kernel, sub=sub),
        grid=(B, S // tm),
        in_specs=[
            pl.BlockSpec((1, tm, dg), lambda b, i: (b, i, 0)),
            pl.BlockSpec((1, tm, dg), lambda b, i: (b, i, 0)),
            pl.BlockSpec((1, tm, D), lambda b, i: (b, i, 0)),
            vec(dg), vec(dg), mat(2 * dg, D), vec(D),
            vec(D), mat(D, D),
            pl.BlockSpec((1, M, D), lambda b, i: (b, 0, 0)),
            vec(D), mat(D, D), mat(D, D),
            mat(D, D), vec(D),
        ],
        out_specs=pl.BlockSpec((1, tm, D), lambda b, i: (b, i, 0)),
        out_shape=jax.ShapeDtypeStruct((B, S, D), F32),
        scratch_shapes=[pltpu.VMEM((M, D), BF16), pltpu.VMEM((M, D), BF16)],
        compiler_params=pltpu.CompilerParams(
            dimension_semantics=("arbitrary", "arbitrary"), vmem_limit_bytes=VMEM_LIMIT),
        name="mix_mem",
    )(yf, yc, x, gf, gc, wo, gpost, gpre, wq, mem, gkv, wk, wv, wmo, gmpost)


def _mlp_kernel(x_ref, gpre_ref, w1_ref, w2_ref, gpost_ref, o_ref, *, ff_block, sub):
    halves = range(x_ref.shape[0] // sub)
    rows = [slice(h * sub, (h + 1) * sub) for h in halves]
    hs = [_rms(x_ref[rows[h], :], gpre_ref[...]).astype(BF16) for h in halves]
    ys = [None for _ in halves]
    for n in range(0, w1_ref.shape[1], ff_block):
        for h in halves:
            a = jnp.maximum(_dot(hs[h], w1_ref[:, n:n + ff_block]), 0.0)
            part = _dot((a * a).astype(BF16), w2_ref[n:n + ff_block, :])
            ys[h] = part if ys[h] is None else ys[h] + part
    for h in halves:
        o_ref[rows[h], :] = x_ref[rows[h], :] + _rms(ys[h], gpost_ref[...])


def _mlp(x, gpre, w1, w2, gpost, tm, ff_block, sub):
    R, D = x.shape
    d_ff = w1.shape[1]
    return pl.pallas_call(
        functools.partial(_mlp_kernel, ff_block=ff_block, sub=sub),
        grid=(R // tm,),
        in_specs=[
            pl.BlockSpec((tm, D), lambda i: (i, 0)),
            pl.BlockSpec((1, D), lambda i: (0, 0)),
            pl.BlockSpec((D, d_ff), lambda i: (0, 0), pipeline_mode=pl.Buffered(1)),
            pl.BlockSpec((d_ff, D), lambda i: (0, 0), pipeline_mode=pl.Buffered(1)),
            pl.BlockSpec((1, D), lambda i: (0, 0)),
        ],
        out_specs=pl.BlockSpec((tm, D), lambda i: (i, 0)),
        out_shape=jax.ShapeDtypeStruct((R, D), F32),
        compiler_params=pltpu.CompilerParams(
            dimension_semantics=("arbitrary",), vmem_limit_bytes=VMEM_LIMIT),
        name="mlp",
    )(x, gpre, w1, w2, gpost)


def kernel(x, mem, w_in, b_fgt, rel_bias, g_fox_out, g_chk_out, w_out, g_mix_pre, g_mix_post,
           g_mem_kv, w_mq, w_mk, w_mv, w_mo, g_mem_pre, g_mem_post,
           w_ff1, w_ff2, g_ff_pre, g_ff_post):
    B, S, D = x.shape
    depth = w_in.shape[0]
    heads = b_fgt.shape[1]
    d_fox = heads * HEAD_DIM
    d_chk = rel_bias.shape[1] * HEAD_DIM
    assert d_fox == d_chk and heads % 2 == 0 and heads <= LANES
    pairs = heads // 2
    row = lambda v: v.reshape(1, -1)

    for l in range(depth):
        wt_all = _wprep(jnp.swapaxes(w_in, 1, 2), l, d_fox, d_chk, heads,
                        scale=HEAD_DIM ** -0.5 * LOG2E, tc=256)
        b_f = jnp.pad(jnp.tile(b_fgt[l], PIECES), (0, LANES - PIECES * heads)).reshape(1, LANES)

        proj, vt, qa, ka = _inproj(x, row(g_mix_pre[l]), wt_all, b_f, heads,
                                   n_val=d_fox + d_chk, tm=1024, sub=256)
        fox_pp, chk_pp = 4, 4
        y_fox = _fox(proj, vt, qa, ka, heads, tq=512, rb=256, pp=fox_pp)
        mem_scale = (D // MEM_HEADS) ** -0.5 * LOG2E
        later = [w_out[l], w_mq[l], w_mk[l], w_mv[l], w_mo[l], w_ff1[l], w_ff2[l]]
        y_chk, (wo_b, wq_b, wk_b, wv_b, wmo_b, w1_b, w2_b) = _chunk(
            proj, vt, _chunk_bias_rows(rel_bias[l], 256, chk_pp), later,
            [1.0, mem_scale, 1.0, 1.0, 1.0, 1.0, 1.0], tq=256, rbs=2, pp=chk_pp,
            col0=2 * pairs // chk_pp, row0=pairs // chk_pp)

        x = _mix_mem(y_fox, y_chk, x, row(g_fox_out[l]), row(g_chk_out[l]), wo_b,
                     row(g_mix_post[l]), row(g_mem_pre[l]), wq_b,
                     mem, row(g_mem_kv[l]), wk_b, wv_b, wmo_b, row(g_mem_post[l]), tm=1024, sub=256)
        x = _mlp(x.reshape(B * S, D), row(g_ff_pre[l]), w1_b, w2_b,
                 row(g_ff_post[l]), tm=1024, ff_block=1024, sub=256).reshape(B, S, D)
    return x
```

```python
import functools
import math

import numpy as np
import jax
import jax.numpy as jnp
from jax import lax
from jax.experimental import pallas as pl
from jax.experimental.pallas import tpu as pltpu

F32 = jnp.float32
BF16 = jnp.bfloat16

EPS = 1e-6
HEAD_DIM = 64
PAIR = 2 * HEAD_DIM
CHUNK = 64
LEFT_CHUNKS = 8
MAX_REL = 128
MEM_HEADS = 4
NEG = -1e30
LOG2E = math.log2(math.e)
LANES = 128
VMEM_LIMIT = 56 * 1024 * 1024
PIECES = 3
SUM_LANE = (HEAD_DIM, 0)


def _rms(x, g):
    return x * lax.rsqrt(jnp.mean(x * x, axis=-1, keepdims=True) + EPS) * g


def _dot(a, b):
    return jnp.dot(a, b, preferred_element_type=F32)


def _dot_nt(a, b):
    return lax.dot_general(a, b, (((1,), (1,)), ((), ())), preferred_element_type=F32)


def _split3(x):
    hi = x.astype(BF16)
    r1 = x - hi.astype(F32)
    mid = r1.astype(BF16)
    lo = (r1 - mid.astype(F32)).astype(BF16)
    return hi, mid, lo


def _wprep_kernel(wt_ref, o_ref, *, d_fox, d_chk, heads, scale):
    o_f = 3 * d_fox
    o_c = o_f + heads
    row = lambda lo, n: wt_ref[0, lo:lo + n, :]
    gates = row(o_f, heads)
    pad = jnp.zeros((LANES - PIECES * heads, gates.shape[1]), gates.dtype)
    o_ref[...] = jnp.concatenate(
        [row(0, d_fox) * scale, row(d_fox, d_fox), row(o_c, d_chk) * scale, row(o_c + d_chk, d_chk)]
        + [gates] * PIECES + [pad, row(2 * d_fox, d_fox), row(o_c + 2 * d_chk, d_chk)],
        axis=0).astype(o_ref.dtype)


def _wprep(wt, layer, d_fox, d_chk, heads, scale, tc):
    _, n_in, D = wt.shape
    n_out = 3 * (d_fox + d_chk) + LANES
    assert heads % 8 == 0, "row pieces must stay sublane-tile aligned"
    return pl.pallas_call(
        functools.partial(_wprep_kernel, d_fox=d_fox, d_chk=d_chk, heads=heads, scale=scale),
        grid=(D // tc,),
        in_specs=[pl.BlockSpec((1, n_in, tc), lambda i: (layer, 0, i))],
        out_specs=pl.BlockSpec((n_out, tc), lambda i: (0, i)),
        out_shape=jax.ShapeDtypeStruct((n_out, D), BF16),
        compiler_params=pltpu.CompilerParams(dimension_semantics=("arbitrary",), vmem_limit_bytes=VMEM_LIMIT),
        name="w_prep",
    )(wt)


def _inproj_kernel(x_ref, g_ref, wt_ref, bf_ref, tri_ref, e_ref, ones_ref,
                   proj_ref, vt_ref, qa_ref, ka_ref, carry_ref, *, heads, sub):
    @pl.when(pl.program_id(1) == 0)
    def _():
        carry_ref[...] = jnp.zeros_like(carry_ref)

    n_main = proj_ref.shape[2]
    halves = range(x_ref.shape[1] // sub)
    rows = [slice(h * sub, (h + 1) * sub) for h in halves]
    lane = lax.broadcasted_iota(jnp.int32, (1, LANES), 1)
    hbs = [_rms(x_ref[0, rows[h], :], g_ref[...]).astype(BF16) for h in halves]
    fs = []
    for h in halves:
        res = _dot_nt(hbs[h], wt_ref[0:n_main + LANES, :])
        proj_ref[0, rows[h], :] = res[:, :n_main].astype(BF16)
        vt_ref[0, :, rows[h]] = _dot_nt(wt_ref[n_main + LANES:, :], hbs[h]).astype(BF16)
        fs.append(res[:, n_main:] + bf_ref[...])
    carry = carry_ref[...]
    for h in halves:
        f = fs[h]
        logf = jnp.minimum(f, 0.0) - jnp.log1p(jnp.exp(-jnp.abs(f)))
        hi, mid, lo = _split3(logf)
        packed = jnp.where(lane < heads, hi, jnp.where(lane < 2 * heads, mid, lo))
        cum = _dot(tri_ref[...], packed)
        c = (cum + pltpu.roll(cum, LANES - heads, 1) + pltpu.roll(cum, LANES - 2 * heads, 1)) + carry
        carry = c[sub - 1:sub, :]
        pieces = jnp.concatenate(_split3(c * LOG2E), axis=-1)
        sel = _dot(pieces, e_ref[...])
        qa_ref[0, rows[h], :] = (sel[:, :LANES] + ones_ref[0:1, :]).astype(BF16)
        ka_ref[0, rows[h], :] = (ones_ref[1:2, :] - sel[:, LANES:]).astype(BF16)
    carry_ref[...] = carry


def _aug_constants(heads):
    k0 = PIECES * heads
    assert 2 * k0 <= LANES
    e = np.zeros((PIECES * LANES, 2 * LANES), np.float32)
    for h in range(heads):
        for r in range(PIECES):
            e[r * LANES + h, PIECES * h + r] = 1.0
            e[r * LANES + h, LANES + k0 + PIECES * h + r] = 1.0
    ones = np.zeros((2, LANES), np.float32)
    ones[0, k0:2 * k0] = 1.0
    ones[1, 0:k0] = 1.0
    return jnp.asarray(e, BF16), jnp.asarray(ones)


def _inproj(x, g, wt_all, b_f, heads, n_val, tm, sub):
    B, S, D = x.shape
    n_main = wt_all.shape[0] - n_val - LANES
    tri = jnp.asarray(np.tril(np.ones((sub, sub), np.float32)), BF16)
    e, ones = _aug_constants(heads)
    const = lambda r, c: pl.BlockSpec((r, c), lambda b, i: (0, 0), pipeline_mode=pl.Buffered(1))
    return pl.pallas_call(
        functools.partial(_inproj_kernel, heads=heads, sub=sub),
        grid=(B, S // tm),
        in_specs=[
            pl.BlockSpec((1, tm, D), lambda b, i: (b, i, 0)),
            const(1, D), const(n_main + LANES + n_val, D), const(1, LANES), const(sub, sub),
            const(PIECES * LANES, 2 * LANES), const(2, LANES),
        ],
        out_specs=[
            pl.BlockSpec((1, tm, n_main), lambda b, i: (b, i, 0)),
            pl.BlockSpec((1, n_val, tm), lambda b, i: (b, 0, i)),
            pl.BlockSpec((1, tm, LANES), lambda b, i: (b, i, 0)),
            pl.BlockSpec((1, tm, LANES), lambda b, i: (b, i, 0)),
        ],
        out_shape=[
            jax.ShapeDtypeStruct((B, S, n_main), BF16),
            jax.ShapeDtypeStruct((B, n_val, S), BF16),
            jax.ShapeDtypeStruct((B, S, LANES), BF16),
            jax.ShapeDtypeStruct((B, S, LANES), BF16),
        ],
        scratch_shapes=[pltpu.VMEM((1, LANES), F32)],
        compiler_params=pltpu.CompilerParams(
            dimension_semantics=("arbitrary", "arbitrary"), vmem_limit_bytes=VMEM_LIMIT),
        name="inproj",
    )(x, g, wt_all, b_f, tri, e, ones)


def _head_mask(lane, hh):
    return lane < HEAD_DIM if hh == 0 else lane >= HEAD_DIM


def _value_aug_t(vt, feat, hh):
    ones = jnp.where(feat == SUM_LANE[hh], 1.0, 0.0).astype(vt.dtype)
    return jnp.where(_head_mask(feat, hh), vt, ones)


def _softmax_t(ss):
    m = functools.reduce(jnp.maximum, [jnp.max(s, axis=0, keepdims=True) for s in ss])
    return [jnp.exp2(s - m).astype(BF16) for s in ss]


def _pair_output_t(acc0, acc1, feat):
    l0 = acc0[SUM_LANE[0]:SUM_LANE[0] + 1, :]
    l1 = acc1[SUM_LANE[1]:SUM_LANE[1] + 1, :]
    return jnp.where(_head_mask(feat, 0), acc0 * (1.0 / l0), acc1 * (1.0 / l1)).T


def _fox_kernel(q_ref, qa_ref, k_ref, ka_ref, vt_ref, o_ref, s_ref, p_ref, *, tq, rb, pp, heads):
    grp = pl.program_id(1)
    lane = lax.broadcasted_iota(jnp.int32, (1, LANES), 1)
    feat = lax.broadcasted_iota(jnp.int32, (PAIR, 1), 0)
    k0 = PIECES * heads
    causal = lax.broadcasted_iota(jnp.int32, (rb, rb), 0) <= lax.broadcasted_iota(jnp.int32, (rb, rb), 1)
    slots = s_ref.shape[0]

    def attend(n):
        units = [(pi, r, hh) for pi in range(pp) for r in range(tq // rb) for hh in range(2)]

        def keys(r):
            past = n * tq + r * rb
            return past, past + rb

        def logits(u):
            pi, r, hh = units[u]
            past, total = keys(r)
            q = q_ref[0, r * rb:(r + 1) * rb, pi * PAIR:(pi + 1) * PAIR]
            qa = qa_ref[0, r * rb:(r + 1) * rb, :]
            a0 = PIECES * (2 * (grp * pp + pi) + hh)
            amask = ((lane >= a0) & (lane < a0 + PIECES)) | ((lane >= k0 + a0) & (lane < k0 + a0 + PIECES))
            q_aug = jnp.concatenate([jnp.where(_head_mask(lane, hh), q, jnp.zeros_like(q)),
                                     jnp.where(amask, qa, jnp.zeros_like(qa))], axis=-1)
            m = None
            for lo, hi in ([(0, past)] if past else []) + [(past, total)]:
                kaug = jnp.concatenate([k_ref[0, lo:hi, pi * PAIR:(pi + 1) * PAIR], ka_ref[0, lo:hi, :]], axis=-1)
                s = _dot_nt(kaug, q_aug)
                if lo == past:
                    s = jnp.where(causal, s, NEG)
                s_ref[u % slots, lo:hi, :] = s
                col_max = jnp.max(s, axis=0, keepdims=True)
                m = col_max if m is None else jnp.maximum(m, col_max)
            return m

        def weights(u, m):
            _, total = keys(units[u][1])
            p_ref[u % slots, 0:total, :] = jnp.exp2(s_ref[u % slots, 0:total, :] - m).astype(BF16)

        def values(u):
            pi, r, hh = units[u]
            _, total = keys(r)
            return _dot(_value_aug_t(vt_ref[0, pi * PAIR:(pi + 1) * PAIR, 0:total], feat, hh),
                        p_ref[u % slots, 0:total, :])

        ms, accs = {}, {}
        n_units = len(units)
        for t in range(0, n_units + 4, 2):
            for u in (t, t + 1):
                if u < n_units:
                    ms[u] = logits(u)
            for u in (t - 2, t - 1):
                if 0 <= u < n_units:
                    weights(u, ms.pop(u))
            for u in (t - 4, t - 3):
                if 0 <= u < n_units:
                    accs[units[u]] = values(u)
            if 0 <= t - 4 < n_units:
                pi, r, _ = units[t - 4]
                o_ref[0, r * rb:(r + 1) * rb, pi * PAIR:(pi + 1) * PAIR] = _pair_output_t(
                    accs.pop((pi, r, 0)), accs.pop((pi, r, 1)), feat).astype(o_ref.dtype)

    for n in range(k_ref.shape[1] // tq):
        pl.when(pl.program_id(2) == n)(functools.partial(attend, n))


def _fox(proj, vt, qa, ka, heads, tq, rb, pp):
    B, S, _ = proj.shape
    groups = heads // 2 // pp
    w = pp * PAIR
    return pl.pallas_call(
        functools.partial(_fox_kernel, tq=tq, rb=rb, pp=pp, heads=heads),
        grid=(B, groups, S // tq),
        in_specs=[
            pl.BlockSpec((1, tq, w), lambda b, g, i: (b, i, g)),
            pl.BlockSpec((1, tq, LANES), lambda b, g, i: (b, i, 0)),
            pl.BlockSpec((1, S, w), lambda b, g, i: (b, 0, groups + g)),
            pl.BlockSpec((1, S, LANES), lambda b, g, i: (b, 0, 0)),
            pl.BlockSpec((1, w, S), lambda b, g, i: (b, g, 0)),
        ],
        out_specs=pl.BlockSpec((1, tq, w), lambda b, g, i: (b, i, g)),
        out_shape=jax.ShapeDtypeStruct((B, S, heads * HEAD_DIM), BF16),
        scratch_shapes=[pltpu.VMEM((6, S, rb), F32), pltpu.VMEM((6, S, rb), BF16)],
        compiler_params=pltpu.CompilerParams(
            dimension_semantics=("arbitrary", "arbitrary", "arbitrary"), vmem_limit_bytes=VMEM_LIMIT),
        name="fox_attn",
    )(proj, qa, proj, ka, vt)


def _chunk_kernel(q_ref, k_ref, vt_ref, g_ref, *refs, tq, rbs, pp, scales):
    n_w = len(scales)
    w_refs, o_ref, wb_refs, bias_ref = refs[:n_w], refs[n_w], refs[n_w + 1:2 * n_w + 1], refs[2 * n_w + 1]
    for w_ref, wb_ref, scale in zip(w_refs, wb_refs, scales):
        w = w_ref[...]
        wb_ref[...] = (w if scale == 1.0 else w * scale).astype(wb_ref.dtype)

    i = pl.program_id(2)
    lane = lax.broadcasted_iota(jnp.int32, (1, LANES), 1)
    feat = lax.broadcasted_iota(jnp.int32, (PAIR, 1), 0)

    @pl.when((pl.program_id(1) == 0) & (i == 0))
    def _():
        k_chunk = lax.broadcasted_iota(jnp.int32, (tq, tq), 0) // CHUNK
        q_chunk = lax.broadcasted_iota(jnp.int32, (tq, tq), 1) // CHUNK
        for which in range(3):
            back = which * (tq // CHUNK) + q_chunk - k_chunk
            vis = (back >= 0) & (back <= LEFT_CHUNKS)
            for hd in range(2 * pp):
                g = jnp.broadcast_to(g_ref[0, which, hd], (tq, 2 * tq))
                toep = pltpu.roll(g, 0, 1, stride=1, stride_axis=0)[:, :tq]
                bias_ref[hd, (2 - which) * tq:(3 - which) * tq, :] = jnp.where(vis, toep, NEG)

    def attend(first_step):
        units = [(pi, blk, hh) for pi in range(pp) for blk in range(rbs) for hh in range(2)]

        def key_span(blk):
            if first_step:
                n_blocks = min(3, blk + 1)
                return (blk - (n_blocks - 1)) * tq, n_blocks * tq
            return pl.multiple_of((i * rbs + blk - 2) * tq, tq), 3 * tq

        def logits(pi, blk, hh):
            cols = slice(pi * PAIR, (pi + 1) * PAIR)
            ks, span = key_span(blk)
            q = q_ref[0, blk * tq:(blk + 1) * tq, cols]
            qh = jnp.where(_head_mask(lane, hh), q, jnp.zeros_like(q))
            return _dot_nt(k_ref[0, pl.ds(ks, span), cols], qh) + bias_ref[2 * pi + hh, 3 * tq - span:, :]

        def weighted_values(pi, blk, hh, p):
            ks, span = key_span(blk)
            return _dot(_value_aug_t(vt_ref[0, pi * PAIR:(pi + 1) * PAIR, pl.ds(ks, span)], feat, hh), p)

        ss, ps, accs = {}, {}, {}
        n_units = len(units)
        for t in range(0, n_units + 4, 2):
            for u in (t, t + 1):
                if u < n_units:
                    ss[u] = logits(*units[u])
            for u in (t - 2, t - 1):
                if 0 <= u < n_units:
                    ps[u] = _softmax_t([ss.pop(u)])[0]
            for u in (t - 4, t - 3):
                if 0 <= u < n_units:
                    accs[units[u]] = weighted_values(*units[u], ps.pop(u))
            if 0 <= t - 4 < n_units:
                pi, blk, _ = units[t - 4]
                o_ref[0, blk * tq:(blk + 1) * tq, pi * PAIR:(pi + 1) * PAIR] = _pair_output_t(
                    accs.pop((pi, blk, 0)), accs.pop((pi, blk, 1)), feat).astype(o_ref.dtype)

    pl.when(i == 0)(functools.partial(attend, True))
    pl.when(i > 0)(functools.partial(attend, False))


def _chunk_bias_rows(rel_table, tq, pp):
    heads = rel_table.shape[0]
    m = np.arange(2 * tq)
    offset = np.where(m < tq, m, m - 2 * tq)
    idx = np.stack([np.clip(which * tq + offset, -MAX_REL, MAX_REL) + MAX_REL for which in range(3)])
    rows = rel_table.astype(F32)[:, idx] * LOG2E
    return rows.reshape(heads // (2 * pp), 2 * pp, 3, 1, 2 * tq).transpose(0, 2, 1, 3, 4)


def _chunk(proj, vt, bias_rows, weights, scales, tq, rbs, pp, col0, row0):
    B, S, _ = proj.shape
    groups = bias_rows.shape[0]
    w = pp * PAIR
    steps = S // (rbs * tq)
    assert tq % CHUNK == 0 and LEFT_CHUNKS * CHUNK <= 2 * tq, "band must fit in three key blocks"
    assert rbs >= 2, "after the first step every query block must have two key blocks behind it"
    assert groups == 1, "each weight slice is cast at exactly one grid step"
    slice_spec = lambda wt: pl.BlockSpec((wt.shape[0] // (B * steps), wt.shape[1]),
                                         lambda g, b, i: (b * steps + i, 0))
    out = pl.pallas_call(
        functools.partial(_chunk_kernel, tq=tq, rbs=rbs, pp=pp, scales=tuple(scales)),
        grid=(groups, B, steps),
        in_specs=[
            pl.BlockSpec((1, rbs * tq, w), lambda g, b, i: (b, i, col0 + g)),
            pl.BlockSpec((1, S, w), lambda g, b, i: (b, 0, col0 + groups + g)),
            pl.BlockSpec((1, w, S), lambda g, b, i: (b, row0 + g, 0)),
            pl.BlockSpec((1, 3, 2 * pp, 1, 2 * tq), lambda g, b, i: (g, 0, 0, 0, 0)),
        ] + [slice_spec(wt) for wt in weights],
        out_specs=[pl.BlockSpec((1, rbs * tq, w), lambda g, b, i: (b, i, g))]
        + [slice_spec(wt) for wt in weights],
        out_shape=[jax.ShapeDtypeStruct((B, S, groups * w), BF16)]
        + [jax.ShapeDtypeStruct(wt.shape, BF16) for wt in weights],
        scratch_shapes=[pltpu.VMEM((2 * pp, 3 * tq, tq), F32)],
        compiler_params=pltpu.CompilerParams(
            dimension_semantics=("arbitrary", "arbitrary", "arbitrary"), vmem_limit_bytes=VMEM_LIMIT),
        name="chunk_attn",
    )(proj, proj, vt, bias_rows, *weights)
    return out[0], out[1:]


def _mix_mem_kernel(yf_ref, yc_ref, x_ref, gf_ref, gc_ref, wo_ref, gpost_ref, gpre_ref, wq_ref,
                    mem_ref, gkv_ref, wk_ref, wv_ref, wmo_ref, gmpost_ref, o_ref, mk_ref, mv_ref, *, sub):
    @pl.when(pl.program_id(1) == 0)
    def _():
        mb = _rms(mem_ref[0], gkv_ref[...]).astype(BF16)
        mk_ref[...] = _dot(mb, wk_ref[...]).astype(BF16)
        mv_ref[...] = _dot(mb, wv_ref[...]).astype(BF16)

    d_fox = yf_ref.shape[2]
    halves = range(x_ref.shape[1] // sub)
    rows = [slice(h * sub, (h + 1) * sub) for h in halves]

    def mix(r):
        yf = _rms(yf_ref[0, r, :].astype(F32), gf_ref[...]).astype(BF16)
        yc = _rms(yc_ref[0, r, :].astype(F32), gc_ref[...]).astype(BF16)
        return _dot(yf, wo_ref[0:d_fox, :]) + _dot(yc, wo_ref[d_fox:, :])

    def query(x1):
        return _dot(_rms(x1, gpre_ref[...]).astype(BF16), wq_ref[...]).astype(BF16)

    def attend(q):
        dh = q.shape[1] // MEM_HEADS
        outs = []
        for hh in range(MEM_HEADS):
            sl = slice(hh * dh, (hh + 1) * dh)
            s = _dot_nt(q[:, sl], mk_ref[:, sl])
            pr = jnp.exp2(s - jnp.max(s, axis=-1, keepdims=True))
            inv = 1.0 / jnp.sum(pr, axis=-1, keepdims=True)
            outs.append((_dot(pr.astype(BF16), mv_ref[:, sl]) * inv).astype(BF16))
        return jnp.concatenate(outs, axis=-1)

    ys = [mix(rows[h]) for h in halves]
    x1s = [x_ref[0, rows[h], :] + _rms(ys[h], gpost_ref[...]) for h in halves]
    qs = [query(x1s[h]) for h in halves]
    os = [attend(qs[h]) for h in halves]
    y2s = [_dot(os[h], wmo_ref[...]) for h in halves]
    for h in halves:
        o_ref[0, rows[h], :] = x1s[h] + _rms(y2s[h], gmpost_ref[...])


def _mix_mem(yf, yc, x, gf, gc, wo, gpost, gpre, wq, mem, gkv, wk, wv, wmo, gmpost, tm, sub):
    B, S, D = x.shape
    M = mem.shape[1]
    dg = yf.shape[2]
    vec = lambda n: pl.BlockSpec((1, n), lambda b, i: (0, 0))
    mat = lambda r, c: pl.BlockSpec((r, c), lambda b, i: (0, 0), pipeline_mode=pl.Buffered(1))
    return pl.pallas_call(
        functools.partial(_mix_mem_kernel, sub=sub),
        grid=(B, S // tm),
        in_specs=[
            pl.BlockSpec((1, tm, dg), lambda b, i: (b, i, 0)),
            pl.BlockSpec((1, tm, dg), lambda b, i: (b, i, 0)),
            pl.BlockSpec((1, tm, D), lambda b, i: (b, i, 0)),
            vec(dg), vec(dg), mat(2 * dg, D), vec(D),
            vec(D), mat(D, D),
            pl.BlockSpec((1, M, D), lambda b, i: (b, 0, 0)),
            vec(D), mat(D, D), mat(D, D),
            mat(D, D), vec(D),
        ],
        out_specs=pl.BlockSpec((1, tm, D), lambda b, i: (b, i, 0)),
        out_shape=jax.ShapeDtypeStruct((B, S, D), F32),
        scratch_shapes=[pltpu.VMEM((M, D), BF16), pltpu.VMEM((M, D), BF16)],
        compiler_params=pltpu.CompilerParams(
            dimension_semantics=("arbitrary", "arbitrary"), vmem_limit_bytes=VMEM_LIMIT),
        name="mix_mem",
    )(yf, yc, x, gf, gc, wo, gpost, gpre, wq, mem, gkv, wk, wv, wmo, gmpost)


def _mlp_kernel(x_ref, gpre_ref, w1_ref, w2_ref, gpost_ref, o_ref, *, ff_block, sub):
    halves = range(x_ref.shape[0] // sub)
    rows = [slice(h * sub, (h + 1) * sub) for h in halves]
    hs = [_rms(x_ref[rows[h], :], gpre_ref[...]).astype(BF16) for h in halves]
    ys = [None for _ in halves]
    for n in range(0, w1_ref.shape[1], ff_block):
        for h in halves:
            a = jnp.maximum(_dot(hs[h], w1_ref[:, n:n + ff_block]), 0.0)
            part = _dot((a * a).astype(BF16), w2_ref[n:n + ff_block, :])
            ys[h] = part if ys[h] is None else ys[h] + part
    for h in halves:
        o_ref[rows[h], :] = x_ref[rows[h], :] + _rms(ys[h], gpost_ref[...])


def _mlp(x, gpre, w1, w2, gpost, tm, ff_block, sub):
    R, D = x.shape
    d_ff = w1.shape[1]
    return pl.pallas_call(
        functools.partial(_mlp_kernel, ff_block=ff_block, sub=sub),
        grid=(R // tm,),
        in_specs=[
            pl.BlockSpec((tm, D), lambda i: (i, 0)),
            pl.BlockSpec((1, D), lambda i: (0, 0)),
            pl.BlockSpec((D, d_ff), lambda i: (0, 0), pipeline_mode=pl.Buffered(1)),
            pl.BlockSpec((d_ff, D), lambda i: (0, 0), pipeline_mode=pl.Buffered(1)),
            pl.BlockSpec((1, D), lambda i: (0, 0)),
        ],
        out_specs=pl.BlockSpec((tm, D), lambda i: (i, 0)),
        out_shape=jax.ShapeDtypeStruct((R, D), F32),
        compiler_params=pltpu.CompilerParams(
            dimension_semantics=("arbitrary",), vmem_limit_bytes=VMEM_LIMIT),
        name="mlp",
    )(x, gpre, w1, w2, gpost)


def kernel(x, mem, w_in, b_fgt, rel_bias, g_fox_out, g_chk_out, w_out, g_mix_pre, g_mix_post,
           g_mem_kv, w_mq, w_mk, w_mv, w_mo, g_mem_pre, g_mem_post,
           w_ff1, w_ff2, g_ff_pre, g_ff_post):
    B, S, D = x.shape
    depth = w_in.shape[0]
    heads = b_fgt.shape[1]
    d_fox = heads * HEAD_DIM
    d_chk = rel_bias.shape[1] * HEAD_DIM
    assert d_fox == d_chk and heads % 2 == 0 and heads <= LANES
    pairs = heads // 2
    row = lambda v: v.reshape(1, -1)

    for l in range(depth):
        wt_all = _wprep(jnp.swapaxes(w_in, 1, 2), l, d_fox, d_chk, heads,
                        scale=HEAD_DIM ** -0.5 * LOG2E, tc=256)
        b_f = jnp.pad(jnp.tile(b_fgt[l], PIECES), (0, LANES - PIECES * heads)).reshape(1, LANES)

        proj, vt, qa, ka = _inproj(x, row(g_mix_pre[l]), wt_all, b_f, heads,
                                   n_val=d_fox + d_chk, tm=1024, sub=256)
        fox_pp, chk_pp = 4, 4
        y_fox = _fox(proj, vt, qa, ka, heads, tq=512, rb=256, pp=fox_pp)
        mem_scale = (D // MEM_HEADS) ** -0.5 * LOG2E
        later = [w_out[l], w_mq[l], w_mk[l], w_mv[l], w_mo[l], w_ff1[l], w_ff2[l]]
        y_chk, (wo_b, wq_b, wk_b, wv_b, wmo_b, w1_b, w2_b) = _chunk(
            proj, vt, _chunk_bias_rows(rel_bias[l], 256, chk_pp), later,
            [1.0, mem_scale, 1.0, 1.0, 1.0, 1.0, 1.0], tq=256, rbs=4, pp=chk_pp,
            col0=2 * pairs // chk_pp, row0=pairs // chk_pp)

        x = _mix_mem(y_fox, y_chk, x, row(g_fox_out[l]), row(g_chk_out[l]), wo_b,
                     row(g_mix_post[l]), row(g_mem_pre[l]), wq_b,
                     mem, row(g_mem_kv[l]), wk_b, wv_b, wmo_b, row(g_mem_post[l]), tm=1024, sub=256)
        x = _mlp(x.reshape(B * S, D), row(g_ff_pre[l]), w1_b, w2_b,
                 row(g_ff_post[l]), tm=1024, ff_block=1024, sub=256).reshape(B, S, D)
    return x
```

```python
import functools
import math

import numpy as np
import jax
import jax.numpy as jnp
from jax import lax
from jax.experimental import pallas as pl
from jax.experimental.pallas import tpu as pltpu

F32 = jnp.float32
BF16 = jnp.bfloat16

EPS = 1e-6
HEAD_DIM = 64
PAIR = 2 * HEAD_DIM
CHUNK = 64
LEFT_CHUNKS = 8
MAX_REL = 128
MEM_HEADS = 4
NEG = -1e30
LOG2E = math.log2(math.e)
LANES = 128
VMEM_LIMIT = 56 * 1024 * 1024
PIECES = 3
SUM_LANE = (HEAD_DIM, 0)


def _rms(x, g):
    return x * lax.rsqrt(jnp.mean(x * x, axis=-1, keepdims=True) + EPS) * g


def _dot(a, b):
    return jnp.dot(a, b, preferred_element_type=F32)


def _dot_nt(a, b):
    return lax.dot_general(a, b, (((1,), (1,)), ((), ())), preferred_element_type=F32)


def _split3(x):
    hi = x.astype(BF16)
    r1 = x - hi.astype(F32)
    mid = r1.astype(BF16)
    lo = (r1 - mid.astype(F32)).astype(BF16)
    return hi, mid, lo


def _wprep_kernel(wt_ref, o_ref, *, d_fox, d_chk, heads, scale):
    o_f = 3 * d_fox
    o_c = o_f + heads
    row = lambda lo, n: wt_ref[0, lo:lo + n, :]
    gates = row(o_f, heads)
    pad = jnp.zeros((LANES - PIECES * heads, gates.shape[1]), gates.dtype)
    o_ref[...] = jnp.concatenate(
        [row(0, d_fox) * scale, row(d_fox, d_fox), row(o_c, d_chk) * scale, row(o_c + d_chk, d_chk)]
        + [gates] * PIECES + [pad, row(2 * d_fox, d_fox), row(o_c + 2 * d_chk, d_chk)],
        axis=0).astype(o_ref.dtype)


def _wprep(wt, layer, d_fox, d_chk, heads, scale, tc):
    _, n_in, D = wt.shape
    n_out = 3 * (d_fox + d_chk) + LANES
    assert heads % 8 == 0, "row pieces must stay sublane-tile aligned"
    return pl.pallas_call(
        functools.partial(_wprep_kernel, d_fox=d_fox, d_chk=d_chk, heads=heads, scale=scale),
        grid=(D // tc,),
        in_specs=[pl.BlockSpec((1, n_in, tc), lambda i: (layer, 0, i))],
        out_specs=pl.BlockSpec((n_out, tc), lambda i: (0, i)),
        out_shape=jax.ShapeDtypeStruct((n_out, D), BF16),
        compiler_params=pltpu.CompilerParams(dimension_semantics=("arbitrary",), vmem_limit_bytes=VMEM_LIMIT),
        name="w_prep",
    )(wt)


def _inproj_kernel(x_ref, g_ref, wt_ref, bf_ref, tri_ref, e_ref, ones_ref,
                   proj_ref, vt_ref, qa_ref, ka_ref, carry_ref, *, heads, sub):
    @pl.when(pl.program_id(1) == 0)
    def _():
        carry_ref[...] = jnp.zeros_like(carry_ref)

    n_main = proj_ref.shape[2]
    halves = range(x_ref.shape[1] // sub)
    rows = [slice(h * sub, (h + 1) * sub) for h in halves]
    lane = lax.broadcasted_iota(jnp.int32, (1, LANES), 1)
    hbs = [_rms(x_ref[0, rows[h], :], g_ref[...]).astype(BF16) for h in halves]
    fs = []
    for h in halves:
        res = _dot_nt(hbs[h], wt_ref[0:n_main + LANES, :])
        proj_ref[0, rows[h], :] = res[:, :n_main].astype(BF16)
        vt_ref[0, :, rows[h]] = _dot_nt(wt_ref[n_main + LANES:, :], hbs[h]).astype(BF16)
        fs.append(res[:, n_main:] + bf_ref[...])
    carry = carry_ref[...]
    for h in halves:
        f = fs[h]
        logf = jnp.minimum(f, 0.0) - jnp.log1p(jnp.exp(-jnp.abs(f)))
        hi, mid, lo = _split3(logf)
        packed = jnp.where(lane < heads, hi, jnp.where(lane < 2 * heads, mid, lo))
        cum = _dot(tri_ref[...], packed)
        c = (cum + pltpu.roll(cum, LANES - heads, 1) + pltpu.roll(cum, LANES - 2 * heads, 1)) + carry
        carry = c[sub - 1:sub, :]
        pieces = jnp.concatenate(_split3(c * LOG2E), axis=-1)
        sel = _dot(pieces, e_ref[...])
        qa_ref[0, rows[h], :] = (sel[:, :LANES] + ones_ref[0:1, :]).astype(BF16)
        ka_ref[0, rows[h], :] = (ones_ref[1:2, :] - sel[:, LANES:]).astype(BF16)
    carry_ref[...] = carry


def _aug_constants(heads):
    k0 = PIECES * heads
    assert 2 * k0 <= LANES
    e = np.zeros((PIECES * LANES, 2 * LANES), np.float32)
    for h in range(heads):
        for r in range(PIECES):
            e[r * LANES + h, PIECES * h + r] = 1.0
            e[r * LANES + h, LANES + k0 + PIECES * h + r] = 1.0
    ones = np.zeros((2, LANES), np.float32)
    ones[0, k0:2 * k0] = 1.0
    ones[1, 0:k0] = 1.0
    return jnp.asarray(e, BF16), jnp.asarray(ones)


def _inproj(x, g, wt_all, b_f, heads, n_val, tm, sub):
    B, S, D = x.shape
    n_main = wt_all.shape[0] - n_val - LANES
    tri = jnp.asarray(np.tril(np.ones((sub, sub), np.float32)), BF16)
    e, ones = _aug_constants(heads)
    const = lambda r, c: pl.BlockSpec((r, c), lambda b, i: (0, 0), pipeline_mode=pl.Buffered(1))
    return pl.pallas_call(
        functools.partial(_inproj_kernel, heads=heads, sub=sub),
        grid=(B, S // tm),
        in_specs=[
            pl.BlockSpec((1, tm, D), lambda b, i: (b, i, 0)),
            const(1, D), const(n_main + LANES + n_val, D), const(1, LANES), const(sub, sub),
            const(PIECES * LANES, 2 * LANES), const(2, LANES),
        ],
        out_specs=[
            pl.BlockSpec((1, tm, n_main), lambda b, i: (b, i, 0)),
            pl.BlockSpec((1, n_val, tm), lambda b, i: (b, 0, i)),
            pl.BlockSpec((1, tm, LANES), lambda b, i: (b, i, 0)),
            pl.BlockSpec((1, tm, LANES), lambda b, i: (b, i, 0)),
        ],
        out_shape=[
            jax.ShapeDtypeStruct((B, S, n_main), BF16),
            jax.ShapeDtypeStruct((B, n_val, S), BF16),
            jax.ShapeDtypeStruct((B, S, LANES), BF16),
            jax.ShapeDtypeStruct((B, S, LANES), BF16),
        ],
        scratch_shapes=[pltpu.VMEM((1, LANES), F32)],
        compiler_params=pltpu.CompilerParams(
            dimension_semantics=("arbitrary", "arbitrary"), vmem_limit_bytes=VMEM_LIMIT),
        name="inproj",
    )(x, g, wt_all, b_f, tri, e, ones)


def _head_mask(lane, hh):
    return lane < HEAD_DIM if hh == 0 else lane >= HEAD_DIM


def _value_aug_t(vt, feat, hh):
    ones = jnp.where(feat == SUM_LANE[hh], 1.0, 0.0).astype(vt.dtype)
    return jnp.where(_head_mask(feat, hh), vt, ones)


def _softmax_t(ss):
    m = functools.reduce(jnp.maximum, [jnp.max(s, axis=0, keepdims=True) for s in ss])
    return [jnp.exp2(s - m).astype(BF16) for s in ss]


def _pair_output_t(acc0, acc1, feat):
    l0 = acc0[SUM_LANE[0]:SUM_LANE[0] + 1, :]
    l1 = acc1[SUM_LANE[1]:SUM_LANE[1] + 1, :]
    return jnp.where(_head_mask(feat, 0), acc0 * (1.0 / l0), acc1 * (1.0 / l1)).T


def _fox_kernel(q_ref, qa_ref, k_ref, ka_ref, vt_ref, o_ref, s_ref, p_ref, *, tq, rb, pp, heads):
    grp = pl.program_id(1)
    lane = lax.broadcasted_iota(jnp.int32, (1, LANES), 1)
    feat = lax.broadcasted_iota(jnp.int32, (PAIR, 1), 0)
    k0 = PIECES * heads
    causal = lax.broadcasted_iota(jnp.int32, (rb, rb), 0) <= lax.broadcasted_iota(jnp.int32, (rb, rb), 1)
    slots = s_ref.shape[0]

    def attend(n):
        units = [(pi, r, hh) for pi in range(pp) for r in range(tq // rb) for hh in range(2)]

        def keys(r):
            past = n * tq + r * rb
            return past, past + rb

        def logits(u):
            pi, r, hh = units[u]
            past, total = keys(r)
            q = q_ref[0, r * rb:(r + 1) * rb, pi * PAIR:(pi + 1) * PAIR]
            qa = qa_ref[0, r * rb:(r + 1) * rb, :]
            a0 = PIECES * (2 * (grp * pp + pi) + hh)
            amask = ((lane >= a0) & (lane < a0 + PIECES)) | ((lane >= k0 + a0) & (lane < k0 + a0 + PIECES))
            q_aug = jnp.concatenate([jnp.where(_head_mask(lane, hh), q, jnp.zeros_like(q)),
                                     jnp.where(amask, qa, jnp.zeros_like(qa))], axis=-1)
            m = None
            for lo, hi in ([(0, past)] if past else []) + [(past, total)]:
                kaug = jnp.concatenate([k_ref[0, lo:hi, pi * PAIR:(pi + 1) * PAIR], ka_ref[0, lo:hi, :]], axis=-1)
                s = _dot_nt(kaug, q_aug)
                if lo == past:
                    s = jnp.where(causal, s, NEG)
                s_ref[u % slots, lo:hi, :] = s
                col_max = jnp.max(s, axis=0, keepdims=True)
                m = col_max if m is None else jnp.maximum(m, col_max)
            return m

        def weights(u, m):
            _, total = keys(units[u][1])
            p_ref[u % slots, 0:total, :] = jnp.exp2(s_ref[u % slots, 0:total, :] - m).astype(BF16)

        def values(u):
            pi, r, hh = units[u]
            _, total = keys(r)
            return _dot(_value_aug_t(vt_ref[0, pi * PAIR:(pi + 1) * PAIR, 0:total], feat, hh),
                        p_ref[u % slots, 0:total, :])

        ms, accs = {}, {}
        n_units = len(units)
        for t in range(0, n_units + 4, 2):
            for u in (t, t + 1):
                if u < n_units:
                    ms[u] = logits(u)
            for u in (t - 2, t - 1):
                if 0 <= u < n_units:
                    weights(u, ms.pop(u))
            for u in (t - 4, t - 3):
                if 0 <= u < n_units:
                    accs[units[u]] = values(u)
            if 0 <= t - 4 < n_units:
                pi, r, _ = units[t - 4]
                o_ref[0, r * rb:(r + 1) * rb, pi * PAIR:(pi + 1) * PAIR] = _pair_output_t(
                    accs.pop((pi, r, 0)), accs.pop((pi, r, 1)), feat).astype(o_ref.dtype)

    for n in range(k_ref.shape[1] // tq):
        pl.when(pl.program_id(2) == n)(functools.partial(attend, n))


def _fox(proj, vt, qa, ka, heads, tq, rb, pp):
    B, S, _ = proj.shape
    groups = heads // 2 // pp
    w = pp * PAIR
    return pl.pallas_call(
        functools.partial(_fox_kernel, tq=tq, rb=rb, pp=pp, heads=heads),
        grid=(B, groups, S // tq),
        in_specs=[
            pl.BlockSpec((1, tq, w), lambda b, g, i: (b, i, g)),
            pl.BlockSpec((1, tq, LANES), lambda b, g, i: (b, i, 0)),
            pl.BlockSpec((1, S, w), lambda b, g, i: (b, 0, groups + g)),
            pl.BlockSpec((1, S, LANES), lambda b, g, i: (b, 0, 0)),
            pl.BlockSpec((1, w, S), lambda b, g, i: (b, g, 0)),
        ],
        out_specs=pl.BlockSpec((1, tq, w), lambda b, g, i: (b, i, g)),
        out_shape=jax.ShapeDtypeStruct((B, S, heads * HEAD_DIM), BF16),
        scratch_shapes=[pltpu.VMEM((6, S, rb), F32), pltpu.VMEM((6, S, rb), BF16)],
        compiler_params=pltpu.CompilerParams(
            dimension_semantics=("arbitrary", "arbitrary", "arbitrary"), vmem_limit_bytes=VMEM_LIMIT),
        name="fox_attn",
    )(proj, qa, proj, ka, vt)


def _chunk_kernel(q_ref, k_ref, vt_ref, g_ref, *refs, tq, rbs, pp, scales):
    n_w = len(scales)
    w_refs, o_ref, wb_refs, bias_ref = refs[:n_w], refs[n_w], refs[n_w + 1:2 * n_w + 1], refs[2 * n_w + 1]
    for w_ref, wb_ref, scale in zip(w_refs, wb_refs, scales):
        w = w_ref[...]
        wb_ref[...] = (w if scale == 1.0 else w * scale).astype(wb_ref.dtype)

    i = pl.program_id(2)
    lane = lax.broadcasted_iota(jnp.int32, (1, LANES), 1)
    feat = lax.broadcasted_iota(jnp.int32, (PAIR, 1), 0)

    @pl.when((pl.program_id(1) == 0) & (i == 0))
    def _():
        k_chunk = lax.broadcasted_iota(jnp.int32, (tq, tq), 0) // CHUNK
        q_chunk = lax.broadcasted_iota(jnp.int32, (tq, tq), 1) // CHUNK
        for which in range(3):
            back = which * (tq // CHUNK) + q_chunk - k_chunk
            vis = (back >= 0) & (back <= LEFT_CHUNKS)
            for hd in range(2 * pp):
                g = jnp.broadcast_to(g_ref[0, which, hd], (tq, 2 * tq))
                toep = pltpu.roll(g, 0, 1, stride=1, stride_axis=0)[:, :tq]
                bias_ref[hd, (2 - which) * tq:(3 - which) * tq, :] = jnp.where(vis, toep, NEG)

    def attend(first_step):
        units = [(pi, blk, hh) for pi in range(pp) for blk in range(rbs) for hh in range(2)]

        def key_span(blk):
            if first_step:
                n_blocks = min(3, blk + 1)
                return (blk - (n_blocks - 1)) * tq, n_blocks * tq
            return pl.multiple_of((i * rbs + blk - 2) * tq, tq), 3 * tq

        def logits(pi, blk, hh):
            cols = slice(pi * PAIR, (pi + 1) * PAIR)
            ks, span = key_span(blk)
            q = q_ref[0, blk * tq:(blk + 1) * tq, cols]
            qh = jnp.where(_head_mask(lane, hh), q, jnp.zeros_like(q))
            return _dot_nt(k_ref[0, pl.ds(ks, span), cols], qh) + bias_ref[2 * pi + hh, 3 * tq - span:, :]

        def weighted_values(pi, blk, hh, p):
            ks, span = key_span(blk)
            return _dot(_value_aug_t(vt_ref[0, pi * PAIR:(pi + 1) * PAIR, pl.ds(ks, span)], feat, hh), p)

        ss, ps, accs = {}, {}, {}
        n_units = len(units)
        for t in range(0, n_units + 4, 2):
            for u in (t, t + 1):
                if u < n_units:
                    ss[u] = logits(*units[u])
            for u in (t - 2, t - 1):
                if 0 <= u < n_units:
                    ps[u] = _softmax_t([ss.pop(u)])[0]
            for u in (t - 4, t - 3):
                if 0 <= u < n_units:
                    accs[units[u]] = weighted_values(*units[u], ps.pop(u))
            if 0 <= t - 4 < n_units:
                pi, blk, _ = units[t - 4]
                o_ref[0, blk * tq:(blk + 1) * tq, pi * PAIR:(pi + 1) * PAIR] = _pair_output_t(
                    accs.pop((pi, blk, 0)), accs.pop((pi, blk, 1)), feat).astype(o_ref.dtype)

    pl.when(i == 0)(functools.partial(attend, True))
    pl.when(i > 0)(functools.partial(attend, False))


def _chunk_bias_rows(rel_table, tq, pp):
    heads = rel_table.shape[0]
    m = np.arange(2 * tq)
    offset = np.where(m < tq, m, m - 2 * tq)
    idx = np.stack([np.clip(which * tq + offset, -MAX_REL, MAX_REL) + MAX_REL for which in range(3)])
    rows = rel_table.astype(F32)[:, idx] * LOG2E
    return rows.reshape(heads // (2 * pp), 2 * pp, 3, 1, 2 * tq).transpose(0, 2, 1, 3, 4)


def _chunk(proj, vt, bias_rows, weights, scales, tq, rbs, pp, col0, row0):
    B, S, _ = proj.shape
    groups = bias_rows.shape[0]
    w = pp * PAIR
    steps = S // (rbs * tq)
    assert tq % CHUNK == 0 and LEFT_CHUNKS * CHUNK <= 2 * tq, "band must fit in three key blocks"
    assert rbs >= 2, "after the first step every query block must have two key blocks behind it"
    assert groups == 1, "each weight slice is cast at exactly one grid step"
    slice_spec = lambda wt: pl.BlockSpec((wt.shape[0] // (B * steps), wt.shape[1]),
                                         lambda g, b, i: (b * steps + i, 0))
    out = pl.pallas_call(
        functools.partial(_chunk_kernel, tq=tq, rbs=rbs, pp=pp, scales=tuple(scales)),
        grid=(groups, B, steps),
        in_specs=[
            pl.BlockSpec((1, rbs * tq, w), lambda g, b, i: (b, i, col0 + g)),
            pl.BlockSpec((1, S, w), lambda g, b, i: (b, 0, col0 + groups + g)),
            pl.BlockSpec((1, w, S), lambda g, b, i: (b, row0 + g, 0)),
            pl.BlockSpec((1, 3, 2 * pp, 1, 2 * tq), lambda g, b, i: (g, 0, 0, 0, 0)),
        ] + [slice_spec(wt) for wt in weights],
        out_specs=[pl.BlockSpec((1, rbs * tq, w), lambda g, b, i: (b, i, g))]
        + [slice_spec(wt) for wt in weights],
        out_shape=[jax.ShapeDtypeStruct((B, S, groups * w), BF16)]
        + [jax.ShapeDtypeStruct(wt.shape, BF16) for wt in weights],
        scratch_shapes=[pltpu.VMEM((2 * pp, 3 * tq, tq), F32)],
        compiler_params=pltpu.CompilerParams(
            dimension_semantics=("arbitrary", "arbitrary", "arbitrary"), vmem_limit_bytes=VMEM_LIMIT),
        name="chunk_attn",
    )(proj, proj, vt, bias_rows, *weights)
    return out[0], out[1:]


def _mix_mem_kernel(yf_ref, yc_ref, x_ref, gf_ref, gc_ref, wo_ref, gpost_ref, gpre_ref, wq_ref,
                    mem_ref, gkv_ref, wk_ref, wv_ref, wmo_ref, gmpost_ref, o_ref, mk_ref, mv_ref, *, sub):
    @pl.when(pl.program_id(1) == 0)
    def _():
        mb = _rms(mem_ref[0], gkv_ref[...]).astype(BF16)
        mk_ref[...] = _dot(mb, wk_ref[...]).astype(BF16)
        mv_ref[...] = _dot(mb, wv_ref[...]).astype(BF16)

    d_fox = yf_ref.shape[2]
    halves = range(x_ref.shape[1] // sub)
    rows = [slice(h * sub, (h + 1) * sub) for h in halves]

    def mix(r):
        yf = _rms(yf_ref[0, r, :].astype(F32), gf_ref[...]).astype(BF16)
        yc = _rms(yc_ref[0, r, :].astype(F32), gc_ref[...]).astype(BF16)
        return _dot(yf, wo_ref[0:d_fox, :]) + _dot(yc, wo_ref[d_fox:, :])

    def query(x1):
        return _dot(_rms(x1, gpre_ref[...]).astype(BF16), wq_ref[...]).astype(BF16)

    def attend(q):
        dh = q.shape[1] // MEM_HEADS
        outs = []
        for hh in range(MEM_HEADS):
            sl = slice(hh * dh, (hh + 1) * dh)
            s = _dot_nt(q[:, sl], mk_ref[:, sl])
            pr = jnp.exp2(s - jnp.max(s, axis=-1, keepdims=True))
            inv = 1.0 / jnp.sum(pr, axis=-1, keepdims=True)
            outs.append((_dot(pr.astype(BF16), mv_ref[:, sl]) * inv).astype(BF16))
        return jnp.concatenate(outs, axis=-1)

    ys = [mix(rows[h]) for h in halves]
    x1s = [x_ref[0, rows[h], :] + _rms(ys[h], gpost_ref[...]) for h in halves]
    qs = [query(x1s[h]) for h in halves]
    os = [attend(qs[h]) for h in halves]
    y2s = [_dot(os[h], wmo_ref[...]) for h in halves]
    for h in halves:
        o_ref[0, rows[h], :] = x1s[h] + _rms(y2s[h], gmpost_ref[...])


def _mix_mem(yf, yc, x, gf, gc, wo, gpost, gpre, wq, mem, gkv, wk, wv, wmo, gmpost, tm, sub):
    B, S, D = x.shape
    M = mem.shape[1]
    dg = yf.shape[2]
    vec = lambda n: pl.BlockSpec((1, n), lambda b, i: (0, 0))
    mat = lambda r, c: pl.BlockSpec((r, c), lambda b, i: (0, 0), pipeline_mode=pl.Buffered(1))
    return pl.pallas_call(
        functools.partial(_mix_mem_kernel, sub=sub),
        grid=(B, S // tm),
        in_specs=[
            pl.BlockSpec((1, tm, dg), lambda b, i: (b, i, 0)),
            pl.BlockSpec((1, tm, dg), lambda b, i: (b, i, 0)),
            pl.BlockSpec((1, tm, D), lambda b, i: (b, i, 0)),
            vec(dg), vec(dg), mat(2 * dg, D), vec(D),
            vec(D), mat(D, D),
            pl.BlockSpec((1, M, D), lambda b, i: (b, 0, 0)),
            vec(D), mat(D, D), mat(D, D),
            mat(D, D), vec(D),
        ],
        out_specs=pl.BlockSpec((1, tm, D), lambda b, i: (b, i, 0)),
        out_shape=jax.ShapeDtypeStruct((B, S, D), F32),
        scratch_shapes=[pltpu.VMEM((M, D), BF16), pltpu.VMEM((M, D), BF16)],
        compiler_params=pltpu.CompilerParams(
            dimension_semantics=("arbitrary", "arbitrary"), vmem_limit_bytes=VMEM_LIMIT),
        name="mix_mem",
    )(yf, yc, x, gf, gc, wo, gpost, gpre, wq, mem, gkv, wk, wv, wmo, gmpost)


def _mlp_kernel(x_ref, gpre_ref, w1_ref, w2_ref, gpost_ref, o_ref, *, ff_block, sub):
    halves = range(x_ref.shape[0] // sub)
    rows = [slice(h * sub, (h + 1) * sub) for h in halves]
    hs = [_rms(x_ref[rows[h], :], gpre_ref[...]).astype(BF16) for h in halves]
    ys = [None for _ in halves]
    for n in range(0, w1_ref.shape[1], ff_block):
        for h in halves:
            a = jnp.maximum(_dot(hs[h], w1_ref[:, n:n + ff_block]), 0.0)
            part = _dot((a * a).astype(BF16), w2_ref[n:n + ff_block, :])
            ys[h] = part if ys[h] is None else ys[h] + part
    for h in halves:
        o_ref[rows[h], :] = x_ref[rows[h], :] + _rms(ys[h], gpost_ref[...])


def _mlp(x, gpre, w1, w2, gpost, tm, ff_block, sub):
    R, D = x.shape
    d_ff = w1.shape[1]
    return pl.pallas_call(
        functools.partial(_mlp_kernel, ff_block=ff_block, sub=sub),
        grid=(R // tm,),
        in_specs=[
            pl.BlockSpec((tm, D), lambda i: (i, 0)),
            pl.BlockSpec((1, D), lambda i: (0, 0)),
            pl.BlockSpec((D, d_ff), lambda i: (0, 0), pipeline_mode=pl.Buffered(1)),
            pl.BlockSpec((d_ff, D), lambda i: (0, 0), pipeline_mode=pl.Buffered(1)),
            pl.BlockSpec((1, D), lambda i: (0, 0)),
        ],
        out_specs=pl.BlockSpec((tm, D), lambda i: (i, 0)),
        out_shape=jax.ShapeDtypeStruct((R, D), F32),
        compiler_params=pltpu.CompilerParams(
            dimension_semantics=("arbitrary",), vmem_limit_bytes=VMEM_LIMIT),
        name="mlp",
    )(x, gpre, w1, w2, gpost)


def kernel(x, mem, w_in, b_fgt, rel_bias, g_fox_out, g_chk_out, w_out, g_mix_pre, g_mix_post,
           g_mem_kv, w_mq, w_mk, w_mv, w_mo, g_mem_pre, g_mem_post,
           w_ff1, w_ff2, g_ff_pre, g_ff_post):
    B, S, D = x.shape
    depth = w_in.shape[0]
    heads = b_fgt.shape[1]
    d_fox = heads * HEAD_DIM
    d_chk = rel_bias.shape[1] * HEAD_DIM
    assert d_fox == d_chk and heads % 2 == 0 and heads <= LANES
    pairs = heads // 2
    row = lambda v: v.reshape(1, -1)

    for l in range(depth):
        wt_all = _wprep(jnp.swapaxes(w_in, 1, 2), l, d_fox, d_chk, heads,
                        scale=HEAD_DIM ** -0.5 * LOG2E, tc=256)
        b_f = jnp.pad(jnp.tile(b_fgt[l], PIECES), (0, LANES - PIECES * heads)).reshape(1, LANES)

        proj, vt, qa, ka = _inproj(x, row(g_mix_pre[l]), wt_all, b_f, heads,
                                   n_val=d_fox + d_chk, tm=1024, sub=256)
        fox_pp, chk_pp = 2, 4
        y_fox = _fox(proj, vt, qa, ka, heads, tq=1024, rb=256, pp=fox_pp)
        mem_scale = (D // MEM_HEADS) ** -0.5 * LOG2E
        later = [w_out[l], w_mq[l], w_mk[l], w_mv[l], w_mo[l], w_ff1[l], w_ff2[l]]
        y_chk, (wo_b, wq_b, wk_b, wv_b, wmo_b, w1_b, w2_b) = _chunk(
            proj, vt, _chunk_bias_rows(rel_bias[l], 256, chk_pp), later,
            [1.0, mem_scale, 1.0, 1.0, 1.0, 1.0, 1.0], tq=256, rbs=2, pp=chk_pp,
            col0=2 * pairs // chk_pp, row0=pairs // chk_pp)

        x = _mix_mem(y_fox, y_chk, x, row(g_fox_out[l]), row(g_chk_out[l]), wo_b,
                     row(g_mix_post[l]), row(g_mem_pre[l]), wq_b,
                     mem, row(g_mem_kv[l]), wk_b, wv_b, wmo_b, row(g_mem_post[l]), tm=1024, sub=256)
        x = _mlp(x.reshape(B * S, D), row(g_ff_pre[l]), w1_b, w2_b,
                 row(g_ff_post[l]), tm=1024, ff_block=1024, sub=256).reshape(B, S, D)
    return x
```

```python
import functools
import math

import numpy as np
import jax
import jax.numpy as jnp
from jax import lax
from jax.experimental import pallas as pl
from jax.experimental.pallas import tpu as pltpu

F32 = jnp.float32
BF16 = jnp.bfloat16

EPS = 1e-6
HEAD_DIM = 64
PAIR = 2 * HEAD_DIM
CHUNK = 64
LEFT_CHUNKS = 8
MAX_REL = 128
MEM_HEADS = 4
NEG = -1e30
LOG2E = math.log2(math.e)
LANES = 128
VMEM_LIMIT = 56 * 1024 * 1024
PIECES = 3
SUM_LANE = (HEAD_DIM, 0)


def _rms(x, g):
    return x * lax.rsqrt(jnp.mean(x * x, axis=-1, keepdims=True) + EPS) * g


def _dot(a, b):
    return jnp.dot(a, b, preferred_element_type=F32)


def _dot_nt(a, b):
    return lax.dot_general(a, b, (((1,), (1,)), ((), ())), preferred_element_type=F32)


def _split3(x):
    hi = x.astype(BF16)
    r1 = x - hi.astype(F32)
    mid = r1.astype(BF16)
    lo = (r1 - mid.astype(F32)).astype(BF16)
    return hi, mid, lo


def _wprep_kernel(wt_ref, o_ref, *, d_fox, d_chk, heads, scale):
    o_f = 3 * d_fox
    o_c = o_f + heads
    row = lambda lo, n: wt_ref[0, lo:lo + n, :]
    gates = row(o_f, heads)
    pad = jnp.zeros((LANES - PIECES * heads, gates.shape[1]), gates.dtype)
    o_ref[...] = jnp.concatenate(
        [row(0, d_fox) * scale, row(d_fox, d_fox), row(o_c, d_chk) * scale, row(o_c + d_chk, d_chk)]
        + [gates] * PIECES + [pad, row(2 * d_fox, d_fox), row(o_c + 2 * d_chk, d_chk)],
        axis=0).astype(o_ref.dtype)


def _wprep(wt, layer, d_fox, d_chk, heads, scale, tc):
    _, n_in, D = wt.shape
    n_out = 3 * (d_fox + d_chk) + LANES
    assert heads % 8 == 0, "row pieces must stay sublane-tile aligned"
    return pl.pallas_call(
        functools.partial(_wprep_kernel, d_fox=d_fox, d_chk=d_chk, heads=heads, scale=scale),
        grid=(D // tc,),
        in_specs=[pl.BlockSpec((1, n_in, tc), lambda i: (layer, 0, i))],
        out_specs=pl.BlockSpec((n_out, tc), lambda i: (0, i)),
        out_shape=jax.ShapeDtypeStruct((n_out, D), BF16),
        compiler_params=pltpu.CompilerParams(dimension_semantics=("arbitrary",), vmem_limit_bytes=VMEM_LIMIT),
        name="w_prep",
    )(wt)


def _inproj_kernel(x_ref, g_ref, wt_ref, bf_ref, tri_ref, e_ref, ones_ref,
                   proj_ref, vt_ref, qa_ref, ka_ref, carry_ref, *, heads, sub):
    @pl.when(pl.program_id(1) == 0)
    def _():
        carry_ref[...] = jnp.zeros_like(carry_ref)

    n_main = proj_ref.shape[2]
    halves = range(x_ref.shape[1] // sub)
    rows = [slice(h * sub, (h + 1) * sub) for h in halves]
    lane = lax.broadcasted_iota(jnp.int32, (1, LANES), 1)
    hbs = [_rms(x_ref[0, rows[h], :], g_ref[...]).astype(BF16) for h in halves]
    fs = []
    for h in halves:
        res = _dot_nt(hbs[h], wt_ref[0:n_main + LANES, :])
        proj_ref[0, rows[h], :] = res[:, :n_main].astype(BF16)
        vt_ref[0, :, rows[h]] = _dot_nt(wt_ref[n_main + LANES:, :], hbs[h]).astype(BF16)
        fs.append(res[:, n_main:] + bf_ref[...])
    carry = carry_ref[...]
    for h in halves:
        f = fs[h]
        logf = jnp.minimum(f, 0.0) - jnp.log1p(jnp.exp(-jnp.abs(f)))
        hi, mid, lo = _split3(logf)
        packed = jnp.where(lane < heads, hi, jnp.where(lane < 2 * heads, mid, lo))
        cum = _dot(tri_ref[...], packed)
        c = (cum + pltpu.roll(cum, LANES - heads, 1) + pltpu.roll(cum, LANES - 2 * heads, 1)) + carry
        carry = c[sub - 1:sub, :]
        pieces = jnp.concatenate(_split3(c * LOG2E), axis=-1)
        sel = _dot(pieces, e_ref[...])
        qa_ref[0, rows[h], :] = (sel[:, :LANES] + ones_ref[0:1, :]).astype(BF16)
        ka_ref[0, rows[h], :] = (ones_ref[1:2, :] - sel[:, LANES:]).astype(BF16)
    carry_ref[...] = carry


def _aug_constants(heads):
    k0 = PIECES * heads
    assert 2 * k0 <= LANES
    e = np.zeros((PIECES * LANES, 2 * LANES), np.float32)
    for h in range(heads):
        for r in range(PIECES):
            e[r * LANES + h, PIECES * h + r] = 1.0
            e[r * LANES + h, LANES + k0 + PIECES * h + r] = 1.0
    ones = np.zeros((2, LANES), np.float32)
    ones[0, k0:2 * k0] = 1.0
    ones[1, 0:k0] = 1.0
    return jnp.asarray(e, BF16), jnp.asarray(ones)


def _inproj(x, g, wt_all, b_f, heads, n_val, tm, sub):
    B, S, D = x.shape
    n_main = wt_all.shape[0] - n_val - LANES
    tri = jnp.asarray(np.tril(np.ones((sub, sub), np.float32)), BF16)
    e, ones = _aug_constants(heads)
    const = lambda r, c: pl.BlockSpec((r, c), lambda b, i: (0, 0), pipeline_mode=pl.Buffered(1))
    return pl.pallas_call(
        functools.partial(_inproj_kernel, heads=heads, sub=sub),
        grid=(B, S // tm),
        in_specs=[
            pl.BlockSpec((1, tm, D), lambda b, i: (b, i, 0)),
            const(1, D), const(n_main + LANES + n_val, D), const(1, LANES), const(sub, sub),
            const(PIECES * LANES, 2 * LANES), const(2, LANES),
        ],
        out_specs=[
            pl.BlockSpec((1, tm, n_main), lambda b, i: (b, i, 0)),
            pl.BlockSpec((1, n_val, tm), lambda b, i: (b, 0, i)),
            pl.BlockSpec((1, tm, LANES), lambda b, i: (b, i, 0)),
            pl.BlockSpec((1, tm, LANES), lambda b, i: (b, i, 0)),
        ],
        out_shape=[
            jax.ShapeDtypeStruct((B, S, n_main), BF16),
            jax.ShapeDtypeStruct((B, n_val, S), BF16),
            jax.ShapeDtypeStruct((B, S, LANES), BF16),
            jax.ShapeDtypeStruct((B, S, LANES), BF16),
        ],
        scratch_shapes=[pltpu.VMEM((1, LANES), F32)],
        compiler_params=pltpu.CompilerParams(
            dimension_semantics=("arbitrary", "arbitrary"), vmem_limit_bytes=VMEM_LIMIT),
        name="inproj",
    )(x, g, wt_all, b_f, tri, e, ones)


def _head_mask(lane, hh):
    return lane < HEAD_DIM if hh == 0 else lane >= HEAD_DIM


def _value_aug_t(vt, feat, hh):
    ones = jnp.where(feat == SUM_LANE[hh], 1.0, 0.0).astype(vt.dtype)
    return jnp.where(_head_mask(feat, hh), vt, ones)


def _softmax_t(ss):
    m = functools.reduce(jnp.maximum, [jnp.max(s, axis=0, keepdims=True) for s in ss])
    return [jnp.exp2(s - m).astype(BF16) for s in ss]


def _pair_output_t(acc0, acc1, feat):
    l0 = acc0[SUM_LANE[0]:SUM_LANE[0] + 1, :]
    l1 = acc1[SUM_LANE[1]:SUM_LANE[1] + 1, :]
    return jnp.where(_head_mask(feat, 0), acc0 * (1.0 / l0), acc1 * (1.0 / l1)).T


def _fox_kernel(q_ref, qa_ref, k_ref, ka_ref, vt_ref, o_ref, s_ref, p_ref, *, tq, rb, pp, heads):
    grp = pl.program_id(1)
    lane = lax.broadcasted_iota(jnp.int32, (1, LANES), 1)
    feat = lax.broadcasted_iota(jnp.int32, (PAIR, 1), 0)
    k0 = PIECES * heads
    causal = lax.broadcasted_iota(jnp.int32, (rb, rb), 0) <= lax.broadcasted_iota(jnp.int32, (rb, rb), 1)
    slots = s_ref.shape[0]

    def attend(n):
        units = [(pi, r, hh) for pi in range(pp) for r in range(tq // rb) for hh in range(2)]

        def keys(r):
            past = n * tq + r * rb
            return past, past + rb

        def logits_pair(u):
            pi, r, _ = units[u]
            past, total = keys(r)
            q = q_ref[0, r * rb:(r + 1) * rb, pi * PAIR:(pi + 1) * PAIR]
            qa = qa_ref[0, r * rb:(r + 1) * rb, :]
            q_augs = []
            for hh in range(2):
                a0 = PIECES * (2 * (grp * pp + pi) + hh)
                amask = ((lane >= a0) & (lane < a0 + PIECES)) | ((lane >= k0 + a0) & (lane < k0 + a0 + PIECES))
                q_augs.append(jnp.concatenate([jnp.where(_head_mask(lane, hh), q, jnp.zeros_like(q)),
                                               jnp.where(amask, qa, jnp.zeros_like(qa))], axis=-1))
            q_both = jnp.concatenate(q_augs, axis=0)
            m = [None, None]
            for lo, hi in ([(0, past)] if past else []) + [(past, total)]:
                kaug = jnp.concatenate([k_ref[0, lo:hi, pi * PAIR:(pi + 1) * PAIR], ka_ref[0, lo:hi, :]], axis=-1)
                s_both = _dot_nt(kaug, q_both)
                for hh in range(2):
                    s = s_both[:, hh * rb:(hh + 1) * rb]
                    if lo == past:
                        s = jnp.where(causal, s, NEG)
                    s_ref[(u + hh) % slots, lo:hi, :] = s
                    col_max = jnp.max(s, axis=0, keepdims=True)
                    m[hh] = col_max if m[hh] is None else jnp.maximum(m[hh], col_max)
            return m

        def weights(u, m):
            _, total = keys(units[u][1])
            p_ref[u % slots, 0:total, :] = jnp.exp2(s_ref[u % slots, 0:total, :] - m).astype(BF16)

        def values(u):
            pi, r, hh = units[u]
            _, total = keys(r)
            return _dot(_value_aug_t(vt_ref[0, pi * PAIR:(pi + 1) * PAIR, 0:total], feat, hh),
                        p_ref[u % slots, 0:total, :])

        ms, accs = {}, {}
        n_units = len(units)
        for t in range(0, n_units + 4, 2):
            if t < n_units:
                ms[t], ms[t + 1] = logits_pair(t)
            for u in (t - 2, t - 1):
                if 0 <= u < n_units:
                    weights(u, ms.pop(u))
            for u in (t - 4, t - 3):
                if 0 <= u < n_units:
                    accs[units[u]] = values(u)
            if 0 <= t - 4 < n_units:
                pi, r, _ = units[t - 4]
                o_ref[0, r * rb:(r + 1) * rb, pi * PAIR:(pi + 1) * PAIR] = _pair_output_t(
                    accs.pop((pi, r, 0)), accs.pop((pi, r, 1)), feat).astype(o_ref.dtype)

    for n in range(k_ref.shape[1] // tq):
        pl.when(pl.program_id(2) == n)(functools.partial(attend, n))


def _fox(proj, vt, qa, ka, heads, tq, rb, pp):
    B, S, _ = proj.shape
    groups = heads // 2 // pp
    w = pp * PAIR
    return pl.pallas_call(
        functools.partial(_fox_kernel, tq=tq, rb=rb, pp=pp, heads=heads),
        grid=(B, groups, S // tq),
        in_specs=[
            pl.BlockSpec((1, tq, w), lambda b, g, i: (b, i, g)),
            pl.BlockSpec((1, tq, LANES), lambda b, g, i: (b, i, 0)),
            pl.BlockSpec((1, S, w), lambda b, g, i: (b, 0, groups + g)),
            pl.BlockSpec((1, S, LANES), lambda b, g, i: (b, 0, 0)),
            pl.BlockSpec((1, w, S), lambda b, g, i: (b, g, 0)),
        ],
        out_specs=pl.BlockSpec((1, tq, w), lambda b, g, i: (b, i, g)),
        out_shape=jax.ShapeDtypeStruct((B, S, heads * HEAD_DIM), BF16),
        scratch_shapes=[pltpu.VMEM((6, S, rb), F32), pltpu.VMEM((6, S, rb), BF16)],
        compiler_params=pltpu.CompilerParams(
            dimension_semantics=("arbitrary", "arbitrary", "arbitrary"), vmem_limit_bytes=VMEM_LIMIT),
        name="fox_attn",
    )(proj, qa, proj, ka, vt)


def _chunk_kernel(q_ref, k_ref, vt_ref, g_ref, *refs, tq, rbs, pp, scales):
    n_w = len(scales)
    w_refs, o_ref, wb_refs, bias_ref = refs[:n_w], refs[n_w], refs[n_w + 1:2 * n_w + 1], refs[2 * n_w + 1]
    for w_ref, wb_ref, scale in zip(w_refs, wb_refs, scales):
        w = w_ref[...]
        wb_ref[...] = (w if scale == 1.0 else w * scale).astype(wb_ref.dtype)

    i = pl.program_id(2)
    lane = lax.broadcasted_iota(jnp.int32, (1, LANES), 1)
    feat = lax.broadcasted_iota(jnp.int32, (PAIR, 1), 0)

    @pl.when((pl.program_id(1) == 0) & (i == 0))
    def _():
        k_chunk = lax.broadcasted_iota(jnp.int32, (tq, tq), 0) // CHUNK
        q_chunk = lax.broadcasted_iota(jnp.int32, (tq, tq), 1) // CHUNK
        for which in range(3):
            back = which * (tq // CHUNK) + q_chunk - k_chunk
            vis = (back >= 0) & (back <= LEFT_CHUNKS)
            for hd in range(2 * pp):
                g = jnp.broadcast_to(g_ref[0, which, hd], (tq, 2 * tq))
                toep = pltpu.roll(g, 0, 1, stride=1, stride_axis=0)[:, :tq]
                bias_ref[hd, (2 - which) * tq:(3 - which) * tq, :] = jnp.where(vis, toep, NEG)

    def attend(first_step):
        units = [(pi, blk, hh) for pi in range(pp) for blk in range(rbs) for hh in range(2)]

        def key_span(blk):
            if first_step:
                n_blocks = min(3, blk + 1)
                return (blk - (n_blocks - 1)) * tq, n_blocks * tq
            return pl.multiple_of((i * rbs + blk - 2) * tq, tq), 3 * tq

        def logits(pi, blk, hh):
            cols = slice(pi * PAIR, (pi + 1) * PAIR)
            ks, span = key_span(blk)
            q = q_ref[0, blk * tq:(blk + 1) * tq, cols]
            qh = jnp.where(_head_mask(lane, hh), q, jnp.zeros_like(q))
            return _dot_nt(k_ref[0, pl.ds(ks, span), cols], qh) + bias_ref[2 * pi + hh, 3 * tq - span:, :]

        def weighted_values(pi, blk, hh, p):
            ks, span = key_span(blk)
            return _dot(_value_aug_t(vt_ref[0, pi * PAIR:(pi + 1) * PAIR, pl.ds(ks, span)], feat, hh), p)

        ss, ps, accs = {}, {}, {}
        n_units = len(units)
        for t in range(0, n_units + 4, 2):
            for u in (t, t + 1):
                if u < n_units:
                    ss[u] = logits(*units[u])
            for u in (t - 2, t - 1):
                if 0 <= u < n_units:
                    ps[u] = _softmax_t([ss.pop(u)])[0]
            for u in (t - 4, t - 3):
                if 0 <= u < n_units:
                    accs[units[u]] = weighted_values(*units[u], ps.pop(u))
            if 0 <= t - 4 < n_units:
                pi, blk, _ = units[t - 4]
                o_ref[0, blk * tq:(blk + 1) * tq, pi * PAIR:(pi + 1) * PAIR] = _pair_output_t(
                    accs.pop((pi, blk, 0)), accs.pop((pi, blk, 1)), feat).astype(o_ref.dtype)

    pl.when(i == 0)(functools.partial(attend, True))
    pl.when(i > 0)(functools.partial(attend, False))


def _chunk_bias_rows(rel_table, tq, pp):
    heads = rel_table.shape[0]
    m = np.arange(2 * tq)
    offset = np.where(m < tq, m, m - 2 * tq)
    idx = np.stack([np.clip(which * tq + offset, -MAX_REL, MAX_REL) + MAX_REL for which in range(3)])
    rows = rel_table.astype(F32)[:, idx] * LOG2E
    return rows.reshape(heads // (2 * pp), 2 * pp, 3, 1, 2 * tq).transpose(0, 2, 1, 3, 4)


def _chunk(proj, vt, bias_rows, weights, scales, tq, rbs, pp, col0, row0):
    B, S, _ = proj.shape
    groups = bias_rows.shape[0]
    w = pp * PAIR
    steps = S // (rbs * tq)
    assert tq % CHUNK == 0 and LEFT_CHUNKS * CHUNK <= 2 * tq, "band must fit in three key blocks"
    assert rbs >= 2, "after the first step every query block must have two key blocks behind it"
    assert groups == 1, "each weight slice is cast at exactly one grid step"
    slice_spec = lambda wt: pl.BlockSpec((wt.shape[0] // (B * steps), wt.shape[1]),
                                         lambda g, b, i: (b * steps + i, 0))
    out = pl.pallas_call(
        functools.partial(_chunk_kernel, tq=tq, rbs=rbs, pp=pp, scales=tuple(scales)),
        grid=(groups, B, steps),
        in_specs=[
            pl.BlockSpec((1, rbs * tq, w), lambda g, b, i: (b, i, col0 + g)),
            pl.BlockSpec((1, S, w), lambda g, b, i: (b, 0, col0 + groups + g)),
            pl.BlockSpec((1, w, S), lambda g, b, i: (b, row0 + g, 0)),
            pl.BlockSpec((1, 3, 2 * pp, 1, 2 * tq), lambda g, b, i: (g, 0, 0, 0, 0)),
        ] + [slice_spec(wt) for wt in weights],
        out_specs=[pl.BlockSpec((1, rbs * tq, w), lambda g, b, i: (b, i, g))]
        + [slice_spec(wt) for wt in weights],
        out_shape=[jax.ShapeDtypeStruct((B, S, groups * w), BF16)]
        + [jax.ShapeDtypeStruct(wt.shape, BF16) for wt in weights],
        scratch_shapes=[pltpu.VMEM((2 * pp, 3 * tq, tq), F32)],
        compiler_params=pltpu.CompilerParams(
            dimension_semantics=("arbitrary", "arbitrary", "arbitrary"), vmem_limit_bytes=VMEM_LIMIT),
        name="chunk_attn",
    )(proj, proj, vt, bias_rows, *weights)
    return out[0], out[1:]


def _mix_mem_kernel(yf_ref, yc_ref, x_ref, gf_ref, gc_ref, wo_ref, gpost_ref, gpre_ref, wq_ref,
                    mem_ref, gkv_ref, wk_ref, wv_ref, wmo_ref, gmpost_ref, o_ref, mk_ref, mv_ref, *, sub):
    @pl.when(pl.program_id(1) == 0)
    def _():
        mb = _rms(mem_ref[0], gkv_ref[...]).astype(BF16)
        mk_ref[...] = _dot(mb, wk_ref[...]).astype(BF16)
        mv_ref[...] = _dot(mb, wv_ref[...]).astype(BF16)

    d_fox = yf_ref.shape[2]
    halves = range(x_ref.shape[1] // sub)
    rows = [slice(h * sub, (h + 1) * sub) for h in halves]

    def mix(r):
        yf = _rms(yf_ref[0, r, :].astype(F32), gf_ref[...]).astype(BF16)
        yc = _rms(yc_ref[0, r, :].astype(F32), gc_ref[...]).astype(BF16)
        return _dot(yf, wo_ref[0:d_fox, :]) + _dot(yc, wo_ref[d_fox:, :])

    def query(x1):
        return _dot(_rms(x1, gpre_ref[...]).astype(BF16), wq_ref[...]).astype(BF16)

    def attend(q):
        dh = q.shape[1] // MEM_HEADS
        outs = []
        for hh in range(MEM_HEADS):
            sl = slice(hh * dh, (hh + 1) * dh)
            s = _dot_nt(q[:, sl], mk_ref[:, sl])
            pr = jnp.exp2(s - jnp.max(s, axis=-1, keepdims=True))
            inv = 1.0 / jnp.sum(pr, axis=-1, keepdims=True)
            outs.append((_dot(pr.astype(BF16), mv_ref[:, sl]) * inv).astype(BF16))
        return jnp.concatenate(outs, axis=-1)

    ys = [mix(rows[h]) for h in halves]
    x1s = [x_ref[0, rows[h], :] + _rms(ys[h], gpost_ref[...]) for h in halves]
    qs = [query(x1s[h]) for h in halves]
    os = [attend(qs[h]) for h in halves]
    y2s = [_dot(os[h], wmo_ref[...]) for h in halves]
    for h in halves:
        o_ref[0, rows[h], :] = x1s[h] + _rms(y2s[h], gmpost_ref[...])


def _mix_mem(yf, yc, x, gf, gc, wo, gpost, gpre, wq, mem, gkv, wk, wv, wmo, gmpost, tm, sub):
    B, S, D = x.shape
    M = mem.shape[1]
    dg = yf.shape[2]
    vec = lambda n: pl.BlockSpec((1, n), lambda b, i: (0, 0))
    mat = lambda r, c: pl.BlockSpec((r, c), lambda b, i: (0, 0), pipeline_mode=pl.Buffered(1))
    return pl.pallas_call(
        functools.partial(_mix_mem_kernel, sub=sub),
        grid=(B, S // tm),
        in_specs=[
            pl.BlockSpec((1, tm, dg), lambda b, i: (b, i, 0)),
            pl.BlockSpec((1, tm, dg), lambda b, i: (b, i, 0)),
            pl.BlockSpec((1, tm, D), lambda b, i: (b, i, 0)),
            vec(dg), vec(dg), mat(2 * dg, D), vec(D),
            vec(D), mat(D, D),
            pl.BlockSpec((1, M, D), lambda b, i: (b, 0, 0)),
            vec(D), mat(D, D), mat(D, D),
            mat(D, D), vec(D),
        ],
        out_specs=pl.BlockSpec((1, tm, D), lambda b, i: (b, i, 0)),
        out_shape=jax.ShapeDtypeStruct((B, S, D), F32),
        scratch_shapes=[pltpu.VMEM((M, D), BF16), pltpu.VMEM((M, D), BF16)],
        compiler_params=pltpu.CompilerParams(
            dimension_semantics=("arbitrary", "arbitrary"), vmem_limit_bytes=VMEM_LIMIT),
        name="mix_mem",
    )(yf, yc, x, gf, gc, wo, gpost, gpre, wq, mem, gkv, wk, wv, wmo, gmpost)


def _mlp_kernel(x_ref, gpre_ref, w1_ref, w2_ref, gpost_ref, o_ref, *, ff_block, sub):
    halves = range(x_ref.shape[0] // sub)
    rows = [slice(h * sub, (h + 1) * sub) for h in halves]
    hs = [_rms(x_ref[rows[h], :], gpre_ref[...]).astype(BF16) for h in halves]
    ys = [None for _ in halves]
    for n in range(0, w1_ref.shape[1], ff_block):
        for h in halves:
            a = jnp.maximum(_dot(hs[h], w1_ref[:, n:n + ff_block]), 0.0)
            part = _dot((a * a).astype(BF16), w2_ref[n:n + ff_block, :])
            ys[h] = part if ys[h] is None else ys[h] + part
    for h in halves:
        o_ref[rows[h], :] = x_ref[rows[h], :] + _rms(ys[h], gpost_ref[...])


def _mlp(x, gpre, w1, w2, gpost, tm, ff_block, sub):
    R, D = x.shape
    d_ff = w1.shape[1]
    return pl.pallas_call(
        functools.partial(_mlp_kernel, ff_block=ff_block, sub=sub),
        grid=(R // tm,),
        in_specs=[
            pl.BlockSpec((tm, D), lambda i: (i, 0)),
            pl.BlockSpec((1, D), lambda i: (0, 0)),
            pl.BlockSpec((D, d_ff), lambda i: (0, 0), pipeline_mode=pl.Buffered(1)),
            pl.BlockSpec((d_ff, D), lambda i: (0, 0), pipeline_mode=pl.Buffered(1)),
            pl.BlockSpec((1, D), lambda i: (0, 0)),
        ],
        out_specs=pl.BlockSpec((tm, D), lambda i: (i, 0)),
        out_shape=jax.ShapeDtypeStruct((R, D), F32),
        compiler_params=pltpu.CompilerParams(
            dimension_semantics=("arbitrary",), vmem_limit_bytes=VMEM_LIMIT),
        name="mlp",
    )(x, gpre, w1, w2, gpost)


def kernel(x, mem, w_in, b_fgt, rel_bias, g_fox_out, g_chk_out, w_out, g_mix_pre, g_mix_post,
           g_mem_kv, w_mq, w_mk, w_mv, w_mo, g_mem_pre, g_mem_post,
           w_ff1, w_ff2, g_ff_pre, g_ff_post):
    B, S, D = x.shape
    depth = w_in.shape[0]
    heads = b_fgt.shape[1]
    d_fox = heads * HEAD_DIM
    d_chk = rel_bias.shape[1] * HEAD_DIM
    assert d_fox == d_chk and heads % 2 == 0 and heads <= LANES
    pairs = heads // 2
    row = lambda v: v.reshape(1, -1)

    for l in range(depth):
        wt_all = _wprep(jnp.swapaxes(w_in, 1, 2), l, d_fox, d_chk, heads,
                        scale=HEAD_DIM ** -0.5 * LOG2E, tc=256)
        b_f = jnp.pad(jnp.tile(b_fgt[l], PIECES), (0, LANES - PIECES * heads)).reshape(1, LANES)

        proj, vt, qa, ka = _inproj(x, row(g_mix_pre[l]), wt_all, b_f, heads,
                                   n_val=d_fox + d_chk, tm=1024, sub=256)
        fox_pp, chk_pp = 4, 4
        y_fox = _fox(proj, vt, qa, ka, heads, tq=512, rb=256, pp=fox_pp)
        mem_scale = (D // MEM_HEADS) ** -0.5 * LOG2E
        later = [w_out[l], w_mq[l], w_mk[l], w_mv[l], w_mo[l], w_ff1[l], w_ff2[l]]
        y_chk, (wo_b, wq_b, wk_b, wv_b, wmo_b, w1_b, w2_b) = _chunk(
            proj, vt, _chunk_bias_rows(rel_bias[l], 256, chk_pp), later,
            [1.0, mem_scale, 1.0, 1.0, 1.0, 1.0, 1.0], tq=256, rbs=2, pp=chk_pp,
            col0=2 * pairs // chk_pp, row0=pairs // chk_pp)

        x = _mix_mem(y_fox, y_chk, x, row(g_fox_out[l]), row(g_chk_out[l]), wo_b,
                     row(g_mix_post[l]), row(g_mem_pre[l]), wq_b,
                     mem, row(g_mem_kv[l]), wk_b, wv_b, wmo_b, row(g_mem_post[l]), tm=1024, sub=256)
        x = _mlp(x.reshape(B * S, D), row(g_ff_pre[l]), w1_b, w2_b,
                 row(g_ff_post[l]), tm=1024, ff_block=1024, sub=256).reshape(B, S, D)
    return x
```

```python
import functools
import math

import numpy as np
import jax
import jax.numpy as jnp
from jax import lax
from jax.experimental import pallas as pl
from jax.experimental.pallas import tpu as pltpu

F32 = jnp.float32
BF16 = jnp.bfloat16

EPS = 1e-6
HEAD_DIM = 64
PAIR = 2 * HEAD_DIM
CHUNK = 64
LEFT_CHUNKS = 8
MAX_REL = 128
MEM_HEADS = 4
NEG = -1e30
LOG2E = math.log2(math.e)
LANES = 128
VMEM_LIMIT = 56 * 1024 * 1024
PIECES = 3
SUM_LANE = (HEAD_DIM, 0)


def _rms(x, g):
    return x * lax.rsqrt(jnp.mean(x * x, axis=-1, keepdims=True) + EPS) * g


def _dot(a, b):
    return jnp.dot(a, b, preferred_element_type=F32)


def _dot_nt(a, b):
    return lax.dot_general(a, b, (((1,), (1,)), ((), ())), preferred_element_type=F32)


def _split3(x):
    hi = x.astype(BF16)
    r1 = x - hi.astype(F32)
    mid = r1.astype(BF16)
    lo = (r1 - mid.astype(F32)).astype(BF16)
    return hi, mid, lo


def _wprep_kernel(wt_ref, o_ref, *, d_fox, d_chk, heads, scale):
    o_f = 3 * d_fox
    o_c = o_f + heads
    row = lambda lo, n: wt_ref[0, lo:lo + n, :]
    gates = row(o_f, heads)
    pad = jnp.zeros((LANES - PIECES * heads, gates.shape[1]), gates.dtype)
    o_ref[...] = jnp.concatenate(
        [row(0, d_fox) * scale, row(d_fox, d_fox), row(o_c, d_chk) * scale, row(o_c + d_chk, d_chk)]
        + [gates] * PIECES + [pad, row(2 * d_fox, d_fox), row(o_c + 2 * d_chk, d_chk)],
        axis=0).astype(o_ref.dtype)


def _wprep(wt, layer, d_fox, d_chk, heads, scale, tc):
    _, n_in, D = wt.shape
    n_out = 3 * (d_fox + d_chk) + LANES
    assert heads % 8 == 0, "row pieces must stay sublane-tile aligned"
    return pl.pallas_call(
        functools.partial(_wprep_kernel, d_fox=d_fox, d_chk=d_chk, heads=heads, scale=scale),
        grid=(D // tc,),
        in_specs=[pl.BlockSpec((1, n_in, tc), lambda i: (layer, 0, i))],
        out_specs=pl.BlockSpec((n_out, tc), lambda i: (0, i)),
        out_shape=jax.ShapeDtypeStruct((n_out, D), BF16),
        compiler_params=pltpu.CompilerParams(dimension_semantics=("arbitrary",), vmem_limit_bytes=VMEM_LIMIT),
        name="w_prep",
    )(wt)


def _inproj_kernel(x_ref, g_ref, wt_ref, bf_ref, tri_ref, e_ref, ones_ref,
                   proj_ref, vt_ref, qa_ref, ka_ref, carry_ref, *, heads, sub):
    @pl.when(pl.program_id(1) == 0)
    def _():
        carry_ref[...] = jnp.zeros_like(carry_ref)

    n_main = proj_ref.shape[2]
    halves = range(x_ref.shape[1] // sub)
    rows = [slice(h * sub, (h + 1) * sub) for h in halves]
    lane = lax.broadcasted_iota(jnp.int32, (1, LANES), 1)
    hbs = [_rms(x_ref[0, rows[h], :], g_ref[...]).astype(BF16) for h in halves]
    fs = []
    for h in halves:
        res = _dot_nt(hbs[h], wt_ref[0:n_main + LANES, :])
        proj_ref[0, rows[h], :] = res[:, :n_main].astype(BF16)
        vt_ref[0, :, rows[h]] = _dot_nt(wt_ref[n_main + LANES:, :], hbs[h]).astype(BF16)
        fs.append(res[:, n_main:] + bf_ref[...])
    carry = carry_ref[...]
    for h in halves:
        f = fs[h]
        logf = jnp.minimum(f, 0.0) - jnp.log1p(jnp.exp(-jnp.abs(f)))
        hi, mid, lo = _split3(logf)
        packed = jnp.where(lane < heads, hi, jnp.where(lane < 2 * heads, mid, lo))
        cum = _dot(tri_ref[...], packed)
        c = (cum + pltpu.roll(cum, LANES - heads, 1) + pltpu.roll(cum, LANES - 2 * heads, 1)) + carry
        carry = c[sub - 1:sub, :]
        pieces = jnp.concatenate(_split3(c * LOG2E), axis=-1)
        sel = _dot(pieces, e_ref[...])
        qa_ref[0, rows[h], :] = (sel[:, :LANES] + ones_ref[0:1, :]).astype(BF16)
        ka_ref[0, rows[h], :] = (ones_ref[1:2, :] - sel[:, LANES:]).astype(BF16)
    carry_ref[...] = carry


def _aug_constants(heads):
    k0 = PIECES * heads
    assert 2 * k0 <= LANES
    e = np.zeros((PIECES * LANES, 2 * LANES), np.float32)
    for h in range(heads):
        for r in range(PIECES):
            e[r * LANES + h, PIECES * h + r] = 1.0
            e[r * LANES + h, LANES + k0 + PIECES * h + r] = 1.0
    ones = np.zeros((2, LANES), np.float32)
    ones[0, k0:2 * k0] = 1.0
    ones[1, 0:k0] = 1.0
    return jnp.asarray(e, BF16), jnp.asarray(ones)


def _inproj(x, g, wt_all, b_f, heads, n_val, tm, sub):
    B, S, D = x.shape
    n_main = wt_all.shape[0] - n_val - LANES
    tri = jnp.asarray(np.tril(np.ones((sub, sub), np.float32)), BF16)
    e, ones = _aug_constants(heads)
    const = lambda r, c: pl.BlockSpec((r, c), lambda b, i: (0, 0), pipeline_mode=pl.Buffered(1))
    return pl.pallas_call(
        functools.partial(_inproj_kernel, heads=heads, sub=sub),
        grid=(B, S // tm),
        in_specs=[
            pl.BlockSpec((1, tm, D), lambda b, i: (b, i, 0)),
            const(1, D), const(n_main + LANES + n_val, D), const(1, LANES), const(sub, sub),
            const(PIECES * LANES, 2 * LANES), const(2, LANES),
        ],
        out_specs=[
            pl.BlockSpec((1, tm, n_main), lambda b, i: (b, i, 0)),
            pl.BlockSpec((1, n_val, tm), lambda b, i: (b, 0, i)),
            pl.BlockSpec((1, tm, LANES), lambda b, i: (b, i, 0)),
            pl.BlockSpec((1, tm, LANES), lambda b, i: (b, i, 0)),
        ],
        out_shape=[
            jax.ShapeDtypeStruct((B, S, n_main), BF16),
            jax.ShapeDtypeStruct((B, n_val, S), BF16),
            jax.ShapeDtypeStruct((B, S, LANES), BF16),
            jax.ShapeDtypeStruct((B, S, LANES), BF16),
        ],
        scratch_shapes=[pltpu.VMEM((1, LANES), F32)],
        compiler_params=pltpu.CompilerParams(
            dimension_semantics=("arbitrary", "arbitrary"), vmem_limit_bytes=VMEM_LIMIT),
        name="inproj",
    )(x, g, wt_all, b_f, tri, e, ones)


def _head_mask(lane, hh):
    return lane < HEAD_DIM if hh == 0 else lane >= HEAD_DIM


def _value_aug_t(vt, feat, hh):
    ones = jnp.where(feat == SUM_LANE[hh], 1.0, 0.0).astype(vt.dtype)
    return jnp.where(_head_mask(feat, hh), vt, ones)


def _softmax_t(ss):
    m = functools.reduce(jnp.maximum, [jnp.max(s, axis=0, keepdims=True) for s in ss])
    return [jnp.exp2(s - m).astype(BF16) for s in ss]


def _pair_output_t(acc0, acc1, feat):
    l0 = acc0[SUM_LANE[0]:SUM_LANE[0] + 1, :]
    l1 = acc1[SUM_LANE[1]:SUM_LANE[1] + 1, :]
    return jnp.where(_head_mask(feat, 0), acc0 * (1.0 / l0), acc1 * (1.0 / l1)).T


def _fox_kernel(q_ref, qa_ref, k_ref, ka_ref, vt_ref, o_ref, s_ref, p_ref, *, tq, rb, pp, heads):
    grp = pl.program_id(1)
    lane = lax.broadcasted_iota(jnp.int32, (1, LANES), 1)
    feat = lax.broadcasted_iota(jnp.int32, (PAIR, 1), 0)
    k0 = PIECES * heads
    causal = lax.broadcasted_iota(jnp.int32, (rb, rb), 0) <= lax.broadcasted_iota(jnp.int32, (rb, rb), 1)
    slots = s_ref.shape[0]

    def attend(n):
        units = [(pi, r, hh) for pi in range(pp) for r in range(tq // rb) for hh in range(2)]

        def keys(r):
            past = n * tq + r * rb
            return past, past + rb

        def logits(u):
            pi, r, hh = units[u]
            past, total = keys(r)
            q = q_ref[0, r * rb:(r + 1) * rb, pi * PAIR:(pi + 1) * PAIR]
            qa = qa_ref[0, r * rb:(r + 1) * rb, :]
            a0 = PIECES * (2 * (grp * pp + pi) + hh)
            amask = ((lane >= a0) & (lane < a0 + PIECES)) | ((lane >= k0 + a0) & (lane < k0 + a0 + PIECES))
            q_aug = jnp.concatenate([jnp.where(_head_mask(lane, hh), q, jnp.zeros_like(q)),
                                     jnp.where(amask, qa, jnp.zeros_like(qa))], axis=-1)
            m = None
            for lo, hi in ([(0, past)] if past else []) + [(past, total)]:
                kaug = jnp.concatenate([k_ref[0, lo:hi, pi * PAIR:(pi + 1) * PAIR], ka_ref[0, lo:hi, :]], axis=-1)
                s = _dot_nt(kaug, q_aug)
                if lo == past:
                    s = jnp.where(causal, s, NEG)
                s_ref[u % slots, lo:hi, :] = s
                col_max = jnp.max(s, axis=0, keepdims=True)
                m = col_max if m is None else jnp.maximum(m, col_max)
            return m

        def weights(u, m):
            _, total = keys(units[u][1])
            p_ref[u % slots, 0:total, :] = jnp.exp2(s_ref[u % slots, 0:total, :] - m).astype(BF16)

        def values(u):
            pi, r, hh = units[u]
            _, total = keys(r)
            return _dot(_value_aug_t(vt_ref[0, pi * PAIR:(pi + 1) * PAIR, 0:total], feat, hh),
                        p_ref[u % slots, 0:total, :])

        ms, accs = {}, {}
        n_units = len(units)
        for t in range(0, n_units + 4, 2):
            for u in (t, t + 1):
                if u < n_units:
                    ms[u] = logits(u)
            for u in (t - 2, t - 1):
                if 0 <= u < n_units:
                    weights(u, ms.pop(u))
            for u in (t - 4, t - 3):
                if 0 <= u < n_units:
                    accs[units[u]] = values(u)
            if 0 <= t - 4 < n_units:
                pi, r, _ = units[t - 4]
                o_ref[0, r * rb:(r + 1) * rb, pi * PAIR:(pi + 1) * PAIR] = _pair_output_t(
                    accs.pop((pi, r, 0)), accs.pop((pi, r, 1)), feat).astype(o_ref.dtype)

    for n in range(k_ref.shape[1] // tq):
        pl.when(pl.program_id(2) == n)(functools.partial(attend, n))


def _fox(proj, vt, qa, ka, heads, tq, rb, pp):
    B, S, _ = proj.shape
    groups = heads // 2 // pp
    w = pp * PAIR
    return pl.pallas_call(
        functools.partial(_fox_kernel, tq=tq, rb=rb, pp=pp, heads=heads),
        grid=(B, groups, S // tq),
        in_specs=[
            pl.BlockSpec((1, tq, w), lambda b, g, i: (b, i, g)),
            pl.BlockSpec((1, tq, LANES), lambda b, g, i: (b, i, 0)),
            pl.BlockSpec((1, S, w), lambda b, g, i: (b, 0, groups + g)),
            pl.BlockSpec((1, S, LANES), lambda b, g, i: (b, 0, 0)),
            pl.BlockSpec((1, w, S), lambda b, g, i: (b, g, 0)),
        ],
        out_specs=pl.BlockSpec((1, tq, w), lambda b, g, i: (b, i, g)),
        out_shape=jax.ShapeDtypeStruct((B, S, heads * HEAD_DIM), BF16),
        scratch_shapes=[pltpu.VMEM((6, S, rb), F32), pltpu.VMEM((6, S, rb), BF16)],
        compiler_params=pltpu.CompilerParams(
            dimension_semantics=("arbitrary", "arbitrary", "arbitrary"), vmem_limit_bytes=VMEM_LIMIT),
        name="fox_attn",
    )(proj, qa, proj, ka, vt)


def _chunk_kernel(q_ref, k_ref, vt_ref, g_ref, *refs, tq, rbs, pp, scales):
    n_w = len(scales)
    w_refs, o_ref, wb_refs, bias_ref = refs[:n_w], refs[n_w], refs[n_w + 1:2 * n_w + 1], refs[2 * n_w + 1]
    for w_ref, wb_ref, scale in zip(w_refs, wb_refs, scales):
        w = w_ref[...]
        wb_ref[...] = (w if scale == 1.0 else w * scale).astype(wb_ref.dtype)

    i = pl.program_id(2)
    lane = lax.broadcasted_iota(jnp.int32, (1, LANES), 1)
    feat = lax.broadcasted_iota(jnp.int32, (PAIR, 1), 0)

    @pl.when((pl.program_id(1) == 0) & (i == 0))
    def _():
        k_chunk = lax.broadcasted_iota(jnp.int32, (tq, tq), 0) // CHUNK
        q_chunk = lax.broadcasted_iota(jnp.int32, (tq, tq), 1) // CHUNK
        for which in range(3):
            back = which * (tq // CHUNK) + q_chunk - k_chunk
            vis = (back >= 0) & (back <= LEFT_CHUNKS)
            for hd in range(2 * pp):
                g = jnp.broadcast_to(g_ref[0, which, hd], (tq, 2 * tq))
                toep = pltpu.roll(g, 0, 1, stride=1, stride_axis=0)[:, :tq]
                bias_ref[hd, (2 - which) * tq:(3 - which) * tq, :] = jnp.where(vis, toep, NEG)

    def attend(first_step):
        units = [(pi, blk, hh) for pi in range(pp) for blk in range(rbs) for hh in range(2)]

        def key_span(blk):
            if first_step:
                n_blocks = min(3, blk + 1)
                return (blk - (n_blocks - 1)) * tq, n_blocks * tq
            return pl.multiple_of((i * rbs + blk - 2) * tq, tq), 3 * tq

        def logits(pi, blk, hh):
            cols = slice(pi * PAIR, (pi + 1) * PAIR)
            ks, span = key_span(blk)
            q = q_ref[0, blk * tq:(blk + 1) * tq, cols]
            qh = jnp.where(_head_mask(lane, hh), q, jnp.zeros_like(q))
            return _dot_nt(k_ref[0, pl.ds(ks, span), cols], qh) + bias_ref[2 * pi + hh, 3 * tq - span:, :]

        def weighted_values(pi, blk, hh, p):
            ks, span = key_span(blk)
            return _dot(_value_aug_t(vt_ref[0, pi * PAIR:(pi + 1) * PAIR, pl.ds(ks, span)], feat, hh), p)

        ss, ps, accs = {}, {}, {}
        n_units = len(units)
        for t in range(0, n_units + 4, 2):
            for u in (t, t + 1):
                if u < n_units:
                    ss[u] = logits(*units[u])
            for u in (t - 2, t - 1):
                if 0 <= u < n_units:
                    ps[u] = _softmax_t([ss.pop(u)])[0]
            for u in (t - 4, t - 3):
                if 0 <= u < n_units:
                    accs[units[u]] = weighted_values(*units[u], ps.pop(u))
            if 0 <= t - 4 < n_units:
                pi, blk, _ = units[t - 4]
                o_ref[0, blk * tq:(blk + 1) * tq, pi * PAIR:(pi + 1) * PAIR] = _pair_output_t(
                    accs.pop((pi, blk, 0)), accs.pop((pi, blk, 1)), feat).astype(o_ref.dtype)

    pl.when(i == 0)(functools.partial(attend, True))
    pl.when(i > 0)(functools.partial(attend, False))


def _chunk_bias_rows(rel_table, tq, pp):
    heads = rel_table.shape[0]
    m = np.arange(2 * tq)
    offset = np.where(m < tq, m, m - 2 * tq)
    idx = np.stack([np.clip(which * tq + offset, -MAX_REL, MAX_REL) + MAX_REL for which in range(3)])
    rows = rel_table.astype(F32)[:, idx] * LOG2E
    return rows.reshape(heads // (2 * pp), 2 * pp, 3, 1, 2 * tq).transpose(0, 2, 1, 3, 4)


def _chunk(proj, vt, bias_rows, weights, scales, tq, rbs, pp, col0, row0):
    B, S, _ = proj.shape
    groups = bias_rows.shape[0]
    w = pp * PAIR
    steps = S // (rbs * tq)
    assert tq % CHUNK == 0 and LEFT_CHUNKS * CHUNK <= 2 * tq, "band must fit in three key blocks"
    assert rbs >= 2, "after the first step every query block must have two key blocks behind it"
    assert groups == 1, "each weight slice is cast at exactly one grid step"
    slice_spec = lambda wt: pl.BlockSpec((wt.shape[0] // (B * steps), wt.shape[1]),
                                         lambda g, b, i: (b * steps + i, 0))
    out = pl.pallas_call(
        functools.partial(_chunk_kernel, tq=tq, rbs=rbs, pp=pp, scales=tuple(scales)),
        grid=(groups, B, steps),
        in_specs=[
            pl.BlockSpec((1, rbs * tq, w), lambda g, b, i: (b, i, col0 + g)),
            pl.BlockSpec((1, S, w), lambda g, b, i: (b, 0, col0 + groups + g)),
            pl.BlockSpec((1, w, S), lambda g, b, i: (b, row0 + g, 0)),
            pl.BlockSpec((1, 3, 2 * pp, 1, 2 * tq), lambda g, b, i: (g, 0, 0, 0, 0)),
        ] + [slice_spec(wt) for wt in weights],
        out_specs=[pl.BlockSpec((1, rbs * tq, w), lambda g, b, i: (b, i, g))]
        + [slice_spec(wt) for wt in weights],
        out_shape=[jax.ShapeDtypeStruct((B, S, groups * w), BF16)]
        + [jax.ShapeDtypeStruct(wt.shape, BF16) for wt in weights],
        scratch_shapes=[pltpu.VMEM((2 * pp, 3 * tq, tq), F32)],
        compiler_params=pltpu.CompilerParams(
            dimension_semantics=("arbitrary", "arbitrary", "arbitrary"), vmem_limit_bytes=VMEM_LIMIT),
        name="chunk_attn",
    )(proj, proj, vt, bias_rows, *weights)
    return out[0], out[1:]


def _mix_mem_kernel(yf_ref, yc_ref, x_ref, gf_ref, gc_ref, wo_ref, gpost_ref, gpre_ref, wq_ref,
                    mem_ref, gkv_ref, wk_ref, wv_ref, wmo_ref, gmpost_ref, o_ref, mk_ref, mv_ref, *, sub):
    @pl.when(pl.program_id(1) == 0)
    def _():
        mb = _rms(mem_ref[0], gkv_ref[...]).astype(BF16)
        mk_ref[...] = _dot(mb, wk_ref[...]).astype(BF16)
        mv_ref[...] = _dot(mb, wv_ref[...]).astype(BF16)

    d_fox = yf_ref.shape[2]
    halves = range(x_ref.shape[1] // sub)
    rows = [slice(h * sub, (h + 1) * sub) for h in halves]

    def mix(r):
        yf = _rms(yf_ref[0, r, :].astype(F32), gf_ref[...]).astype(BF16)
        yc = _rms(yc_ref[0, r, :].astype(F32), gc_ref[...]).astype(BF16)
        return _dot(yf, wo_ref[0:d_fox, :]) + _dot(yc, wo_ref[d_fox:, :])

    def query(x1):
        return _dot(_rms(x1, gpre_ref[...]).astype(BF16), wq_ref[...]).astype(BF16)

    def attend_all(qs):
        dh = qs[0].shape[1] // MEM_HEADS
        units = [(h, hh) for h in halves for hh in range(MEM_HEADS)]
        ss, ps, outs = {}, {}, {h: [] for h in halves}
        for t in range(len(units) + 2):
            if t < len(units):
                h, hh = units[t]
                sl = slice(hh * dh, (hh + 1) * dh)
                ss[t] = _dot_nt(qs[h][:, sl], mk_ref[:, sl])
            if 0 <= t - 1 < len(units):
                s = ss.pop(t - 1)
                pr = jnp.exp2(s - jnp.max(s, axis=-1, keepdims=True))
                ps[t - 1] = (pr.astype(BF16), 1.0 / jnp.sum(pr, axis=-1, keepdims=True))
            if 0 <= t - 2 < len(units):
                h, hh = units[t - 2]
                sl = slice(hh * dh, (hh + 1) * dh)
                pr, inv = ps.pop(t - 2)
                outs[h].append((_dot(pr, mv_ref[:, sl]) * inv).astype(BF16))
        return [jnp.concatenate(outs[h], axis=-1) for h in halves]

    ys = [mix(rows[h]) for h in halves]
    x1s = [x_ref[0, rows[h], :] + _rms(ys[h], gpost_ref[...]) for h in halves]
    qs = [query(x1s[h]) for h in halves]
    os = attend_all(qs)
    y2s = [_dot(os[h], wmo_ref[...]) for h in halves]
    for h in halves:
        o_ref[0, rows[h], :] = x1s[h] + _rms(y2s[h], gmpost_ref[...])


def _mix_mem(yf, yc, x, gf, gc, wo, gpost, gpre, wq, mem, gkv, wk, wv, wmo, gmpost, tm, sub):
    B, S, D = x.shape
    M = mem.shape[1]
    dg = yf.shape[2]
    vec = lambda n: pl.BlockSpec((1, n), lambda b, i: (0, 0))
    mat = lambda r, c: pl.BlockSpec((r, c), lambda b, i: (0, 0), pipeline_mode=pl.Buffered(1))
    return pl.pallas_call(
        functools.partial(_mix_mem_kernel, sub=sub),
        grid=(B, S // tm),
        in_specs=[
            pl.BlockSpec((1, tm, dg), lambda b, i: (b, i, 0)),
            pl.BlockSpec((1, tm, dg), lambda b, i: (b, i, 0)),
            pl.BlockSpec((1, tm, D), lambda b, i: (b, i, 0)),
            vec(dg), vec(dg), mat(2 * dg, D), vec(D),
            vec(D), mat(D, D),
            pl.BlockSpec((1, M, D), lambda b, i: (b, 0, 0)),
            vec(D), mat(D, D), mat(D, D),
            mat(D, D), vec(D),
        ],
        out_specs=pl.BlockSpec((1, tm, D), lambda b, i: (b, i, 0)),
        out_shape=jax.ShapeDtypeStruct((B, S, D), F32),
        scratch_shapes=[pltpu.VMEM((M, D), BF16), pltpu.VMEM((M, D), BF16)],
        compiler_params=pltpu.CompilerParams(
            dimension_semantics=("arbitrary", "arbitrary"), vmem_limit_bytes=VMEM_LIMIT),
        name="mix_mem",
    )(yf, yc, x, gf, gc, wo, gpost, gpre, wq, mem, gkv, wk, wv, wmo, gmpost)


def _mlp_kernel(x_ref, gpre_ref, w1_ref, w2_ref, gpost_ref, o_ref, *, ff_block, sub):
    halves = range(x_ref.shape[0] // sub)
    rows = [slice(h * sub, (h + 1) * sub) for h in halves]
    hs = [_rms(x_ref[rows[h], :], gpre_ref[...]).astype(BF16) for h in halves]
    ys = [None for _ in halves]
    for n in range(0, w1_ref.shape[1], ff_block):
        for h in halves:
            a = jnp.maximum(_dot(hs[h], w1_ref[:, n:n + ff_block]), 0.0)
            part = _dot((a * a).astype(BF16), w2_ref[n:n + ff_block, :])
            ys[h] = part if ys[h] is None else ys[h] + part
    for h in halves:
        o_ref[rows[h], :] = x_ref[rows[h], :] + _rms(ys[h], gpost_ref[...])


def _mlp(x, gpre, w1, w2, gpost, tm, ff_block, sub):
    R, D = x.shape
    d_ff = w1.shape[1]
    return pl.pallas_call(
        functools.partial(_mlp_kernel, ff_block=ff_block, sub=sub),
        grid=(R // tm,),
        in_specs=[
            pl.BlockSpec((tm, D), lambda i: (i, 0)),
            pl.BlockSpec((1, D), lambda i: (0, 0)),
            pl.BlockSpec((D, d_ff), lambda i: (0, 0), pipeline_mode=pl.Buffered(1)),
            pl.BlockSpec((d_ff, D), lambda i: (0, 0), pipeline_mode=pl.Buffered(1)),
            pl.BlockSpec((1, D), lambda i: (0, 0)),
        ],
        out_specs=pl.BlockSpec((tm, D), lambda i: (i, 0)),
        out_shape=jax.ShapeDtypeStruct((R, D), F32),
        compiler_params=pltpu.CompilerParams(
            dimension_semantics=("arbitrary",), vmem_limit_bytes=VMEM_LIMIT),
        name="mlp",
    )(x, gpre, w1, w2, gpost)


def kernel(x, mem, w_in, b_fgt, rel_bias, g_fox_out, g_chk_out, w_out, g_mix_pre, g_mix_post,
           g_mem_kv, w_mq, w_mk, w_mv, w_mo, g_mem_pre, g_mem_post,
           w_ff1, w_ff2, g_ff_pre, g_ff_post):
    B, S, D = x.shape
    depth = w_in.shape[0]
    heads = b_fgt.shape[1]
    d_fox = heads * HEAD_DIM
    d_chk = rel_bias.shape[1] * HEAD_DIM
    assert d_fox == d_chk and heads % 2 == 0 and heads <= LANES
    pairs = heads // 2
    row = lambda v: v.reshape(1, -1)

    for l in range(depth):
        wt_all = _wprep(jnp.swapaxes(w_in, 1, 2), l, d_fox, d_chk, heads,
                        scale=HEAD_DIM ** -0.5 * LOG2E, tc=256)
        b_f = jnp.pad(jnp.tile(b_fgt[l], PIECES), (0, LANES - PIECES * heads)).reshape(1, LANES)

        proj, vt, qa, ka = _inproj(x, row(g_mix_pre[l]), wt_all, b_f, heads,
                                   n_val=d_fox + d_chk, tm=1024, sub=256)
        fox_pp, chk_pp = 4, 4
        y_fox = _fox(proj, vt, qa, ka, heads, tq=512, rb=256, pp=fox_pp)
        mem_scale = (D // MEM_HEADS) ** -0.5 * LOG2E
        later = [w_out[l], w_mq[l], w_mk[l], w_mv[l], w_mo[l], w_ff1[l], w_ff2[l]]
        y_chk, (wo_b, wq_b, wk_b, wv_b, wmo_b, w1_b, w2_b) = _chunk(
            proj, vt, _chunk_bias_rows(rel_bias[l], 256, chk_pp), later,
            [1.0, mem_scale, 1.0, 1.0, 1.0, 1.0, 1.0], tq=256, rbs=2, pp=chk_pp,
            col0=2 * pairs // chk_pp, row0=pairs // chk_pp)

        x = _mix_mem(y_fox, y_chk, x, row(g_fox_out[l]), row(g_chk_out[l]), wo_b,
                     row(g_mix_post[l]), row(g_mem_pre[l]), wq_b,
                     mem, row(g_mem_kv[l]), wk_b, wv_b, wmo_b, row(g_mem_post[l]), tm=1024, sub=256)
        x = _mlp(x.reshape(B * S, D), row(g_ff_pre[l]), w1_b, w2_b,
                 row(g_ff_post[l]), tm=1024, ff_block=1024, sub=256).reshape(B, S, D)
    return x
```

```python
import functools
import math

import numpy as np
import jax
import jax.numpy as jnp
from jax import lax
from jax.experimental import pallas as pl
from jax.experimental.pallas import tpu as pltpu

F32 = jnp.float32
BF16 = jnp.bfloat16

EPS = 1e-6
HEAD_DIM = 64
PAIR = 2 * HEAD_DIM
CHUNK = 64
LEFT_CHUNKS = 8
MAX_REL = 128
MEM_HEADS = 4
NEG = -1e30
LOG2E = math.log2(math.e)
LANES = 128
VMEM_LIMIT = 56 * 1024 * 1024
PIECES = 3
SUM_LANE = (HEAD_DIM, 0)


def _rms(x, g):
    return x * lax.rsqrt(jnp.mean(x * x, axis=-1, keepdims=True) + EPS) * g


def _dot(a, b):
    return jnp.dot(a, b, preferred_element_type=F32)


def _dot_nt(a, b):
    return lax.dot_general(a, b, (((1,), (1,)), ((), ())), preferred_element_type=F32)


def _split3(x):
    hi = x.astype(BF16)
    r1 = x - hi.astype(F32)
    mid = r1.astype(BF16)
    lo = (r1 - mid.astype(F32)).astype(BF16)
    return hi, mid, lo


def _wprep_kernel(wt_ref, o_ref, *, d_fox, d_chk, heads, scale):
    o_f = 3 * d_fox
    o_c = o_f + heads
    row = lambda lo, n: wt_ref[0, lo:lo + n, :]
    gates = row(o_f, heads)
    pad = jnp.zeros((LANES - PIECES * heads, gates.shape[1]), gates.dtype)
    o_ref[...] = jnp.concatenate(
        [row(0, d_fox) * scale, row(d_fox, d_fox), row(o_c, d_chk) * scale, row(o_c + d_chk, d_chk)]
        + [gates] * PIECES + [pad, row(2 * d_fox, d_fox), row(o_c + 2 * d_chk, d_chk)],
        axis=0).astype(o_ref.dtype)


def _wprep(wt, layer, d_fox, d_chk, heads, scale, tc):
    _, n_in, D = wt.shape
    n_out = 3 * (d_fox + d_chk) + LANES
    assert heads % 8 == 0, "row pieces must stay sublane-tile aligned"
    return pl.pallas_call(
        functools.partial(_wprep_kernel, d_fox=d_fox, d_chk=d_chk, heads=heads, scale=scale),
        grid=(D // tc,),
        in_specs=[pl.BlockSpec((1, n_in, tc), lambda i: (layer, 0, i))],
        out_specs=pl.BlockSpec((n_out, tc), lambda i: (0, i)),
        out_shape=jax.ShapeDtypeStruct((n_out, D), BF16),
        compiler_params=pltpu.CompilerParams(dimension_semantics=("arbitrary",), vmem_limit_bytes=VMEM_LIMIT),
        name="w_prep",
    )(wt)


def _inproj_kernel(x_ref, g_ref, wt_ref, bf_ref, tri_ref, e_ref, ones_ref,
                   proj_ref, vt_ref, qa_ref, ka_ref, carry_ref, *, heads, sub):
    @pl.when(pl.program_id(1) == 0)
    def _():
        carry_ref[...] = jnp.zeros_like(carry_ref)

    n_main = proj_ref.shape[2]
    halves = range(x_ref.shape[1] // sub)
    rows = [slice(h * sub, (h + 1) * sub) for h in halves]
    lane = lax.broadcasted_iota(jnp.int32, (1, LANES), 1)
    hbs = [_rms(x_ref[0, rows[h], :], g_ref[...]).astype(BF16) for h in halves]
    def project(h):
        res = _dot_nt(hbs[h], wt_ref[0:n_main + LANES, :])
        proj_ref[0, rows[h], :] = res[:, :n_main].astype(BF16)
        vt_ref[0, :, rows[h]] = _dot_nt(wt_ref[n_main + LANES:, :], hbs[h]).astype(BF16)
        return res[:, n_main:] + bf_ref[...]

    def gate(h, f, carry):
        logf = jnp.minimum(f, 0.0) - jnp.log1p(jnp.exp(-jnp.abs(f)))
        hi, mid, lo = _split3(logf)
        packed = jnp.where(lane < heads, hi, jnp.where(lane < 2 * heads, mid, lo))
        cum = _dot(tri_ref[...], packed)
        c = (cum + pltpu.roll(cum, LANES - heads, 1) + pltpu.roll(cum, LANES - 2 * heads, 1)) + carry
        carry = c[sub - 1:sub, :]
        pieces = jnp.concatenate(_split3(c * LOG2E), axis=-1)
        sel = _dot(pieces, e_ref[...])
        qa_ref[0, rows[h], :] = (sel[:, :LANES] + ones_ref[0:1, :]).astype(BF16)
        ka_ref[0, rows[h], :] = (ones_ref[1:2, :] - sel[:, LANES:]).astype(BF16)
        return carry

    carry = carry_ref[...]
    fs = {}
    for h in range(len(halves) + 1):
        if h < len(halves):
            fs[h] = project(h)
        if h >= 1:
            carry = gate(h - 1, fs.pop(h - 1), carry)
    carry_ref[...] = carry


def _aug_constants(heads):
    k0 = PIECES * heads
    assert 2 * k0 <= LANES
    e = np.zeros((PIECES * LANES, 2 * LANES), np.float32)
    for h in range(heads):
        for r in range(PIECES):
            e[r * LANES + h, PIECES * h + r] = 1.0
            e[r * LANES + h, LANES + k0 + PIECES * h + r] = 1.0
    ones = np.zeros((2, LANES), np.float32)
    ones[0, k0:2 * k0] = 1.0
    ones[1, 0:k0] = 1.0
    return jnp.asarray(e, BF16), jnp.asarray(ones)


def _inproj(x, g, wt_all, b_f, heads, n_val, tm, sub):
    B, S, D = x.shape
    n_main = wt_all.shape[0] - n_val - LANES
    tri = jnp.asarray(np.tril(np.ones((sub, sub), np.float32)), BF16)
    e, ones = _aug_constants(heads)
    const = lambda r, c: pl.BlockSpec((r, c), lambda b, i: (0, 0), pipeline_mode=pl.Buffered(1))
    return pl.pallas_call(
        functools.partial(_inproj_kernel, heads=heads, sub=sub),
        grid=(B, S // tm),
        in_specs=[
            pl.BlockSpec((1, tm, D), lambda b, i: (b, i, 0)),
            const(1, D), const(n_main + LANES + n_val, D), const(1, LANES), const(sub, sub),
            const(PIECES * LANES, 2 * LANES), const(2, LANES),
        ],
        out_specs=[
            pl.BlockSpec((1, tm, n_main), lambda b, i: (b, i, 0)),
            pl.BlockSpec((1, n_val, tm), lambda b, i: (b, 0, i)),
            pl.BlockSpec((1, tm, LANES), lambda b, i: (b, i, 0)),
            pl.BlockSpec((1, tm, LANES), lambda b, i: (b, i, 0)),
        ],
        out_shape=[
            jax.ShapeDtypeStruct((B, S, n_main), BF16),
            jax.ShapeDtypeStruct((B, n_val, S), BF16),
            jax.ShapeDtypeStruct((B, S, LANES), BF16),
            jax.ShapeDtypeStruct((B, S, LANES), BF16),
        ],
        scratch_shapes=[pltpu.VMEM((1, LANES), F32)],
        compiler_params=pltpu.CompilerParams(
            dimension_semantics=("arbitrary", "arbitrary"), vmem_limit_bytes=VMEM_LIMIT),
        name="inproj",
    )(x, g, wt_all, b_f, tri, e, ones)


def _head_mask(lane, hh):
    return lane < HEAD_DIM if hh == 0 else lane >= HEAD_DIM


def _value_aug_t(vt, feat, hh):
    ones = jnp.where(feat == SUM_LANE[hh], 1.0, 0.0).astype(vt.dtype)
    return jnp.where(_head_mask(feat, hh), vt, ones)


def _softmax_t(ss):
    m = functools.reduce(jnp.maximum, [jnp.max(s, axis=0, keepdims=True) for s in ss])
    return [jnp.exp2(s - m).astype(BF16) for s in ss]


def _pair_output_t(acc0, acc1, feat):
    l0 = acc0[SUM_LANE[0]:SUM_LANE[0] + 1, :]
    l1 = acc1[SUM_LANE[1]:SUM_LANE[1] + 1, :]
    return jnp.where(_head_mask(feat, 0), acc0 * (1.0 / l0), acc1 * (1.0 / l1)).T


def _fox_kernel(q_ref, qa_ref, k_ref, ka_ref, vt_ref, o_ref, s_ref, p_ref, *, tq, rb, pp, heads):
    grp = pl.program_id(1)
    lane = lax.broadcasted_iota(jnp.int32, (1, LANES), 1)
    feat = lax.broadcasted_iota(jnp.int32, (PAIR, 1), 0)
    k0 = PIECES * heads
    causal = lax.broadcasted_iota(jnp.int32, (rb, rb), 0) <= lax.broadcasted_iota(jnp.int32, (rb, rb), 1)
    slots = s_ref.shape[0]

    def attend(n):
        units = [(pi, r, hh) for pi in range(pp) for r in range(tq // rb) for hh in range(2)]

        def keys(r):
            past = n * tq + r * rb
            return past, past + rb

        def logits(u):
            pi, r, hh = units[u]
            past, total = keys(r)
            q = q_ref[0, r * rb:(r + 1) * rb, pi * PAIR:(pi + 1) * PAIR]
            qa = qa_ref[0, r * rb:(r + 1) * rb, :]
            a0 = PIECES * (2 * (grp * pp + pi) + hh)
            amask = ((lane >= a0) & (lane < a0 + PIECES)) | ((lane >= k0 + a0) & (lane < k0 + a0 + PIECES))
            q_aug = jnp.concatenate([jnp.where(_head_mask(lane, hh), q, jnp.zeros_like(q)),
                                     jnp.where(amask, qa, jnp.zeros_like(qa))], axis=-1)
            m = None
            for lo, hi in ([(0, past)] if past else []) + [(past, total)]:
                kaug = jnp.concatenate([k_ref[0, lo:hi, pi * PAIR:(pi + 1) * PAIR], ka_ref[0, lo:hi, :]], axis=-1)
                s = _dot_nt(kaug, q_aug)
                if lo == past:
                    s = jnp.where(causal, s, NEG)
                s_ref[u % slots, lo:hi, :] = s
                col_max = jnp.max(s, axis=0, keepdims=True)
                m = col_max if m is None else jnp.maximum(m, col_max)
            return m

        def weights(u, m):
            _, total = keys(units[u][1])
            p_ref[u % slots, 0:total, :] = jnp.exp2(s_ref[u % slots, 0:total, :] - m).astype(BF16)

        def values(u):
            pi, r, hh = units[u]
            _, total = keys(r)
            return _dot(_value_aug_t(vt_ref[0, pi * PAIR:(pi + 1) * PAIR, 0:total], feat, hh),
                        p_ref[u % slots, 0:total, :])

        ms, accs = {}, {}
        n_units = len(units)
        for t in range(0, n_units + 4, 2):
            for u in (t, t + 1):
                if u < n_units:
                    ms[u] = logits(u)
            for u in (t - 2, t - 1):
                if 0 <= u < n_units:
                    weights(u, ms.pop(u))
            for u in (t - 4, t - 3):
                if 0 <= u < n_units:
                    accs[units[u]] = values(u)
            if 0 <= t - 4 < n_units:
                pi, r, _ = units[t - 4]
                o_ref[0, r * rb:(r + 1) * rb, pi * PAIR:(pi + 1) * PAIR] = _pair_output_t(
                    accs.pop((pi, r, 0)), accs.pop((pi, r, 1)), feat).astype(o_ref.dtype)

    for n in range(k_ref.shape[1] // tq):
        pl.when(pl.program_id(2) == n)(functools.partial(attend, n))


def _fox(proj, vt, qa, ka, heads, tq, rb, pp):
    B, S, _ = proj.shape
    groups = heads // 2 // pp
    w = pp * PAIR
    return pl.pallas_call(
        functools.partial(_fox_kernel, tq=tq, rb=rb, pp=pp, heads=heads),
        grid=(B, groups, S // tq),
        in_specs=[
            pl.BlockSpec((1, tq, w), lambda b, g, i: (b, i, g)),
            pl.BlockSpec((1, tq, LANES), lambda b, g, i: (b, i, 0)),
            pl.BlockSpec((1, S, w), lambda b, g, i: (b, 0, groups + g)),
            pl.BlockSpec((1, S, LANES), lambda b, g, i: (b, 0, 0)),
            pl.BlockSpec((1, w, S), lambda b, g, i: (b, g, 0)),
        ],
        out_specs=pl.BlockSpec((1, tq, w), lambda b, g, i: (b, i, g)),
        out_shape=jax.ShapeDtypeStruct((B, S, heads * HEAD_DIM), BF16),
        scratch_shapes=[pltpu.VMEM((6, S, rb), F32), pltpu.VMEM((6, S, rb), BF16)],
        compiler_params=pltpu.CompilerParams(
            dimension_semantics=("arbitrary", "arbitrary", "arbitrary"), vmem_limit_bytes=VMEM_LIMIT),
        name="fox_attn",
    )(proj, qa, proj, ka, vt)


def _chunk_kernel(q_ref, k_ref, vt_ref, g_ref, *refs, tq, rbs, pp, scales):
    n_w = len(scales)
    w_refs, o_ref, wb_refs, bias_ref = refs[:n_w], refs[n_w], refs[n_w + 1:2 * n_w + 1], refs[2 * n_w + 1]
    for w_ref, wb_ref, scale in zip(w_refs, wb_refs, scales):
        w = w_ref[...]
        wb_ref[...] = (w if scale == 1.0 else w * scale).astype(wb_ref.dtype)

    i = pl.program_id(2)
    lane = lax.broadcasted_iota(jnp.int32, (1, LANES), 1)
    feat = lax.broadcasted_iota(jnp.int32, (PAIR, 1), 0)

    @pl.when((pl.program_id(1) == 0) & (i == 0))
    def _():
        k_chunk = lax.broadcasted_iota(jnp.int32, (tq, tq), 0) // CHUNK
        q_chunk = lax.broadcasted_iota(jnp.int32, (tq, tq), 1) // CHUNK
        for which in range(3):
            back = which * (tq // CHUNK) + q_chunk - k_chunk
            vis = (back >= 0) & (back <= LEFT_CHUNKS)
            for hd in range(2 * pp):
                g = jnp.broadcast_to(g_ref[0, which, hd], (tq, 2 * tq))
                toep = pltpu.roll(g, 0, 1, stride=1, stride_axis=0)[:, :tq]
                bias_ref[hd, (2 - which) * tq:(3 - which) * tq, :] = jnp.where(vis, toep, NEG)

    def attend(first_step):
        units = [(pi, blk, hh) for pi in range(pp) for blk in range(rbs) for hh in range(2)]

        def key_span(blk):
            if first_step:
                n_blocks = min(3, blk + 1)
                return (blk - (n_blocks - 1)) * tq, n_blocks * tq
            return pl.multiple_of((i * rbs + blk - 2) * tq, tq), 3 * tq

        def logits(pi, blk, hh):
            cols = slice(pi * PAIR, (pi + 1) * PAIR)
            ks, span = key_span(blk)
            q = q_ref[0, blk * tq:(blk + 1) * tq, cols]
            qh = jnp.where(_head_mask(lane, hh), q, jnp.zeros_like(q))
            return _dot_nt(k_ref[0, pl.ds(ks, span), cols], qh) + bias_ref[2 * pi + hh, 3 * tq - span:, :]

        def weighted_values(pi, blk, hh, p):
            ks, span = key_span(blk)
            return _dot(_value_aug_t(vt_ref[0, pi * PAIR:(pi + 1) * PAIR, pl.ds(ks, span)], feat, hh), p)

        ss, ps, accs = {}, {}, {}
        n_units = len(units)
        for t in range(0, n_units + 4, 2):
            for u in (t, t + 1):
                if u < n_units:
                    ss[u] = logits(*units[u])
            for u in (t - 2, t - 1):
                if 0 <= u < n_units:
                    ps[u] = _softmax_t([ss.pop(u)])[0]
            for u in (t - 4, t - 3):
                if 0 <= u < n_units:
                    accs[units[u]] = weighted_values(*units[u], ps.pop(u))
            if 0 <= t - 4 < n_units:
                pi, blk, _ = units[t - 4]
                o_ref[0, blk * tq:(blk + 1) * tq, pi * PAIR:(pi + 1) * PAIR] = _pair_output_t(
                    accs.pop((pi, blk, 0)), accs.pop((pi, blk, 1)), feat).astype(o_ref.dtype)

    pl.when(i == 0)(functools.partial(attend, True))
    pl.when(i > 0)(functools.partial(attend, False))


def _chunk_bias_rows(rel_table, tq, pp):
    heads = rel_table.shape[0]
    m = np.arange(2 * tq)
    offset = np.where(m < tq, m, m - 2 * tq)
    idx = np.stack([np.clip(which * tq + offset, -MAX_REL, MAX_REL) + MAX_REL for which in range(3)])
    rows = rel_table.astype(F32)[:, idx] * LOG2E
    return rows.reshape(heads // (2 * pp), 2 * pp, 3, 1, 2 * tq).transpose(0, 2, 1, 3, 4)


def _chunk(proj, vt, bias_rows, weights, scales, tq, rbs, pp, col0, row0):
    B, S, _ = proj.shape
    groups = bias_rows.shape[0]
    w = pp * PAIR
    steps = S // (rbs * tq)
    assert tq % CHUNK == 0 and LEFT_CHUNKS * CHUNK <= 2 * tq, "band must fit in three key blocks"
    assert rbs >= 2, "after the first step every query block must have two key blocks behind it"
    assert groups == 1, "each weight slice is cast at exactly one grid step"
    slice_spec = lambda wt: pl.BlockSpec((wt.shape[0] // (B * steps), wt.shape[1]),
                                         lambda g, b, i: (b * steps + i, 0))
    out = pl.pallas_call(
        functools.partial(_chunk_kernel, tq=tq, rbs=rbs, pp=pp, scales=tuple(scales)),
        grid=(groups, B, steps),
        in_specs=[
            pl.BlockSpec((1, rbs * tq, w), lambda g, b, i: (b, i, col0 + g)),
            pl.BlockSpec((1, S, w), lambda g, b, i: (b, 0, col0 + groups + g)),
            pl.BlockSpec((1, w, S), lambda g, b, i: (b, row0 + g, 0)),
            pl.BlockSpec((1, 3, 2 * pp, 1, 2 * tq), lambda g, b, i: (g, 0, 0, 0, 0)),
        ] + [slice_spec(wt) for wt in weights],
        out_specs=[pl.BlockSpec((1, rbs * tq, w), lambda g, b, i: (b, i, g))]
        + [slice_spec(wt) for wt in weights],
        out_shape=[jax.ShapeDtypeStruct((B, S, groups * w), BF16)]
        + [jax.ShapeDtypeStruct(wt.shape, BF16) for wt in weights],
        scratch_shapes=[pltpu.VMEM((2 * pp, 3 * tq, tq), F32)],
        compiler_params=pltpu.CompilerParams(
            dimension_semantics=("arbitrary", "arbitrary", "arbitrary"), vmem_limit_bytes=VMEM_LIMIT),
        name="chunk_attn",
    )(proj, proj, vt, bias_rows, *weights)
    return out[0], out[1:]


def _mix_mem_kernel(yf_ref, yc_ref, x_ref, gf_ref, gc_ref, wo_ref, gpost_ref, gpre_ref, wq_ref,
                    mem_ref, gkv_ref, wk_ref, wv_ref, wmo_ref, gmpost_ref, o_ref, mk_ref, mv_ref, *, sub):
    @pl.when(pl.program_id(1) == 0)
    def _():
        mb = _rms(mem_ref[0], gkv_ref[...]).astype(BF16)
        mk_ref[...] = _dot(mb, wk_ref[...]).astype(BF16)
        mv_ref[...] = _dot(mb, wv_ref[...]).astype(BF16)

    d_fox = yf_ref.shape[2]
    halves = range(x_ref.shape[1] // sub)
    rows = [slice(h * sub, (h + 1) * sub) for h in halves]

    def mix(r):
        yf = _rms(yf_ref[0, r, :].astype(F32), gf_ref[...]).astype(BF16)
        yc = _rms(yc_ref[0, r, :].astype(F32), gc_ref[...]).astype(BF16)
        return _dot(yf, wo_ref[0:d_fox, :]) + _dot(yc, wo_ref[d_fox:, :])

    def query(x1):
        return _dot(_rms(x1, gpre_ref[...]).astype(BF16), wq_ref[...]).astype(BF16)

    def attend_all(qs):
        dh = qs[0].shape[1] // MEM_HEADS
        units = [(h, hh) for h in halves for hh in range(MEM_HEADS)]
        ss, ps, outs = {}, {}, {h: [] for h in halves}
        for t in range(len(units) + 2):
            if t < len(units):
                h, hh = units[t]
                sl = slice(hh * dh, (hh + 1) * dh)
                ss[t] = _dot_nt(qs[h][:, sl], mk_ref[:, sl])
            if 0 <= t - 1 < len(units):
                s = ss.pop(t - 1)
                pr = jnp.exp2(s - jnp.max(s, axis=-1, keepdims=True))
                ps[t - 1] = (pr.astype(BF16), 1.0 / jnp.sum(pr, axis=-1, keepdims=True))
            if 0 <= t - 2 < len(units):
                h, hh = units[t - 2]
                sl = slice(hh * dh, (hh + 1) * dh)
                pr, inv = ps.pop(t - 2)
                outs[h].append((_dot(pr, mv_ref[:, sl]) * inv).astype(BF16))
        return [jnp.concatenate(outs[h], axis=-1) for h in halves]

    ys = [mix(rows[h]) for h in halves]
    x1s = [x_ref[0, rows[h], :] + _rms(ys[h], gpost_ref[...]) for h in halves]
    qs = [query(x1s[h]) for h in halves]
    os = attend_all(qs)
    y2s = [_dot(os[h], wmo_ref[...]) for h in halves]
    for h in halves:
        o_ref[0, rows[h], :] = x1s[h] + _rms(y2s[h], gmpost_ref[...])


def _mix_mem(yf, yc, x, gf, gc, wo, gpost, gpre, wq, mem, gkv, wk, wv, wmo, gmpost, tm, sub):
    B, S, D = x.shape
    M = mem.shape[1]
    dg = yf.shape[2]
    vec = lambda n: pl.BlockSpec((1, n), lambda b, i: (0, 0))
    mat = lambda r, c: pl.BlockSpec((r, c), lambda b, i: (0, 0), pipeline_mode=pl.Buffered(1))
    return pl.pallas_call(
        functools.partial(_mix_mem_kernel, sub=sub),
        grid=(B, S // tm),
        in_specs=[
            pl.BlockSpec((1, tm, dg), lambda b, i: (b, i, 0)),
            pl.BlockSpec((1, tm, dg), lambda b, i: (b, i, 0)),
            pl.BlockSpec((1, tm, D), lambda b, i: (b, i, 0)),
            vec(dg), vec(dg), mat(2 * dg, D), vec(D),
            vec(D), mat(D, D),
            pl.BlockSpec((1, M, D), lambda b, i: (b, 0, 0)),
            vec(D), mat(D, D), mat(D, D),
            mat(D, D), vec(D),
        ],
        out_specs=pl.BlockSpec((1, tm, D), lambda b, i: (b, i, 0)),
        out_shape=jax.ShapeDtypeStruct((B, S, D), F32),
        scratch_shapes=[pltpu.VMEM((M, D), BF16), pltpu.VMEM((M, D), BF16)],
        compiler_params=pltpu.CompilerParams(
            dimension_semantics=("arbitrary", "arbitrary"), vmem_limit_bytes=VMEM_LIMIT),
        name="mix_mem",
    )(yf, yc, x, gf, gc, wo, gpost, gpre, wq, mem, gkv, wk, wv, wmo, gmpost)


def _mlp_kernel(x_ref, gpre_ref, w1_ref, w2_ref, gpost_ref, o_ref, *, ff_block, sub):
    halves = range(x_ref.shape[0] // sub)
    rows = [slice(h * sub, (h + 1) * sub) for h in halves]
    hs = [_rms(x_ref[rows[h], :], gpre_ref[...]).astype(BF16) for h in halves]
    ys = [None for _ in halves]
    for n in range(0, w1_ref.shape[1], ff_block):
        for h in halves:
            a = jnp.maximum(_dot(hs[h], w1_ref[:, n:n + ff_block]), 0.0)
            part = _dot((a * a).astype(BF16), w2_ref[n:n + ff_block, :])
            ys[h] = part if ys[h] is None else ys[h] + part
    for h in halves:
        o_ref[rows[h], :] = x_ref[rows[h], :] + _rms(ys[h], gpost_ref[...])


def _mlp(x, gpre, w1, w2, gpost, tm, ff_block, sub):
    R, D = x.shape
    d_ff = w1.shape[1]
    return pl.pallas_call(
        functools.partial(_mlp_kernel, ff_block=ff_block, sub=sub),
        grid=(R // tm,),
        in_specs=[
            pl.BlockSpec((tm, D), lambda i: (i, 0)),
            pl.BlockSpec((1, D), lambda i: (0, 0)),
            pl.BlockSpec((D, d_ff), lambda i: (0, 0), pipeline_mode=pl.Buffered(1)),
            pl.BlockSpec((d_ff, D), lambda i: (0, 0), pipeline_mode=pl.Buffered(1)),
            pl.BlockSpec((1, D), lambda i: (0, 0)),
        ],
        out_specs=pl.BlockSpec((tm, D), lambda i: (i, 0)),
        out_shape=jax.ShapeDtypeStruct((R, D), F32),
        compiler_params=pltpu.CompilerParams(
            dimension_semantics=("arbitrary",), vmem_limit_bytes=VMEM_LIMIT),
        name="mlp",
    )(x, gpre, w1, w2, gpost)


def kernel(x, mem, w_in, b_fgt, rel_bias, g_fox_out, g_chk_out, w_out, g_mix_pre, g_mix_post,
           g_mem_kv, w_mq, w_mk, w_mv, w_mo, g_mem_pre, g_mem_post,
           w_ff1, w_ff2, g_ff_pre, g_ff_post):
    B, S, D = x.shape
    depth = w_in.shape[0]
    heads = b_fgt.shape[1]
    d_fox = heads * HEAD_DIM
    d_chk = rel_bias.shape[1] * HEAD_DIM
    assert d_fox == d_chk and heads % 2 == 0 and heads <= LANES
    pairs = heads // 2
    row = lambda v: v.reshape(1, -1)

    for l in range(depth):
        wt_all = _wprep(jnp.swapaxes(w_in, 1, 2), l, d_fox, d_chk, heads,
                        scale=HEAD_DIM ** -0.5 * LOG2E, tc=256)
        b_f = jnp.pad(jnp.tile(b_fgt[l], PIECES), (0, LANES - PIECES * heads)).reshape(1, LANES)

        proj, vt, qa, ka = _inproj(x, row(g_mix_pre[l]), wt_all, b_f, heads,
                                   n_val=d_fox + d_chk, tm=1024, sub=256)
        fox_pp, chk_pp = 4, 4
        y_fox = _fox(proj, vt, qa, ka, heads, tq=512, rb=256, pp=fox_pp)
        mem_scale = (D // MEM_HEADS) ** -0.5 * LOG2E
        later = [w_out[l], w_mq[l], w_mk[l], w_mv[l], w_mo[l], w_ff1[l], w_ff2[l]]
        y_chk, (wo_b, wq_b, wk_b, wv_b, wmo_b, w1_b, w2_b) = _chunk(
            proj, vt, _chunk_bias_rows(rel_bias[l], 256, chk_pp), later,
            [1.0, mem_scale, 1.0, 1.0, 1.0, 1.0, 1.0], tq=256, rbs=2, pp=chk_pp,
            col0=2 * pairs // chk_pp, row0=pairs // chk_pp)

        x = _mix_mem(y_fox, y_chk, x, row(g_fox_out[l]), row(g_chk_out[l]), wo_b,
                     row(g_mix_post[l]), row(g_mem_pre[l]), wq_b,
                     mem, row(g_mem_kv[l]), wk_b, wv_b, wmo_b, row(g_mem_post[l]), tm=1024, sub=256)
        x = _mlp(x.reshape(B * S, D), row(g_ff_pre[l]), w1_b, w2_b,
                 row(g_ff_post[l]), tm=1024, ff_block=1024, sub=256).reshape(B, S, D)
    return x
```
